```python
import math
import jax, jax.numpy as jnp
from jax import lax
import numpy as np

D_MODEL = 1024
BATCH = 8
SEQ = 4096
DEPTH = 1

CONV_CH = 512
CONV_K = 3
N_HEADS = 8
HEAD_DIM = 64
N_KV = 2
HPG = N_HEADS // N_KV
ATTN_W = N_HEADS * HEAD_DIM
KV_W = N_KV * HEAD_DIM
MIX_W = CONV_CH + ATTN_W
CMP_BLOCK = 32
CMP_STRIDE = 16
CMP_HIDDEN = 256
SEL_BLOCK = 64
SEL_TOP = 16
WINDOW = 512
Q_BLOCK = 64
N_GATES = 3 * N_HEADS
PEER_HEADS = 8
PEER_NKEYS = 128
PEER_EXPERTS = PEER_NKEYS * PEER_NKEYS
PEER_DKEY = 256
PEER_TOPK = 16
PEER_CHUNK = 128

EPS = 1e-6
NEG_INF = -1e30
FORCE = 1e4

COL_SIZES = [CONV_CH, CONV_CH, CONV_CH, ATTN_W, KV_W, KV_W, KV_W, KV_W, KV_W, KV_W, N_GATES]
IN_COLS = sum(COL_SIZES)
SPLIT_POINTS = list(np.cumsum(COL_SIZES)[:-1])

kernel_name = "hybrid_conv_nsa_peer_layer"


def rms_norm(x, g):
    x32 = x.astype(jnp.float32)
    y = x32 * lax.rsqrt(jnp.mean(x32 * x32, axis=-1, keepdims=True) + EPS)
    return (y * g.astype(jnp.float32)).astype(x.dtype)


def masked_softmax(s, valid):
    s = jnp.where(valid, s.astype(jnp.float32), NEG_INF)
    return jax.nn.softmax(s, axis=-1)


def short_conv(z, w):
    c = z.shape[-1]
    return lax.conv_general_dilated(
        z, w[:, None, :].astype(z.dtype), window_strides=(1,),
        padding=[(CONV_K - 1, 0)], dimension_numbers=("NWC", "WIO", "NWC"),
        feature_group_count=c)


def compress_blocks(k, pos, w1, w2):
    b, s = k.shape[0], k.shape[1]
    n_cmp = (s - CMP_BLOCK) // CMP_STRIDE + 1
    idx = (jnp.arange(n_cmp) * CMP_STRIDE)[:, None] + jnp.arange(CMP_BLOCK)[None, :]
    blocks = k[:, idx] + pos[None, None, :, None, :]
    flat = blocks.transpose(0, 1, 3, 2, 4).reshape(b, n_cmp, N_KV, CMP_BLOCK * HEAD_DIM)
    return jax.nn.gelu(flat @ w1) @ w2


def nsa(q, kc, vc, ks, vs, kw, vw, gates):
    b, s = q.shape[0], q.shape[1]
    dt = q.dtype
    n_cmp = kc.shape[1]
    n_sel = s // SEL_BLOCK
    n_top = min(SEL_TOP, n_sel)
    scale = HEAD_DIM ** -0.5
    cmp_start = jnp.arange(n_cmp) * CMP_STRIDE
    cmp_end = cmp_start + CMP_BLOCK - 1
    sel_start = jnp.arange(n_sel) * SEL_BLOCK
    overlap = ((cmp_start[:, None] < sel_start[None, :] + SEL_BLOCK)
               & (cmp_start[:, None] + CMP_BLOCK > sel_start[None, :])).astype(jnp.float32)
    ks_blk = ks.reshape(b, n_sel, SEL_BLOCK, N_KV, HEAD_DIM).transpose(0, 3, 1, 2, 4)
    vs_blk = vs.reshape(b, n_sel, SEL_BLOCK, N_KV, HEAD_DIM).transpose(0, 3, 1, 2, 4)
    kw_pad = jnp.pad(kw, ((0, 0), (WINDOW, 0), (0, 0), (0, 0)))
    vw_pad = jnp.pad(vw, ((0, 0), (WINDOW, 0), (0, 0), (0, 0)))
    gather_blocks = jax.vmap(jax.vmap(lambda kb, ix: kb[ix]))
    j = jnp.arange(n_sel)

    def block(i):
        t0 = i * Q_BLOCK
        tq = t0 + jnp.arange(Q_BLOCK)
        qb = lax.dynamic_slice_in_dim(q, t0, Q_BLOCK, axis=1).reshape(b, Q_BLOCK, N_KV, HPG, HEAD_DIM)
        s_c = jnp.einsum('bqghd,bngd->bghqn', qb, kc).astype(jnp.float32) * scale
        valid_c = cmp_end[None, :] <= tq[:, None]
        has_c = jnp.any(valid_c, axis=-1).astype(jnp.float32)[:, None]
        p_c = masked_softmax(s_c, valid_c) * has_c
        o_c = jnp.einsum('bghqn,bngd->bqghd', p_c.astype(dt), vc)
        imp = jnp.einsum('bghqn,nj->bgqj', p_c, overlap)
        qblk = tq // SEL_BLOCK
        forced = (j[None, :] == 0) | (j[None, :] == qblk[:, None]) | (j[None, :] == qblk[:, None] - 1)
        imp = jnp.where(j[None, :] > qblk[:, None], -FORCE, imp + jnp.where(forced, FORCE, 0.0))
        _, sel = lax.top_k(imp, n_top)
        k_sel = gather_blocks(ks_blk, sel)
        v_sel = gather_blocks(vs_blk, sel).reshape(b, N_KV, Q_BLOCK, n_top * SEL_BLOCK, HEAD_DIM)
        kpos = sel[..., None] * SEL_BLOCK + jnp.arange(SEL_BLOCK)
        valid_s = (kpos <= tq[None, None, :, None, None]).reshape(b, N_KV, Q_BLOCK, n_top * SEL_BLOCK)
        s_s = jnp.einsum('bqghd,bgqnkd->bghqnk', qb, k_sel).reshape(
            b, N_KV, HPG, Q_BLOCK, n_top * SEL_BLOCK).astype(jnp.float32) * scale
        p_s = masked_softmax(s_s, valid_s[:, :, None])
        o_s = jnp.einsum('bghqm,bgqmd->bqghd', p_s.astype(dt), v_sel)
        kwb = lax.dynamic_slice_in_dim(kw_pad, t0, WINDOW + Q_BLOCK, axis=1)
        vwb = lax.dynamic_slice_in_dim(vw_pad, t0, WINDOW + Q_BLOCK, axis=1)
        tk = t0 - WINDOW + jnp.arange(WINDOW + Q_BLOCK)
        valid_w = (tk[None, :] <= tq[:, None]) & (tk[None, :] > tq[:, None] - WINDOW) & (tk[None, :] >= 0)
        s_w = jnp.einsum('bqghd,bkgd->bghqk', qb, kwb).astype(jnp.float32) * scale
        p_w = masked_softmax(s_w, valid_w)
        o_w = jnp.einsum('bghqk,bkgd->bqghd', p_w.astype(dt), vwb)
        g = lax.dynamic_slice_in_dim(gates, t0, Q_BLOCK, axis=1).reshape(b, Q_BLOCK, N_KV, HPG, 3).astype(dt)
        o = g[..., 0:1] * o_c + g[..., 1:2] * o_s + g[..., 2:3] * o_w
        return o.reshape(b, Q_BLOCK, ATTN_W)

    out = lax.map(block, jnp.arange(s // Q_BLOCK))
    return out.transpose(1, 0, 2, 3).reshape(b, s, ATTN_W)


def peer(xn, wq, subkeys, u, v):
    b, s, d = xn.shape
    t = b * s
    xt = xn.reshape(t, d)
    q = (xt @ wq).reshape(t, PEER_HEADS, 2, PEER_DKEY // 2)
    sc = jnp.einsum('thcd,hcnd->thcn', q, subkeys).astype(jnp.float32)
    sv, si = lax.top_k(sc, PEER_TOPK)
    cand = (sv[:, :, 0, :, None] + sv[:, :, 1, None, :]).reshape(t, PEER_HEADS, PEER_TOPK * PEER_TOPK)
    best, bi = lax.top_k(cand, PEER_TOPK)
    i1 = jnp.take_along_axis(si[:, :, 0], bi // PEER_TOPK, axis=-1)
    i2 = jnp.take_along_axis(si[:, :, 1], bi % PEER_TOPK, axis=-1)
    eidx = (i1 * PEER_NKEYS + i2).reshape(t, PEER_HEADS * PEER_TOPK)
    gate = jax.nn.softmax(best, axis=-1).reshape(t, PEER_HEADS * PEER_TOPK).astype(xn.dtype)
    n_c = t // PEER_CHUNK

    def chunk(args):
        xc, ec, gc = args
        h = jnp.einsum('td,tkd->tk', xc, u[ec])
        a = gc * jax.nn.gelu(h)
        return jnp.einsum('tk,tkd->td', a, v[ec])

    out = lax.map(chunk, (xt.reshape(n_c, PEER_CHUNK, d),
                          eidx.reshape(n_c, PEER_CHUNK, PEER_HEADS * PEER_TOPK),
                          gate.reshape(n_c, PEER_CHUNK, PEER_HEADS * PEER_TOPK)))
    return out.reshape(b, s, d)


def setup_inputs(seed: int = 0) -> dict:
    key = jax.random.key(seed)
    ks = jax.random.split(key, 20)
    f32 = jnp.float32
    nrm = lambda k, shape, sc: jax.random.normal(k, shape, f32) * sc
    L = DEPTH
    return {
        "x": nrm(ks[0], (BATCH, SEQ, D_MODEL), 1.0),
        "ln1": 1.0 + nrm(ks[1], (L, D_MODEL), 0.02),
        "w_in": nrm(ks[2], (L, D_MODEL, IN_COLS), D_MODEL ** -0.5),
        "conv_w": nrm(ks[3], (L, CONV_K, CONV_CH), CONV_K ** -0.5),
        "cmp_pos_k": nrm(ks[4], (L, CMP_BLOCK, HEAD_DIM), 0.1),
        "cmp_pos_v": nrm(ks[5], (L, CMP_BLOCK, HEAD_DIM), 0.1),
        "cmp_k_w1": nrm(ks[6], (L, CMP_BLOCK * HEAD_DIM, CMP_HIDDEN), (CMP_BLOCK * HEAD_DIM) ** -0.5),
        "cmp_k_w2": nrm(ks[7], (L, CMP_HIDDEN, HEAD_DIM), CMP_HIDDEN ** -0.5),
        "cmp_v_w1": nrm(ks[8], (L, CMP_BLOCK * HEAD_DIM, CMP_HIDDEN), (CMP_BLOCK * HEAD_DIM) ** -0.5),
        "cmp_v_w2": nrm(ks[9], (L, CMP_HIDDEN, HEAD_DIM), CMP_HIDDEN ** -0.5),
        "gn_conv": 1.0 + nrm(ks[10], (L, CONV_CH), 0.02),
        "gn_attn": 1.0 + nrm(ks[11], (L, ATTN_W), 0.02),
        "w_out": nrm(ks[12], (L, MIX_W, D_MODEL), MIX_W ** -0.5),
        "ln2": 1.0 + nrm(ks[13], (L, D_MODEL), 0.02),
        "peer_wq": nrm(ks[14], (L, D_MODEL, PEER_HEADS * PEER_DKEY), D_MODEL ** -0.5),
        "peer_subkeys": nrm(ks[15], (L, PEER_HEADS, 2, PEER_NKEYS, PEER_DKEY // 2), (PEER_DKEY // 2) ** -0.5),
        "peer_u": nrm(ks[16], (L, PEER_EXPERTS, D_MODEL), D_MODEL ** -0.5),
        "peer_v": nrm(ks[17], (L, PEER_EXPERTS, D_MODEL), 0.5),
        "ln_f": 1.0 + nrm(ks[18], (D_MODEL,), 0.02),
    }


def reference(x, ln1, w_in, conv_w, cmp_pos_k, cmp_pos_v, cmp_k_w1, cmp_k_w2, cmp_v_w1, cmp_v_w2,
              gn_conv, gn_attn, w_out, ln2, peer_wq, peer_subkeys, peer_u, peer_v, ln_f):
    b, s, _ = x.shape
    h = x
    for l in range(DEPTH):
        xn = rms_norm(h, ln1[l])
        proj = xn @ w_in[l]
        c_h, c_b, c_c, q, kc, vc, ksl, vsl, kwn, vwn, g = jnp.split(proj, SPLIT_POINTS, axis=-1)
        conv_out = c_b * short_conv(c_c * c_h, conv_w[l])
        r = lambda t: t.reshape(b, s, N_KV, HEAD_DIM)
        kc_c = compress_blocks(r(kc), cmp_pos_k[l], cmp_k_w1[l], cmp_k_w2[l])
        vc_c = compress_blocks(r(vc), cmp_pos_v[l], cmp_v_w1[l], cmp_v_w2[l])
        gates = jax.nn.sigmoid(g.astype(jnp.float32)).reshape(b, s, N_HEADS, 3)
        attn_out = nsa(q.reshape(b, s, N_HEADS, HEAD_DIM), kc_c, vc_c, r(ksl), r(vsl), r(kwn), r(vwn), gates)
        mix = jnp.concatenate([rms_norm(conv_out, gn_conv[l]), rms_norm(attn_out, gn_attn[l])], axis=-1)
        h = h + mix @ w_out[l]
        h = h + peer(rms_norm(h, ln2[l]), peer_wq[l], peer_subkeys[l], peer_u[l], peer_v[l])
    return rms_norm(h, ln_f)
```

```python
import functools
import math

import jax
import jax.numpy as jnp
import numpy as np
from jax import lax
from jax.experimental import pallas as pl
from jax.experimental.pallas import tpu as pltpu

F32 = jnp.float32
BF16 = jnp.bfloat16

D_MODEL = 1024
CONV_CH = 512
CONV_K = 3
N_HEADS = 8
HEAD_DIM = 64
N_KV = 2
HPG = N_HEADS // N_KV
ATTN_W = N_HEADS * HEAD_DIM
CMP_BLOCK = 32
CMP_STRIDE = 16
CMP_HIDDEN = 256
SEL_BLOCK = 64
SEL_TOP = 16
WINDOW = 512
N_GATES = 3 * N_HEADS
PEER_HEADS = 8
PEER_NKEYS = 128
PEER_EXPERTS = PEER_NKEYS * PEER_NKEYS
PEER_DKEY = 256
PEER_TOPK = 16
EPS = 1e-6
NEG_INF = -1e30
FORCE = 1e4

LANES = 128
SUBLANES = 8
VMEM_LIMIT = 56 * 1024 * 1024

TQ = 256
KC = 256
T_PROJ = 256
T_TOPK = 256
T_PEER = 64
HALF = D_MODEL // 2
ROWS_PER_EXPERT = HALF // LANES


def _cparams(sem):
    return pltpu.CompilerParams(dimension_semantics=sem, vmem_limit_bytes=VMEM_LIMIT)


def _dot_nt(a, b, precision=None):
    return lax.dot_general(a, b, (((1,), (1,)), ((), ())), preferred_element_type=F32, precision=precision)


def _dot(a, b):
    return jnp.dot(a, b, preferred_element_type=F32)


def _rms(x, g):
    return x * lax.rsqrt(jnp.mean(x * x, axis=-1, keepdims=True) + EPS) * g


def _gelu(x):
    c = math.sqrt(2.0 / math.pi)
    return 0.5 * x * (1.0 + jnp.tanh(c * (x + 0.044715 * (x * x * x))))


def _inproj_kernel(x_ref, ln1_ref, w_ref, conv_ref, q_ref, kcvc_ref, kv4_ref, gates_ref):
    xn = _rms(x_ref[...], ln1_ref[...]).astype(BF16)
    n_conv, n_q = 3 * CONV_CH, N_HEADS * LANES
    o = 0
    conv_ref[...] = _dot(xn, w_ref[:, o:o + n_conv])
    o += n_conv
    q_ref[...] = (_dot(xn, w_ref[:, o:o + n_q]) * (HEAD_DIM ** -0.5)).astype(BF16)
    o += n_q
    kcvc_ref[...] = _dot(xn, w_ref[:, o:o + 2 * LANES])
    o += 2 * LANES
    kv4_ref[...] = _dot(xn, w_ref[:, o:o + 4 * LANES]).astype(BF16)
    o += 4 * LANES
    gates_ref[...] = jax.nn.sigmoid(_dot(xn, w_ref[:, o:o + LANES]))


def _inproj(x2, ln1, w_cat):
    t = x2.shape[0]
    ncols = w_cat.shape[1]
    row = lambda i: (i, 0)
    fixed = lambda i: (0, 0)
    return pl.pallas_call(
        _inproj_kernel,
        grid=(t // T_PROJ,),
        in_specs=[pl.BlockSpec((T_PROJ, D_MODEL), row),
                  pl.BlockSpec((1, D_MODEL), fixed),
                  pl.BlockSpec((D_MODEL, ncols), fixed)],
        out_specs=[pl.BlockSpec((T_PROJ, 3 * CONV_CH), row),
                   pl.BlockSpec((T_PROJ, N_HEADS * LANES), row),
                   pl.BlockSpec((T_PROJ, 2 * LANES), row),
                   pl.BlockSpec((T_PROJ, 4 * LANES), row),
                   pl.BlockSpec((T_PROJ, LANES), row)],
        out_shape=[jax.ShapeDtypeStruct((t, 3 * CONV_CH), F32),
                   jax.ShapeDtypeStruct((t, N_HEADS * LANES), BF16),
                   jax.ShapeDtypeStruct((t, 2 * LANES), F32),
                   jax.ShapeDtypeStruct((t, 4 * LANES), BF16),
                   jax.ShapeDtypeStruct((t, LANES), F32)],
        compiler_params=_cparams(("parallel",)),
        name="inproj",
    )(x2, ln1, w_cat)


def _compress_kernel(ck_ref, cv_ref, posk_ref, posv_ref, w1k_ref, w2k_ref, w1v_ref, w2v_ref, ok_ref, ov_ref):
    half = CMP_STRIDE * HEAD_DIM
    for c_ref, pos_ref, w1_ref, w2_ref, o_ref in ((ck_ref, posk_ref, w1k_ref, w2k_ref, ok_ref),
                                                  (cv_ref, posv_ref, w1v_ref, w2v_ref, ov_ref)):
        for g in range(N_KV):
            c = c_ref[0, g]
            nc = c.shape[0]
            a = _dot((c + pos_ref[0:1, :]).astype(BF16), w1_ref[0:half, :])
            b = _dot((c + pos_ref[1:2, :]).astype(BF16), w1_ref[half:2 * half, :])
            hid = a + pltpu.roll(b, nc - 1, 0)
            out = _dot(_gelu(hid).astype(BF16), w2_ref[...])
            rows = lax.broadcasted_iota(jnp.int32, out.shape, 0)
            o_ref[0, g] = jnp.where(rows < nc - 1, out, 0.0).astype(BF16)


def _compress(ck, cv, posk, posv, w1k, w2k, w1v, w2v):
    b, g, nc, cw = ck.shape
    blk = lambda i: (i, 0, 0, 0)
    fixed = lambda i: (0, 0)
    return pl.pallas_call(
        _compress_kernel,
        grid=(b,),
        in_specs=[pl.BlockSpec((1, g, nc, cw), blk), pl.BlockSpec((1, g, nc, cw), blk),
                  pl.BlockSpec(posk.shape, fixed), pl.BlockSpec(posv.shape, fixed),
                  pl.BlockSpec(w1k.shape, fixed), pl.BlockSpec(w2k.shape, fixed),
                  pl.BlockSpec(w1v.shape, fixed), pl.BlockSpec(w2v.shape, fixed)],
        out_specs=[pl.BlockSpec((1, g, nc, HEAD_DIM), blk), pl.BlockSpec((1, g, nc, HEAD_DIM), blk)],
        out_shape=[jax.ShapeDtypeStruct((b, g, nc, HEAD_DIM), BF16)] * 2,
        compiler_params=_cparams(("parallel",)),
        name="compress",
    )(ck, cv, posk, posv, w1k, w2k, w1v, w2v)


def _nsa_kernel(q_ref, kse_ref, vs1_ref, kw0_ref, vw1_ref, kc0_ref, vc_ref, gates_ref, ovt_ref, o_ref,
                m_ref, acc_ref):
    qt = pl.program_id(1)
    t0 = qt * TQ
    nc = kc0_ref.shape[2]
    n_sel = ovt_ref.shape[0]
    row = t0 + lax.broadcasted_iota(jnp.int32, (TQ, 1), 0)
    col = lax.broadcasted_iota(jnp.int32, (1, KC), 1)
    causal = (t0 + col) <= row
    band = (t0 - 2 * KC + col) > (row - WINDOW)
    eye = (lax.broadcasted_iota(jnp.int32, (TQ, TQ), 0)
           == lax.broadcasted_iota(jnp.int32, (TQ, TQ), 1)).astype(BF16)
    ncol = lax.broadcasted_iota(jnp.int32, (1, nc), 1)
    valid_c = ((ncol * CMP_STRIDE + (CMP_BLOCK - 1)) <= row) & (ncol < nc - 1)
    jrow = lax.broadcasted_iota(jnp.int32, (n_sel, TQ), 0)
    qblk = jnp.right_shift(t0 + lax.broadcasted_iota(jnp.int32, (n_sel, TQ), 1), int(math.log2(SEL_BLOCK)))
    forced = (jrow == 0) | (jrow == qblk) | (jrow == qblk - 1)
    lane128 = lax.broadcasted_iota(jnp.int32, (TQ, LANES), 1)
    gates = gates_ref[0]

    def online(h, s, v):
        m_old = m_ref[h][:, 0:1]
        m_new = jnp.maximum(m_old, jnp.max(s, axis=-1, keepdims=True))
        alpha = jnp.exp(m_old - m_new)
        p = jnp.exp(s - m_new)
        acc_ref[h] = alpha * acc_ref[h] + _dot(p.astype(BF16), v)
        m_ref[h] = jnp.broadcast_to(m_new, (TQ, LANES))

    def first(h, s, v):
        m = jnp.max(s, axis=-1, keepdims=True)
        acc_ref[h] = _dot(jnp.exp(s - m).astype(BF16), v)
        m_ref[h] = jnp.broadcast_to(m, (TQ, LANES))

    def finish(h):
        a = acc_ref[h]
        return a[:, 0:HEAD_DIM] / a[:, HEAD_DIM:2 * HEAD_DIM]

    for g in range(N_KV):
        kc0 = kc0_ref[0, g]
        vc = vc_ref[0, g]
        qs = [q_ref[0, :, (g * HPG + h) * LANES:(g * HPG + h + 1) * LANES] for h in range(HPG)]
        o_c = []
        psum = jnp.zeros((TQ, nc), F32)
        for h in range(HPG):
            s = jnp.where(valid_c, _dot_nt(qs[h], kc0), NEG_INF)
            m = jnp.max(s, axis=-1, keepdims=True)
            p = jnp.where(valid_c, jnp.exp(s - m), 0.0)
            l = jnp.sum(p, axis=-1, keepdims=True)
            pn = p * (1.0 / jnp.where(l > 0.0, l, 1.0))
            o_c.append(_dot(pn.astype(BF16), vc))
            psum = psum + pn
        imp_t = _dot_nt(ovt_ref[...], psum, precision=lax.Precision.HIGHEST)
        val = jnp.where(jrow > qblk, -FORCE, imp_t + jnp.where(forced, FORCE, 0.0))
        rank = jnp.zeros((n_sel, TQ), jnp.int32)
        for k in range(n_sel):
            vk = val[k:k + 1, :]
            ahead = (vk > val) | ((vk == val) & (jrow > k))
            rank = rank + ahead.astype(jnp.int32)
        sel_t = (rank < SEL_TOP).astype(BF16)
        pad_t = jnp.concatenate([jnp.zeros((LANES - n_sel, TQ), BF16), sel_t], axis=0)
        sel_q = _dot_nt(eye, pad_t)
        bias = jnp.where((lane128 >= HEAD_DIM) & (sel_q < 0.5), NEG_INF, 0.0).astype(BF16)
        lhs = [qs[h] + bias for h in range(HPG)]
        kd = kse_ref[0, g, pl.ds(pl.multiple_of(t0, KC), KC), :]
        vd = vs1_ref[0, g, pl.ds(pl.multiple_of(t0, KC), KC), :]
        for h in range(HPG):
            first(h, jnp.where(causal, _dot_nt(lhs[h], kd), NEG_INF), vd)

        def sel_body(c, carry):
            k0 = pl.multiple_of(c * KC, KC)
            kk = kse_ref[0, g, pl.ds(k0, KC), :]
            vv = vs1_ref[0, g, pl.ds(k0, KC), :]
            for h in range(HPG):
                online(h, _dot_nt(lhs[h], kk), vv)
            return carry

        lax.fori_loop(0, qt, sel_body, 0)
        o_s = [finish(h) for h in range(HPG)]
        kd = kw0_ref[0, g, pl.ds(pl.multiple_of(t0, KC), KC), :]
        vd = vw1_ref[0, g, pl.ds(pl.multiple_of(t0, KC), KC), :]
        for h in range(HPG):
            first(h, jnp.where(causal, _dot_nt(qs[h], kd), NEG_INF), vd)

        @pl.when(qt >= 1)
        def _():
            k0 = pl.multiple_of(t0 - KC, KC)
            kk = kw0_ref[0, g, pl.ds(k0, KC), :]
            vv = vw1_ref[0, g, pl.ds(k0, KC), :]
            for h in range(HPG):
                online(h, _dot_nt(qs[h], kk), vv)

        @pl.when(qt >= 2)
        def _():
            k0 = pl.multiple_of(t0 - 2 * KC, KC)
            kk = kw0_ref[0, g, pl.ds(k0, KC), :]
            vv = vw1_ref[0, g, pl.ds(k0, KC), :]
            for h in range(HPG):
                online(h, jnp.where(band, _dot_nt(qs[h], kk), NEG_INF), vv)

        for h in range(HPG):
            hh = g * HPG + h
            o_w = finish(h)
            o = (gates[:, 3 * hh:3 * hh + 1] * o_c[h] + gates[:, 3 * hh + 1:3 * hh + 2] * o_s[h]
                 + gates[:, 3 * hh + 2:3 * hh + 3] * o_w)
            o_ref[0, :, hh * HEAD_DIM:(hh + 1) * HEAD_DIM] = o


def _nsa(q, kse, vs1, kw0, vw1, kc0, vc, gates, ovt):
    b, s, _ = q.shape
    nc = kc0.shape[2]
    seq = lambda i, j: (i, 0, 0, 0)
    tile = lambda i, j: (i, j, 0)
    fixed = lambda i, j: (0, 0)
    return pl.pallas_call(
        _nsa_kernel,
        grid=(b, s // TQ),
        in_specs=[pl.BlockSpec((1, TQ, N_HEADS * LANES), tile),
                  pl.BlockSpec((1, N_KV, s, LANES), seq), pl.BlockSpec((1, N_KV, s, LANES), seq),
                  pl.BlockSpec((1, N_KV, s, LANES), seq), pl.BlockSpec((1, N_KV, s, LANES), seq),
                  pl.BlockSpec((1, N_KV, nc, LANES), seq), pl.BlockSpec((1, N_KV, nc, HEAD_DIM), seq),
                  pl.BlockSpec((1, TQ, LANES), tile),
                  pl.BlockSpec(ovt.shape, fixed)],
        out_specs=pl.BlockSpec((1, TQ, ATTN_W), tile),
        out_shape=jax.ShapeDtypeStruct((b, s, ATTN_W), F32),
        scratch_shapes=[pltpu.VMEM((HPG, TQ, LANES), F32), pltpu.VMEM((HPG, TQ, LANES), F32)],
        compiler_params=_cparams(("parallel", "arbitrary")),
        name="nsa",
    )(q, kse, vs1, kw0, vw1, kc0, vc, gates, ovt)


def _post_kernel(x_ref, conv_ref, halo_ref, attn_ref, convw_ref, gnc_ref, gna_ref, wout_ref, ln2_ref, wq_ref,
                 h1_ref, hn_ref, pq_ref, *, tiles_per_seq):
    i = pl.program_id(0)
    c_h = conv_ref[:, 0:CONV_CH]
    c_b = conv_ref[:, CONV_CH:2 * CONV_CH]
    c_c = conv_ref[:, 2 * CONV_CH:3 * CONV_CH]
    z = c_c * c_h
    keep = jnp.where(i % tiles_per_seq == 0, 0.0, 1.0)
    zp = halo_ref[:, 2 * CONV_CH:3 * CONV_CH] * halo_ref[:, 0:CONV_CH] * keep
    rows = lax.broadcasted_iota(jnp.int32, z.shape, 0)
    n = z.shape[0]
    z1 = jnp.where(rows == 0, zp[SUBLANES - 1:SUBLANES, :], pltpu.roll(z, 1, 0))
    z2 = jnp.where(rows == 0, zp[SUBLANES - 2:SUBLANES - 1, :],
                   jnp.where(rows == 1, zp[SUBLANES - 1:SUBLANES, :], pltpu.roll(z, 2, 0)))
    conv = convw_ref[0:1, :] * z2 + convw_ref[1:2, :] * z1 + convw_ref[2:3, :] * z
    nc = _rms(c_b * conv, gnc_ref[...]).astype(BF16)
    na = _rms(attn_ref[...], gna_ref[...]).astype(BF16)
    h1 = x_ref[...] + _dot(nc, wout_ref[0:CONV_CH, :]) + _dot(na, wout_ref[CONV_CH:CONV_CH + ATTN_W, :])
    h1_ref[...] = h1
    hn = _rms(h1, ln2_ref[...])
    hn_ref[...] = hn
    pq_ref[...] = _dot(hn.astype(BF16), wq_ref[...]).astype(BF16)


def _post(x2, conv, attn2, convw, gnc, gna, wout, ln2, wq, seq_len):
    t = x2.shape[0]
    row = lambda i: (i, 0)
    fixed = lambda i: (0, 0)
    halo = lambda i: (jnp.maximum(i * (T_PROJ // SUBLANES) - 1, 0), 0)
    nq = wq.shape[1]
    return pl.pallas_call(
        functools.partial(_post_kernel, tiles_per_seq=seq_len // T_PROJ),
        grid=(t // T_PROJ,),
        in_specs=[pl.BlockSpec((T_PROJ, D_MODEL), row),
                  pl.BlockSpec((T_PROJ, 3 * CONV_CH), row),
                  pl.BlockSpec((SUBLANES, 3 * CONV_CH), halo),
                  pl.BlockSpec((T_PROJ, ATTN_W), row),
                  pl.BlockSpec(convw.shape, fixed), pl.BlockSpec(gnc.shape, fixed), pl.BlockSpec(gna.shape, fixed),
                  pl.BlockSpec(wout.shape, fixed), pl.BlockSpec(ln2.shape, fixed), pl.BlockSpec(wq.shape, fixed)],
        out_specs=[pl.BlockSpec((T_PROJ, D_MODEL), row), pl.BlockSpec((T_PROJ, D_MODEL), row),
                   pl.BlockSpec((T_PROJ, nq), row)],
        out_shape=[jax.ShapeDtypeStruct((t, D_MODEL), F32), jax.ShapeDtypeStruct((t, D_MODEL), F32),
                   jax.ShapeDtypeStruct((t, nq), BF16)],
        compiler_params=_cparams(("parallel",)),
        name="post",
    )(x2, conv, conv, attn2, convw, gnc, gna, wout, ln2, wq)


def _staircase():
    return [(a, b) for a in range(PEER_TOPK) for b in range(PEER_TOPK) if (a + 1) * (b + 1) <= PEER_TOPK]


N_CAND = 56


def _topk_kernel(pq_ref, sk_ref, e_ref, g_ref, sv_ref, si_ref, cand_ref, ce_ref, et_ref, gt_ref):
    tt = pq_ref.shape[0]
    rown = lax.broadcasted_iota(jnp.int32, (PEER_NKEYS, tt), 0)
    rowc = lax.broadcasted_iota(jnp.int32, (N_CAND, tt), 0)
    pairs = _staircase()
    for h in range(PEER_HEADS):
        for c in range(2):
            off = (h * 2 + c) * PEER_NKEYS
            x = _dot_nt(sk_ref[h, c], pq_ref[:, off:off + PEER_NKEYS])
            for it in range(PEER_TOPK):
                m = jnp.max(x, axis=0, keepdims=True)
                idx = jnp.min(jnp.where(x == m, rown, PEER_NKEYS), axis=0, keepdims=True)
                sv_ref[c, it:it + 1, :] = m
                si_ref[c, it:it + 1, :] = idx
                x = jnp.where(rown == idx, -jnp.inf, x)
        cand_ref[...] = jnp.full((N_CAND, tt), -jnp.inf, F32)
        ce_ref[...] = jnp.zeros((N_CAND, tt), jnp.int32)
        for r, (a, b) in enumerate(pairs):
            cand_ref[r:r + 1, :] = sv_ref[0, a:a + 1, :] + sv_ref[1, b:b + 1, :]
            ce_ref[r:r + 1, :] = si_ref[0, a:a + 1, :] * PEER_NKEYS + si_ref[1, b:b + 1, :]
        x = cand_ref[...]
        ce = ce_ref[...]
        best = []
        for it in range(PEER_TOPK):
            m = jnp.max(x, axis=0, keepdims=True)
            idx = jnp.min(jnp.where(x == m, rowc, N_CAND), axis=0, keepdims=True)
            hit = rowc == idx
            et_ref[h * PEER_TOPK + it:h * PEER_TOPK + it + 1, :] = jnp.max(jnp.where(hit, ce, -1), axis=0, keepdims=True)
            best.append(m)
            x = jnp.where(hit, -jnp.inf, x)
        ex = [jnp.exp(v - best[0]) for v in best]
        tot = ex[0]
        for v in ex[1:]:
            tot = tot + v
        inv = 1.0 / tot
        for it in range(PEER_TOPK):
            gt_ref[h * PEER_TOPK + it:h * PEER_TOPK + it + 1, :] = ex[it] * inv
    e_ref[...] = (et_ref[...] * ROWS_PER_EXPERT).T
    g_ref[...] = gt_ref[...].T


def _topk(pq, sk):
    t, nq = pq.shape
    nk = PEER_HEADS * PEER_TOPK
    row = lambda i: (i, 0)
    return pl.pallas_call(
        _topk_kernel,
        grid=(t // T_TOPK,),
        in_specs=[pl.BlockSpec((T_TOPK, nq), row), pl.BlockSpec(sk.shape, lambda i: (0, 0, 0, 0))],
        out_specs=[pl.BlockSpec((T_TOPK, nk), row), pl.BlockSpec((T_TOPK, nk), row)],
        out_shape=[jax.ShapeDtypeStruct((t, nk), jnp.int32), jax.ShapeDtypeStruct((t, nk), F32)],
        scratch_shapes=[pltpu.VMEM((2, PEER_TOPK, T_TOPK), F32), pltpu.VMEM((2, PEER_TOPK, T_TOPK), jnp.int32),
                        pltpu.VMEM((N_CAND, T_TOPK), F32), pltpu.VMEM((N_CAND, T_TOPK), jnp.int32),
                        pltpu.VMEM((nk, T_TOPK), jnp.int32), pltpu.VMEM((nk, T_TOPK), F32)],
        compiler_params=_cparams(("parallel",)),
        name="topk",
    )(pq, sk)


def _unpack(w):
    lo = pltpu.bitcast(w << 16, F32)
    hi = pltpu.bitcast(w & jnp.uint32(0xFFFF0000), F32)
    return lo, hi


def _peer_u_kernel(e_ref, tab_ref, xlo_ref, xhi_ref, gate_ref, a_ref, p_ref, h_ref):
    nk = e_ref.shape[1]
    ones = jnp.ones((SUBLANES, LANES), F32)

    def token(t, carry):
        r0 = pl.multiple_of(t * ROWS_PER_EXPERT, ROWS_PER_EXPERT)
        xl = xlo_ref[pl.ds(r0, ROWS_PER_EXPERT), :]
        xh = xhi_ref[pl.ds(r0, ROWS_PER_EXPERT), :]
        for k in range(nk):
            e0 = pl.multiple_of(e_ref[t, k], ROWS_PER_EXPERT)
            lo, hi = _unpack(tab_ref[pl.ds(e0, ROWS_PER_EXPERT), :])
            prod = lo * xl + hi * xh
            for s in range(ROWS_PER_EXPERT):
                p_ref[s * nk + k:s * nk + k + 1, :] = prod[s:s + 1, :]
        sums = _dot_nt(ones, p_ref[...])
        hrow = sums[0:1, 0:nk]
        for s in range(1, ROWS_PER_EXPERT):
            hrow = hrow + sums[0:1, s * nk:(s + 1) * nk]
        h_ref[pl.ds(t, 1), :] = hrow
        return carry

    lax.fori_loop(0, e_ref.shape[0], token, 0)
    a_ref[...] = gate_ref[...] * _gelu(h_ref[...])


def _peer_u(eidx, tab, xlo, xhi, gate):
    t, nk = eidx.shape
    row = lambda i: (i, 0)
    return pl.pallas_call(
        _peer_u_kernel,
        grid=(t // T_PEER,),
        in_specs=[pl.BlockSpec((T_PEER, nk), row, memory_space=pltpu.SMEM),
                  pl.BlockSpec(tab.shape, lambda i: (0, 0), pipeline_mode=pl.Buffered(1)),
                  pl.BlockSpec((T_PEER * ROWS_PER_EXPERT, LANES), row),
                  pl.BlockSpec((T_PEER * ROWS_PER_EXPERT, LANES), row),
                  pl.BlockSpec((T_PEER, nk), row)],
        out_specs=pl.BlockSpec((T_PEER, nk), row),
        out_shape=jax.ShapeDtypeStruct((t, nk), F32),
        scratch_shapes=[pltpu.VMEM((ROWS_PER_EXPERT * nk, LANES), F32), pltpu.VMEM((T_PEER, nk), F32)],
        compiler_params=_cparams(("arbitrary",)),
        name="peer_u",
    )(eidx, tab, xlo, xhi, gate)


def _peer_v_kernel(e_ref, a_ref, tab_ref, olo_ref, ohi_ref):
    nk = e_ref.shape[1]

    def token(t, carry):
        accs = [jnp.zeros((ROWS_PER_EXPERT, LANES), F32) for _ in range(4)]
        for k in range(nk):
            e0 = pl.multiple_of(e_ref[t, k], ROWS_PER_EXPERT)
            lo, hi = _unpack(tab_ref[pl.ds(e0, ROWS_PER_EXPERT), :])
            a = a_ref[t, k]
            j = 2 * (k % 2)
            accs[j] = accs[j] + a * lo
            accs[j + 1] = accs[j + 1] + a * hi
        r0 = pl.multiple_of(t * ROWS_PER_EXPERT, ROWS_PER_EXPERT)
        olo_ref[pl.ds(r0, ROWS_PER_EXPERT), :] = accs[0] + accs[2]
        ohi_ref[pl.ds(r0, ROWS_PER_EXPERT), :] = accs[1] + accs[3]
        return carry

    lax.fori_loop(0, e_ref.shape[0], token, 0)


def _peer_v(eidx, a, tab):
    t, nk = eidx.shape
    row = lambda i: (i, 0)
    return pl.pallas_call(
        _peer_v_kernel,
        grid=(t // T_PEER,),
        in_specs=[pl.BlockSpec((T_PEER, nk), row, memory_space=pltpu.SMEM),
                  pl.BlockSpec((T_PEER, nk), row, memory_space=pltpu.SMEM),
                  pl.BlockSpec(tab.shape, lambda i: (0, 0), pipeline_mode=pl.Buffered(1))],
        out_specs=[pl.BlockSpec((T_PEER * ROWS_PER_EXPERT, LANES), row),
                   pl.BlockSpec((T_PEER * ROWS_PER_EXPERT, LANES), row)],
        out_shape=[jax.ShapeDtypeStruct((t * ROWS_PER_EXPERT, LANES), F32)] * 2,
        compiler_params=_cparams(("arbitrary",)),
        name="peer_v",
    )(eidx, a, tab)


def _final_kernel(h1_ref, p_ref, g_ref, o_ref):
    o_ref[...] = _rms(h1_ref[...] + p_ref[...], g_ref[...])


def _final(h1, peer, g):
    t = h1.shape[0]
    row = lambda i: (i, 0)
    return pl.pallas_call(
        _final_kernel,
        grid=(t // T_PROJ,),
        in_specs=[pl.BlockSpec((T_PROJ, D_MODEL), row), pl.BlockSpec((T_PROJ, D_MODEL), row),
                  pl.BlockSpec((1, D_MODEL), lambda i: (0, 0))],
        out_specs=pl.BlockSpec((T_PROJ, D_MODEL), row),
        out_shape=jax.ShapeDtypeStruct((t, D_MODEL), F32),
        compiler_params=_cparams(("parallel",)),
        name="final",
    )(h1, peer, g)


def _pack_table(w):
    bits = lax.bitcast_convert_type(w.astype(BF16), jnp.uint16).astype(jnp.uint32)
    words = bits[:, :HALF] | (bits[:, HALF:] << 16)
    return words.reshape(w.shape[0] * ROWS_PER_EXPERT, LANES)


def _pad_lanes(a, left):
    z = jnp.zeros_like(a)
    return jnp.concatenate([a, z] if left else [z, a], axis=-1)


def _layer(h, l, ln1, w_in, conv_w, cmp_pos_k, cmp_pos_v, cmp_k_w1, cmp_k_w2, cmp_v_w1, cmp_v_w2,
           gn_conv, gn_attn, w_out, ln2, peer_wq, peer_subkeys, peer_u, peer_v):
    b, s, _ = h.shape
    t = b * s
    x2 = h.reshape(t, D_MODEL)
    w = w_in[l]
    o_q = 3 * CONV_CH
    o_kv = o_q + ATTN_W
    o_g = o_kv + 6 * N_KV * HEAD_DIM
    wq_heads = w[:, o_q:o_kv].reshape(D_MODEL, N_HEADS, HEAD_DIM)
    wq_pad = _pad_lanes(wq_heads, True).reshape(D_MODEL, N_HEADS * LANES)
    wg_pad = jnp.pad(w[:, o_g:], ((0, 0), (0, LANES - N_GATES)))
    w_cat = jnp.concatenate([w[:, :o_q], wq_pad, w[:, o_kv:o_g], wg_pad], axis=1).astype(BF16)
    conv, q, kcvc, kv4, gates = _inproj(x2, ln1[l][None, :], w_cat)

    nc = s // CMP_STRIDE

    def chunks(a):
        a = a.reshape(b, nc, CMP_STRIDE, N_KV, HEAD_DIM).transpose(0, 3, 1, 2, 4)
        return a.reshape(b, N_KV, nc, CMP_STRIDE * HEAD_DIM)

    pos2 = lambda p: p.reshape(2, CMP_STRIDE * HEAD_DIM)
    kcc, vcc = _compress(chunks(kcvc[:, :LANES]), chunks(kcvc[:, LANES:]), pos2(cmp_pos_k[l]), pos2(cmp_pos_v[l]),
                         cmp_k_w1[l].astype(BF16), cmp_k_w2[l].astype(BF16),
                         cmp_v_w1[l].astype(BF16), cmp_v_w2[l].astype(BF16))

    def group(a):
        return a.reshape(b, s, N_KV, HEAD_DIM).transpose(0, 2, 1, 3)

    ks, vs, kw, vw = (group(kv4[:, i * LANES:(i + 1) * LANES]) for i in range(4))
    n_sel = s // SEL_BLOCK
    blk_of_key = jnp.arange(s) // SEL_BLOCK
    onehot = (blk_of_key[:, None] == jnp.arange(n_sel)[None, :]).astype(BF16)
    onehot = jnp.pad(onehot, ((0, 0), (0, HEAD_DIM - n_sel))) if n_sel < HEAD_DIM else onehot
    ones = jnp.ones((b, N_KV, s, HEAD_DIM), BF16)
    kse = jnp.concatenate([ks, jnp.broadcast_to(onehot, (b, N_KV, s, HEAD_DIM))], axis=-1)
    vs1 = jnp.concatenate([vs, ones], axis=-1)
    kw0 = _pad_lanes(kw, True)
    vw1 = jnp.concatenate([vw, ones], axis=-1)
    kc0 = _pad_lanes(kcc, True)
    n_cmp = (s - CMP_BLOCK) // CMP_STRIDE + 1
    cs = np.arange(nc) * CMP_STRIDE
    ss = np.arange(HEAD_DIM) * SEL_BLOCK
    ov = ((cs[:, None] < ss[None, :] + SEL_BLOCK) & (cs[:, None] + CMP_BLOCK > ss[None, :])
          & (np.arange(nc)[:, None] < n_cmp) & (np.arange(HEAD_DIM)[None, :] < n_sel))
    ovt = jnp.asarray(ov.T.astype(np.float32))
    attn = _nsa(q.reshape(b, s, N_HEADS * LANES), kse, vs1, kw0, vw1, kc0, vcc,
                gates.reshape(b, s, LANES), ovt)

    h1, hn, pq = _post(x2, conv, attn.reshape(t, ATTN_W), conv_w[l], gn_conv[l][None, :], gn_attn[l][None, :],
                       w_out[l].astype(BF16), ln2[l][None, :], peer_wq[l].astype(BF16), s)

    eidx, gate = _topk(pq, peer_subkeys[l].astype(BF16))
    xlo = hn[:, :HALF].reshape(t * ROWS_PER_EXPERT, LANES)
    xhi = hn[:, HALF:].reshape(t * ROWS_PER_EXPERT, LANES)
    a = _peer_u(eidx, _pack_table(peer_u[l]), xlo, xhi, gate)
    olo, ohi = _peer_v(eidx, a, _pack_table(peer_v[l]))
    peer = jnp.concatenate([olo.reshape(t, HALF), ohi.reshape(t, HALF)], axis=1)
    return h1, peer


def kernel(x, ln1, w_in, conv_w, cmp_pos_k, cmp_pos_v, cmp_k_w1, cmp_k_w2, cmp_v_w1, cmp_v_w2, gn_conv, gn_attn,
           w_out, ln2, peer_wq, peer_subkeys, peer_u, peer_v, ln_f):
    b, s, _ = x.shape
    depth = w_in.shape[0]
    h = x
    for l in range(depth):
        h1, peer = _layer(h, l, ln1, w_in, conv_w, cmp_pos_k, cmp_pos_v, cmp_k_w1, cmp_k_w2, cmp_v_w1, cmp_v_w2,
                          gn_conv, gn_attn, w_out, ln2, peer_wq, peer_subkeys, peer_u, peer_v)
        if l + 1 < depth:
            h = (h1 + peer).reshape(b, s, D_MODEL)
    return _final(h1, peer, ln_f[None, :]).reshape(b, s, D_MODEL)
```

```python
import functools
import math

import jax
import jax.numpy as jnp
import numpy as np
from jax import lax
from jax.experimental import pallas as pl
from jax.experimental.pallas import tpu as pltpu

F32 = jnp.float32
BF16 = jnp.bfloat16

D_MODEL = 1024
CONV_CH = 512
CONV_K = 3
N_HEADS = 8
HEAD_DIM = 64
N_KV = 2
HPG = N_HEADS // N_KV
ATTN_W = N_HEADS * HEAD_DIM
CMP_BLOCK = 32
CMP_STRIDE = 16
CMP_HIDDEN = 256
SEL_BLOCK = 64
SEL_TOP = 16
WINDOW = 512
N_GATES = 3 * N_HEADS
PEER_HEADS = 8
PEER_NKEYS = 128
PEER_EXPERTS = PEER_NKEYS * PEER_NKEYS
PEER_DKEY = 256
PEER_TOPK = 16
EPS = 1e-6
NEG_INF = -1e30
FORCE = 1e4

LANES = 128
SUBLANES = 8
VMEM_LIMIT = 56 * 1024 * 1024

TQ = 256
KC = 256
T_PROJ = 256
T_TOPK = 256
T_PEER = 64
HALF = D_MODEL // 2
ROWS_PER_EXPERT = HALF // LANES


def _cparams(sem):
    return pltpu.CompilerParams(dimension_semantics=sem, vmem_limit_bytes=VMEM_LIMIT)


def _dot_nt(a, b, precision=None):
    return lax.dot_general(a, b, (((1,), (1,)), ((), ())), preferred_element_type=F32, precision=precision)


def _dot(a, b):
    return jnp.dot(a, b, preferred_element_type=F32)


def _rms(x, g):
    return x * lax.rsqrt(jnp.mean(x * x, axis=-1, keepdims=True) + EPS) * g


def _gelu(x):
    c = math.sqrt(2.0 / math.pi)
    return 0.5 * x * (1.0 + jnp.tanh(c * (x + 0.044715 * (x * x * x))))


def _inproj_kernel(x_ref, ln1_ref, w_ref, conv_ref, q_ref, kcvc_ref, kv4_ref, gates_ref):
    xn = _rms(x_ref[...], ln1_ref[...]).astype(BF16)
    n_conv, n_q = 3 * CONV_CH, N_HEADS * LANES
    o = 0
    conv_ref[...] = _dot(xn, w_ref[:, o:o + n_conv])
    o += n_conv
    q_ref[...] = (_dot(xn, w_ref[:, o:o + n_q]) * (HEAD_DIM ** -0.5)).astype(BF16)
    o += n_q
    kcvc_ref[...] = _dot(xn, w_ref[:, o:o + 2 * LANES])
    o += 2 * LANES
    kv4_ref[...] = _dot(xn, w_ref[:, o:o + 4 * LANES]).astype(BF16)
    o += 4 * LANES
    gates_ref[...] = jax.nn.sigmoid(_dot(xn, w_ref[:, o:o + LANES]))


def _inproj(x2, ln1, w_cat):
    t = x2.shape[0]
    ncols = w_cat.shape[1]
    row = lambda i: (i, 0)
    fixed = lambda i: (0, 0)
    return pl.pallas_call(
        _inproj_kernel,
        grid=(t // T_PROJ,),
        in_specs=[pl.BlockSpec((T_PROJ, D_MODEL), row),
                  pl.BlockSpec((1, D_MODEL), fixed),
                  pl.BlockSpec((D_MODEL, ncols), fixed)],
        out_specs=[pl.BlockSpec((T_PROJ, 3 * CONV_CH), row),
                   pl.BlockSpec((T_PROJ, N_HEADS * LANES), row),
                   pl.BlockSpec((T_PROJ, 2 * LANES), row),
                   pl.BlockSpec((T_PROJ, 4 * LANES), row),
                   pl.BlockSpec((T_PROJ, LANES), row)],
        out_shape=[jax.ShapeDtypeStruct((t, 3 * CONV_CH), F32),
                   jax.ShapeDtypeStruct((t, N_HEADS * LANES), BF16),
                   jax.ShapeDtypeStruct((t, 2 * LANES), F32),
                   jax.ShapeDtypeStruct((t, 4 * LANES), BF16),
                   jax.ShapeDtypeStruct((t, LANES), F32)],
        compiler_params=_cparams(("parallel",)),
        name="inproj",
    )(x2, ln1, w_cat)


def _compress_kernel(ck_ref, cv_ref, posk_ref, posv_ref, w1k_ref, w2k_ref, w1v_ref, w2v_ref, ok_ref, ov_ref):
    half = CMP_STRIDE * HEAD_DIM
    for c_ref, pos_ref, w1_ref, w2_ref, o_ref in ((ck_ref, posk_ref, w1k_ref, w2k_ref, ok_ref),
                                                  (cv_ref, posv_ref, w1v_ref, w2v_ref, ov_ref)):
        for g in range(N_KV):
            c = c_ref[0, g]
            nc = c.shape[0]
            a = _dot((c + pos_ref[0:1, :]).astype(BF16), w1_ref[0:half, :])
            b = _dot((c + pos_ref[1:2, :]).astype(BF16), w1_ref[half:2 * half, :])
            hid = a + pltpu.roll(b, nc - 1, 0)
            out = _dot(_gelu(hid).astype(BF16), w2_ref[...])
            rows = lax.broadcasted_iota(jnp.int32, out.shape, 0)
            o_ref[0, g] = jnp.where(rows < nc - 1, out, 0.0).astype(BF16)


def _compress(ck, cv, posk, posv, w1k, w2k, w1v, w2v):
    b, g, nc, cw = ck.shape
    blk = lambda i: (i, 0, 0, 0)
    fixed = lambda i: (0, 0)
    return pl.pallas_call(
        _compress_kernel,
        grid=(b,),
        in_specs=[pl.BlockSpec((1, g, nc, cw), blk), pl.BlockSpec((1, g, nc, cw), blk),
                  pl.BlockSpec(posk.shape, fixed), pl.BlockSpec(posv.shape, fixed),
                  pl.BlockSpec(w1k.shape, fixed), pl.BlockSpec(w2k.shape, fixed),
                  pl.BlockSpec(w1v.shape, fixed), pl.BlockSpec(w2v.shape, fixed)],
        out_specs=[pl.BlockSpec((1, g, nc, HEAD_DIM), blk), pl.BlockSpec((1, g, nc, HEAD_DIM), blk)],
        out_shape=[jax.ShapeDtypeStruct((b, g, nc, HEAD_DIM), BF16)] * 2,
        compiler_params=_cparams(("parallel",)),
        name="compress",
    )(ck, cv, posk, posv, w1k, w2k, w1v, w2v)


def _nsa_kernel(q_ref, kse_ref, vs1_ref, kw0_ref, vw1_ref, kc0_ref, vc_ref, gates_ref, ovt_ref, o_ref,
                m_ref, acc_ref):
    qt = pl.program_id(1)
    t0 = qt * TQ
    nc = kc0_ref.shape[2]
    n_sel = ovt_ref.shape[0]
    row = t0 + lax.broadcasted_iota(jnp.int32, (TQ, 1), 0)
    col = lax.broadcasted_iota(jnp.int32, (1, KC), 1)
    causal = (t0 + col) <= row
    band = (t0 - 2 * KC + col) > (row - WINDOW)
    eye = (lax.broadcasted_iota(jnp.int32, (TQ, TQ), 0)
           == lax.broadcasted_iota(jnp.int32, (TQ, TQ), 1)).astype(BF16)
    ncol = lax.broadcasted_iota(jnp.int32, (1, nc), 1)
    valid_c = ((ncol * CMP_STRIDE + (CMP_BLOCK - 1)) <= row) & (ncol < nc - 1)
    jrow = lax.broadcasted_iota(jnp.int32, (n_sel, TQ), 0)
    qblk = jnp.right_shift(t0 + lax.broadcasted_iota(jnp.int32, (n_sel, TQ), 1), int(math.log2(SEL_BLOCK)))
    forced = (jrow == 0) | (jrow == qblk) | (jrow == qblk - 1)
    lane128 = lax.broadcasted_iota(jnp.int32, (TQ, LANES), 1)
    gates = gates_ref[0]

    def online(h, s, v):
        m_old = m_ref[h][:, 0:1]
        m_new = jnp.maximum(m_old, jnp.max(s, axis=-1, keepdims=True))
        alpha = jnp.exp(m_old - m_new)
        p = jnp.exp(s - m_new)
        acc_ref[h] = alpha * acc_ref[h] + _dot(p.astype(BF16), v)
        m_ref[h] = jnp.broadcast_to(m_new, (TQ, LANES))

    def first(h, s, v):
        m = jnp.max(s, axis=-1, keepdims=True)
        acc_ref[h] = _dot(jnp.exp(s - m).astype(BF16), v)
        m_ref[h] = jnp.broadcast_to(m, (TQ, LANES))

    def finish(h):
        a = acc_ref[h]
        return a[:, 0:HEAD_DIM] / a[:, HEAD_DIM:2 * HEAD_DIM]

    for g in range(N_KV):
        kc0 = kc0_ref[0, g]
        vc = vc_ref[0, g]
        qs = [q_ref[0, :, (g * HPG + h) * LANES:(g * HPG + h + 1) * LANES] for h in range(HPG)]
        o_c = []
        psum = jnp.zeros((TQ, nc), F32)
        for h in range(HPG):
            s = jnp.where(valid_c, _dot_nt(qs[h], kc0), NEG_INF)
            m = jnp.max(s, axis=-1, keepdims=True)
            p = jnp.where(valid_c, jnp.exp(s - m), 0.0)
            l = jnp.sum(p, axis=-1, keepdims=True)
            pn = p * (1.0 / jnp.where(l > 0.0, l, 1.0))
            o_c.append(_dot(pn.astype(BF16), vc))
            psum = psum + pn
        imp_t = _dot_nt(ovt_ref[...], psum, precision=lax.Precision.HIGHEST)
        val = jnp.where(jrow > qblk, -FORCE, imp_t + jnp.where(forced, FORCE, 0.0))
        rank = jnp.zeros((n_sel, TQ), jnp.int32)
        for k in range(n_sel):
            vk = val[k:k + 1, :]
            ahead = (vk > val) | ((vk == val) & (jrow > k))
            rank = rank + ahead.astype(jnp.int32)
        sel_t = (rank < SEL_TOP).astype(BF16)
        pad_t = jnp.concatenate([jnp.zeros((LANES - n_sel, TQ), BF16), sel_t], axis=0)
        sel_q = _dot_nt(eye, pad_t)
        bias = jnp.where((lane128 >= HEAD_DIM) & (sel_q < 0.5), NEG_INF, 0.0).astype(BF16)
        lhs = [qs[h] + bias for h in range(HPG)]
        kd = kse_ref[0, g, pl.ds(pl.multiple_of(t0, KC), KC), :]
        vd = vs1_ref[0, g, pl.ds(pl.multiple_of(t0, KC), KC), :]
        for h in range(HPG):
            first(h, jnp.where(causal, _dot_nt(lhs[h], kd), NEG_INF), vd)

        def sel_body(c, carry):
            k0 = pl.multiple_of(c * KC, KC)
            kk = kse_ref[0, g, pl.ds(k0, KC), :]
            vv = vs1_ref[0, g, pl.ds(k0, KC), :]
            for h in range(HPG):
                online(h, _dot_nt(lhs[h], kk), vv)
            return carry

        lax.fori_loop(0, qt, sel_body, 0)
        o_s = [finish(h) for h in range(HPG)]
        kd = kw0_ref[0, g, pl.ds(pl.multiple_of(t0, KC), KC), :]
        vd = vw1_ref[0, g, pl.ds(pl.multiple_of(t0, KC), KC), :]
        for h in range(HPG):
            first(h, jnp.where(causal, _dot_nt(qs[h], kd), NEG_INF), vd)

        @pl.when(qt >= 1)
        def _():
            k0 = pl.multiple_of(t0 - KC, KC)
            kk = kw0_ref[0, g, pl.ds(k0, KC), :]
            vv = vw1_ref[0, g, pl.ds(k0, KC), :]
            for h in range(HPG):
                online(h, _dot_nt(qs[h], kk), vv)

        @pl.when(qt >= 2)
        def _():
            k0 = pl.multiple_of(t0 - 2 * KC, KC)
            kk = kw0_ref[0, g, pl.ds(k0, KC), :]
            vv = vw1_ref[0, g, pl.ds(k0, KC), :]
            for h in range(HPG):
                online(h, jnp.where(band, _dot_nt(qs[h], kk), NEG_INF), vv)

        for h in range(HPG):
            hh = g * HPG + h
            o_w = finish(h)
            o = (gates[:, 3 * hh:3 * hh + 1] * o_c[h] + gates[:, 3 * hh + 1:3 * hh + 2] * o_s[h]
                 + gates[:, 3 * hh + 2:3 * hh + 3] * o_w)
            o_ref[0, :, hh * HEAD_DIM:(hh + 1) * HEAD_DIM] = o


def _nsa(q, kse, vs1, kw0, vw1, kc0, vc, gates, ovt):
    b, s, _ = q.shape
    nc = kc0.shape[2]
    seq = lambda i, j: (i, 0, 0, 0)
    tile = lambda i, j: (i, j, 0)
    fixed = lambda i, j: (0, 0)
    return pl.pallas_call(
        _nsa_kernel,
        grid=(b, s // TQ),
        in_specs=[pl.BlockSpec((1, TQ, N_HEADS * LANES), tile),
                  pl.BlockSpec((1, N_KV, s, LANES), seq), pl.BlockSpec((1, N_KV, s, LANES), seq),
                  pl.BlockSpec((1, N_KV, s, LANES), seq), pl.BlockSpec((1, N_KV, s, LANES), seq),
                  pl.BlockSpec((1, N_KV, nc, LANES), seq), pl.BlockSpec((1, N_KV, nc, HEAD_DIM), seq),
                  pl.BlockSpec((1, TQ, LANES), tile),
                  pl.BlockSpec(ovt.shape, fixed)],
        out_specs=pl.BlockSpec((1, TQ, ATTN_W), tile),
        out_shape=jax.ShapeDtypeStruct((b, s, ATTN_W), F32),
        scratch_shapes=[pltpu.VMEM((HPG, TQ, LANES), F32), pltpu.VMEM((HPG, TQ, LANES), F32)],
        compiler_params=_cparams(("parallel", "arbitrary")),
        name="nsa",
    )(q, kse, vs1, kw0, vw1, kc0, vc, gates, ovt)


def _post_kernel(x_ref, conv_ref, halo_ref, attn_ref, convw_ref, gnc_ref, gna_ref, wout_ref, ln2_ref, wq_ref,
                 h1_ref, xh_ref, xl_ref, pq_ref, *, tiles_per_seq):
    i = pl.program_id(0)
    c_h = conv_ref[:, 0:CONV_CH]
    c_b = conv_ref[:, CONV_CH:2 * CONV_CH]
    c_c = conv_ref[:, 2 * CONV_CH:3 * CONV_CH]
    z = c_c * c_h
    keep = jnp.where(i % tiles_per_seq == 0, 0.0, 1.0)
    zp = halo_ref[:, 2 * CONV_CH:3 * CONV_CH] * halo_ref[:, 0:CONV_CH] * keep
    rows = lax.broadcasted_iota(jnp.int32, z.shape, 0)
    n = z.shape[0]
    z1 = jnp.where(rows == 0, zp[SUBLANES - 1:SUBLANES, :], pltpu.roll(z, 1, 0))
    z2 = jnp.where(rows == 0, zp[SUBLANES - 2:SUBLANES - 1, :],
                   jnp.where(rows == 1, zp[SUBLANES - 1:SUBLANES, :], pltpu.roll(z, 2, 0)))
    conv = convw_ref[0:1, :] * z2 + convw_ref[1:2, :] * z1 + convw_ref[2:3, :] * z
    nc = _rms(c_b * conv, gnc_ref[...]).astype(BF16)
    na = _rms(attn_ref[...], gna_ref[...]).astype(BF16)
    h1 = x_ref[...] + _dot(nc, wout_ref[0:CONV_CH, :]) + _dot(na, wout_ref[CONV_CH:CONV_CH + ATTN_W, :])
    h1_ref[...] = h1
    hn = _rms(h1, ln2_ref[...])
    xh = hn.astype(BF16)
    xh_ref[...] = xh
    xl_ref[...] = (hn - xh.astype(F32)).astype(BF16)
    pq_ref[...] = _dot(xh, wq_ref[...]).astype(BF16)


def _post(x2, conv, attn2, convw, gnc, gna, wout, ln2, wq, seq_len):
    t = x2.shape[0]
    row = lambda i: (i, 0)
    fixed = lambda i: (0, 0)
    halo = lambda i: (jnp.maximum(i * (T_PROJ // SUBLANES) - 1, 0), 0)
    nq = wq.shape[1]
    return pl.pallas_call(
        functools.partial(_post_kernel, tiles_per_seq=seq_len // T_PROJ),
        grid=(t // T_PROJ,),
        in_specs=[pl.BlockSpec((T_PROJ, D_MODEL), row),
                  pl.BlockSpec((T_PROJ, 3 * CONV_CH), row),
                  pl.BlockSpec((SUBLANES, 3 * CONV_CH), halo),
                  pl.BlockSpec((T_PROJ, ATTN_W), row),
                  pl.BlockSpec(convw.shape, fixed), pl.BlockSpec(gnc.shape, fixed), pl.BlockSpec(gna.shape, fixed),
                  pl.BlockSpec(wout.shape, fixed), pl.BlockSpec(ln2.shape, fixed), pl.BlockSpec(wq.shape, fixed)],
        out_specs=[pl.BlockSpec((T_PROJ, D_MODEL), row), pl.BlockSpec((T_PROJ, D_MODEL), row),
                   pl.BlockSpec((T_PROJ, D_MODEL), row), pl.BlockSpec((T_PROJ, nq), row)],
        out_shape=[jax.ShapeDtypeStruct((t, D_MODEL), F32), jax.ShapeDtypeStruct((t, D_MODEL), BF16),
                   jax.ShapeDtypeStruct((t, D_MODEL), BF16), jax.ShapeDtypeStruct((t, nq), BF16)],
        compiler_params=_cparams(("parallel",)),
        name="post",
    )(x2, conv, conv, attn2, convw, gnc, gna, wout, ln2, wq)


def _staircase():
    return [(a, b) for a in range(PEER_TOPK) for b in range(PEER_TOPK) if (a + 1) * (b + 1) <= PEER_TOPK]


N_CAND = 56


def _topk_kernel(pq_ref, sk_ref, e_ref, g_ref, sv_ref, si_ref, cand_ref, ce_ref, et_ref, gt_ref):
    tt = pq_ref.shape[0]
    rown = lax.broadcasted_iota(jnp.int32, (PEER_NKEYS, tt), 0)
    rowc = lax.broadcasted_iota(jnp.int32, (N_CAND, tt), 0)
    pairs = _staircase()
    for h in range(PEER_HEADS):
        for c in range(2):
            off = (h * 2 + c) * PEER_NKEYS
            x = _dot_nt(sk_ref[h, c], pq_ref[:, off:off + PEER_NKEYS])
            for it in range(PEER_TOPK):
                m = jnp.max(x, axis=0, keepdims=True)
                idx = jnp.min(jnp.where(x == m, rown, PEER_NKEYS), axis=0, keepdims=True)
                sv_ref[c, it:it + 1, :] = m
                si_ref[c, it:it + 1, :] = idx
                x = jnp.where(rown == idx, -jnp.inf, x)
        cand_ref[...] = jnp.full((N_CAND, tt), -jnp.inf, F32)
        ce_ref[...] = jnp.zeros((N_CAND, tt), jnp.int32)
        for r, (a, b) in enumerate(pairs):
            cand_ref[r:r + 1, :] = sv_ref[0, a:a + 1, :] + sv_ref[1, b:b + 1, :]
            ce_ref[r:r + 1, :] = si_ref[0, a:a + 1, :] * PEER_NKEYS + si_ref[1, b:b + 1, :]
        x = cand_ref[...]
        ce = ce_ref[...]
        best = []
        for it in range(PEER_TOPK):
            m = jnp.max(x, axis=0, keepdims=True)
            idx = jnp.min(jnp.where(x == m, rowc, N_CAND), axis=0, keepdims=True)
            hit = rowc == idx
            et_ref[h * PEER_TOPK + it:h * PEER_TOPK + it + 1, :] = jnp.max(jnp.where(hit, ce, -1), axis=0, keepdims=True)
            best.append(m)
            x = jnp.where(hit, -jnp.inf, x)
        ex = [jnp.exp(v - best[0]) for v in best]
        tot = ex[0]
        for v in ex[1:]:
            tot = tot + v
        inv = 1.0 / tot
        for it in range(PEER_TOPK):
            gt_ref[h * PEER_TOPK + it:h * PEER_TOPK + it + 1, :] = ex[it] * inv
    e_ref[...] = (et_ref[...] * ROWS_PER_EXPERT).T
    g_ref[...] = gt_ref[...].T


def _topk(pq, sk):
    t, nq = pq.shape
    nk = PEER_HEADS * PEER_TOPK
    row = lambda i: (i, 0)
    return pl.pallas_call(
        _topk_kernel,
        grid=(t // T_TOPK,),
        in_specs=[pl.BlockSpec((T_TOPK, nq), row), pl.BlockSpec(sk.shape, lambda i: (0, 0, 0, 0))],
        out_specs=[pl.BlockSpec((T_TOPK, nk), row), pl.BlockSpec((T_TOPK, nk), row)],
        out_shape=[jax.ShapeDtypeStruct((t, nk), jnp.int32), jax.ShapeDtypeStruct((t, nk), F32)],
        scratch_shapes=[pltpu.VMEM((2, PEER_TOPK, T_TOPK), F32), pltpu.VMEM((2, PEER_TOPK, T_TOPK), jnp.int32),
                        pltpu.VMEM((N_CAND, T_TOPK), F32), pltpu.VMEM((N_CAND, T_TOPK), jnp.int32),
                        pltpu.VMEM((nk, T_TOPK), jnp.int32), pltpu.VMEM((nk, T_TOPK), F32)],
        compiler_params=_cparams(("parallel",)),
        name="topk",
    )(pq, sk)


def _gather_rows(e_ref, tab_ref, w_ref, t):
    for k in range(e_ref.shape[1]):
        e0 = pl.multiple_of(e_ref[t, k], ROWS_PER_EXPERT)
        w_ref[k * ROWS_PER_EXPERT:(k + 1) * ROWS_PER_EXPERT, :] = tab_ref[pl.ds(e0, ROWS_PER_EXPERT), :]


def _for_each_token(e_ref, tab_ref, w_refs, compute):
    n = e_ref.shape[0]
    _gather_rows(e_ref, tab_ref, w_refs[0], 0)
    for t in range(n):
        if t + 1 < n:
            _gather_rows(e_ref, tab_ref, w_refs[(t + 1) % 2], t + 1)
        compute(t, pltpu.bitcast(w_refs[t % 2][...], BF16))


def _diag16(n):
    lane = lax.broadcasted_iota(jnp.int32, (2 * SUBLANES, n), 1)
    row = lax.broadcasted_iota(jnp.int32, (2 * SUBLANES, n), 0)
    return (lane & (SUBLANES - 1)) == (row & (SUBLANES - 1)), row < SUBLANES


def _peer_u_kernel(e_ref, tab_ref, x_ref, gate_ref, g8_ref, a_ref, w0_ref, w1_ref, hs_ref):
    diag, _ = _diag16(SUBLANES * e_ref.shape[1])

    def token(t, w):
        x16 = pltpu.bitcast(x_ref[t * SUBLANES:(t + 1) * SUBLANES, :], BF16)
        r = _dot_nt(x16, w)
        hs_ref[t:t + 1, :] = jnp.sum(jnp.where(diag, r, 0.0), axis=0, keepdims=True)

    _for_each_token(e_ref, tab_ref, (w0_ref, w1_ref), token)
    h = jnp.dot(hs_ref[...], g8_ref[...], preferred_element_type=F32, precision=lax.Precision.HIGHEST)
    a_ref[...] = gate_ref[...] * _gelu(h)


def _peer_u(eidx, tab, xw, gate, g8):
    t, nk = eidx.shape
    row = lambda i: (i, 0)
    fixed = lambda i: (0, 0)
    return pl.pallas_call(
        _peer_u_kernel,
        grid=(t // T_PEER,),
        in_specs=[pl.BlockSpec((T_PEER, nk), row, memory_space=pltpu.SMEM),
                  pl.BlockSpec(tab.shape, fixed, pipeline_mode=pl.Buffered(1)),
                  pl.BlockSpec((T_PEER * SUBLANES, LANES), row),
                  pl.BlockSpec((T_PEER, nk), row),
                  pl.BlockSpec(g8.shape, fixed)],
        out_specs=pl.BlockSpec((T_PEER, nk), row),
        out_shape=jax.ShapeDtypeStruct((t, nk), F32),
        scratch_shapes=[pltpu.VMEM((ROWS_PER_EXPERT * nk, LANES), jnp.uint32),
                        pltpu.VMEM((ROWS_PER_EXPERT * nk, LANES), jnp.uint32),
                        pltpu.VMEM((T_PEER, SUBLANES * nk), F32)],
        compiler_params=_cparams(("arbitrary",)),
        name="peer_u",
    )(eidx, tab, xw, gate, g8)


def _peer_v_kernel(e_ref, tab_ref, a_ref, rep_ref, o_ref, w0_ref, w1_ref, arep_ref):
    n = SUBLANES * e_ref.shape[1]
    diag, top = _diag16(n)
    arep_ref[...] = jnp.dot(a_ref[...], rep_ref[...], preferred_element_type=F32, precision=lax.Precision.HIGHEST)

    def token(t, w):
        l32 = jnp.where(diag, jnp.broadcast_to(arep_ref[t:t + 1, :], (2 * SUBLANES, n)), 0.0)
        hi = l32.astype(BF16).astype(F32)
        lhs = jnp.where(top, hi, l32 - hi).astype(BF16)
        out = _dot(lhs, w)
        o_ref[t * SUBLANES:(t + 1) * SUBLANES, :] = out[0:SUBLANES] + out[SUBLANES:2 * SUBLANES]

    _for_each_token(e_ref, tab_ref, (w0_ref, w1_ref), token)


def _peer_v(eidx, tab, a, rep):
    t, nk = eidx.shape
    row = lambda i: (i, 0)
    fixed = lambda i: (0, 0)
    return pl.pallas_call(
        _peer_v_kernel,
        grid=(t // T_PEER,),
        in_specs=[pl.BlockSpec((T_PEER, nk), row, memory_space=pltpu.SMEM),
                  pl.BlockSpec(tab.shape, fixed, pipeline_mode=pl.Buffered(1)),
                  pl.BlockSpec((T_PEER, nk), row),
                  pl.BlockSpec(rep.shape, fixed)],
        out_specs=pl.BlockSpec((T_PEER * SUBLANES, LANES), row),
        out_shape=jax.ShapeDtypeStruct((t * SUBLANES, LANES), F32),
        scratch_shapes=[pltpu.VMEM((ROWS_PER_EXPERT * nk, LANES), jnp.uint32),
                        pltpu.VMEM((ROWS_PER_EXPERT * nk, LANES), jnp.uint32),
                        pltpu.VMEM((T_PEER, SUBLANES * nk), F32)],
        compiler_params=_cparams(("arbitrary",)),
        name="peer_v",
    )(eidx, tab, a, rep)


def _final_kernel(h1_ref, p_ref, g_ref, o_ref):
    o_ref[...] = _rms(h1_ref[...] + p_ref[...], g_ref[...])


def _final(h1, peer, g):
    t = h1.shape[0]
    row = lambda i: (i, 0)
    return pl.pallas_call(
        _final_kernel,
        grid=(t // T_PROJ,),
        in_specs=[pl.BlockSpec((T_PROJ, D_MODEL), row), pl.BlockSpec((T_PROJ, D_MODEL), row),
                  pl.BlockSpec((1, D_MODEL), lambda i: (0, 0))],
        out_specs=pl.BlockSpec((T_PROJ, D_MODEL), row),
        out_shape=jax.ShapeDtypeStruct((t, D_MODEL), F32),
        compiler_params=_cparams(("parallel",)),
        name="final",
    )(h1, peer, g)


def _pack_rows(a):
    r, n, _ = a.shape
    bits = lax.bitcast_convert_type(a, jnp.uint16).astype(jnp.uint32).reshape(r, n // 2, 2, LANES)
    return (bits[:, :, 0] | (bits[:, :, 1] << 16)).reshape(r * n // 2, LANES)


def _pad_lanes(a, left):
    z = jnp.zeros_like(a)
    return jnp.concatenate([a, z] if left else [z, a], axis=-1)


def _layer(h, l, ln1, w_in, conv_w, cmp_pos_k, cmp_pos_v, cmp_k_w1, cmp_k_w2, cmp_v_w1, cmp_v_w2,
           gn_conv, gn_attn, w_out, ln2, peer_wq, peer_subkeys, peer_u, peer_v):
    b, s, _ = h.shape
    t = b * s
    x2 = h.reshape(t, D_MODEL)
    w = w_in[l]
    o_q = 3 * CONV_CH
    o_kv = o_q + ATTN_W
    o_g = o_kv + 6 * N_KV * HEAD_DIM
    wq_heads = w[:, o_q:o_kv].reshape(D_MODEL, N_HEADS, HEAD_DIM)
    wq_pad = _pad_lanes(wq_heads, True).reshape(D_MODEL, N_HEADS * LANES)
    wg_pad = jnp.pad(w[:, o_g:], ((0, 0), (0, LANES - N_GATES)))
    w_cat = jnp.concatenate([w[:, :o_q], wq_pad, w[:, o_kv:o_g], wg_pad], axis=1).astype(BF16)
    conv, q, kcvc, kv4, gates = _inproj(x2, ln1[l][None, :], w_cat)

    nc = s // CMP_STRIDE

    def chunks(a):
        a = a.reshape(b, nc, CMP_STRIDE, N_KV, HEAD_DIM).transpose(0, 3, 1, 2, 4)
        return a.reshape(b, N_KV, nc, CMP_STRIDE * HEAD_DIM)

    pos2 = lambda p: p.reshape(2, CMP_STRIDE * HEAD_DIM)
    kcc, vcc = _compress(chunks(kcvc[:, :LANES]), chunks(kcvc[:, LANES:]), pos2(cmp_pos_k[l]), pos2(cmp_pos_v[l]),
                         cmp_k_w1[l].astype(BF16), cmp_k_w2[l].astype(BF16),
                         cmp_v_w1[l].astype(BF16), cmp_v_w2[l].astype(BF16))

    def group(a):
        return a.reshape(b, s, N_KV, HEAD_DIM).transpose(0, 2, 1, 3)

    ks, vs, kw, vw = (group(kv4[:, i * LANES:(i + 1) * LANES]) for i in range(4))
    n_sel = s // SEL_BLOCK
    blk_of_key = jnp.arange(s) // SEL_BLOCK
    onehot = (blk_of_key[:, None] == jnp.arange(n_sel)[None, :]).astype(BF16)
    onehot = jnp.pad(onehot, ((0, 0), (0, HEAD_DIM - n_sel))) if n_sel < HEAD_DIM else onehot
    ones = jnp.ones((b, N_KV, s, HEAD_DIM), BF16)
    kse = jnp.concatenate([ks, jnp.broadcast_to(onehot, (b, N_KV, s, HEAD_DIM))], axis=-1)
    vs1 = jnp.concatenate([vs, ones], axis=-1)
    kw0 = _pad_lanes(kw, True)
    vw1 = jnp.concatenate([vw, ones], axis=-1)
    kc0 = _pad_lanes(kcc, True)
    n_cmp = (s - CMP_BLOCK) // CMP_STRIDE + 1
    cs = np.arange(nc) * CMP_STRIDE
    ss = np.arange(HEAD_DIM) * SEL_BLOCK
    ov = ((cs[:, None] < ss[None, :] + SEL_BLOCK) & (cs[:, None] + CMP_BLOCK > ss[None, :])
          & (np.arange(nc)[:, None] < n_cmp) & (np.arange(HEAD_DIM)[None, :] < n_sel))
    ovt = jnp.asarray(ov.T.astype(np.float32))
    attn = _nsa(q.reshape(b, s, N_HEADS * LANES), kse, vs1, kw0, vw1, kc0, vcc,
                gates.reshape(b, s, LANES), ovt)

    h1, xh, xl, pq = _post(x2, conv, attn.reshape(t, ATTN_W), conv_w[l], gn_conv[l][None, :], gn_attn[l][None, :],
                       w_out[l].astype(BF16), ln2[l][None, :], peer_wq[l].astype(BF16), s)

    eidx, gate = _topk(pq, peer_subkeys[l].astype(BF16))
    nk = PEER_HEADS * PEER_TOPK
    g8 = (jnp.arange(SUBLANES * nk)[:, None] // SUBLANES == jnp.arange(nk)[None, :]).astype(F32)
    table = lambda w: _pack_rows(w.astype(BF16).reshape(PEER_EXPERTS, SUBLANES, LANES))
    xw = _pack_rows(jnp.concatenate([xh.reshape(t, SUBLANES, LANES), xl.reshape(t, SUBLANES, LANES)], axis=1))
    a = _peer_u(eidx, table(peer_u[l]), xw, gate, g8)
    peer = _peer_v(eidx, table(peer_v[l]), a, g8.T).reshape(t, D_MODEL)
    return h1, peer


def kernel(x, ln1, w_in, conv_w, cmp_pos_k, cmp_pos_v, cmp_k_w1, cmp_k_w2, cmp_v_w1, cmp_v_w2, gn_conv, gn_attn,
           w_out, ln2, peer_wq, peer_subkeys, peer_u, peer_v, ln_f):
    b, s, _ = x.shape
    depth = w_in.shape[0]
    h = x
    for l in range(depth):
        h1, peer = _layer(h, l, ln1, w_in, conv_w, cmp_pos_k, cmp_pos_v, cmp_k_w1, cmp_k_w2, cmp_v_w1, cmp_v_w2,
                          gn_conv, gn_attn, w_out, ln2, peer_wq, peer_subkeys, peer_u, peer_v)
        if l + 1 < depth:
            h = (h1 + peer).reshape(b, s, D_MODEL)
    return _final(h1, peer, ln_f[None, :]).reshape(b, s, D_MODEL)
```

```python
import functools
import math

import jax
import jax.numpy as jnp
import numpy as np
from jax import lax
from jax.experimental import pallas as pl
from jax.experimental.pallas import tpu as pltpu

F32 = jnp.float32
BF16 = jnp.bfloat16

D_MODEL = 1024
CONV_CH = 512
CONV_K = 3
N_HEADS = 8
HEAD_DIM = 64
N_KV = 2
HPG = N_HEADS // N_KV
ATTN_W = N_HEADS * HEAD_DIM
CMP_BLOCK = 32
CMP_STRIDE = 16
CMP_HIDDEN = 256
SEL_BLOCK = 64
SEL_TOP = 16
WINDOW = 512
N_GATES = 3 * N_HEADS
PEER_HEADS = 8
PEER_NKEYS = 128
PEER_EXPERTS = PEER_NKEYS * PEER_NKEYS
PEER_DKEY = 256
PEER_TOPK = 16
EPS = 1e-6
NEG_INF = -1e30
FORCE = 1e4

LANES = 128
SUBLANES = 8
VMEM_LIMIT = 56 * 1024 * 1024

TQ = 256
KC = 256
T_PROJ = 256
T_TOPK = 256
T_PEER = 64
HALF = D_MODEL // 2
ROWS_PER_EXPERT = HALF // LANES


def _cparams(sem):
    return pltpu.CompilerParams(dimension_semantics=sem, vmem_limit_bytes=VMEM_LIMIT)


def _dot_nt(a, b, precision=None):
    return lax.dot_general(a, b, (((1,), (1,)), ((), ())), preferred_element_type=F32, precision=precision)


def _dot(a, b):
    return jnp.dot(a, b, preferred_element_type=F32)


def _rms(x, g):
    return x * lax.rsqrt(jnp.mean(x * x, axis=-1, keepdims=True) + EPS) * g


def _gelu(x):
    c = math.sqrt(2.0 / math.pi)
    return 0.5 * x * (1.0 + jnp.tanh(c * (x + 0.044715 * (x * x * x))))


def _inproj_kernel(x_ref, ln1_ref, w_ref, conv_ref, q_ref, kcvc_ref, kv4_ref, gates_ref):
    xn = _rms(x_ref[...], ln1_ref[...]).astype(BF16)
    n_conv, n_q = 3 * CONV_CH, N_HEADS * LANES
    o = 0
    conv_ref[...] = _dot(xn, w_ref[:, o:o + n_conv])
    o += n_conv
    q_ref[...] = (_dot(xn, w_ref[:, o:o + n_q]) * (HEAD_DIM ** -0.5)).astype(BF16)
    o += n_q
    kcvc_ref[...] = _dot(xn, w_ref[:, o:o + 2 * LANES])
    o += 2 * LANES
    kv4_ref[...] = _dot(xn, w_ref[:, o:o + 4 * LANES]).astype(BF16)
    o += 4 * LANES
    gates_ref[...] = jax.nn.sigmoid(_dot(xn, w_ref[:, o:o + LANES]))


def _inproj(x2, ln1, w_cat):
    t = x2.shape[0]
    ncols = w_cat.shape[1]
    row = lambda i: (i, 0)
    fixed = lambda i: (0, 0)
    return pl.pallas_call(
        _inproj_kernel,
        grid=(t // T_PROJ,),
        in_specs=[pl.BlockSpec((T_PROJ, D_MODEL), row),
                  pl.BlockSpec((1, D_MODEL), fixed),
                  pl.BlockSpec((D_MODEL, ncols), fixed)],
        out_specs=[pl.BlockSpec((T_PROJ, 3 * CONV_CH), row),
                   pl.BlockSpec((T_PROJ, N_HEADS * LANES), row),
                   pl.BlockSpec((T_PROJ, 2 * LANES), row),
                   pl.BlockSpec((T_PROJ, 4 * LANES), row),
                   pl.BlockSpec((T_PROJ, LANES), row)],
        out_shape=[jax.ShapeDtypeStruct((t, 3 * CONV_CH), F32),
                   jax.ShapeDtypeStruct((t, N_HEADS * LANES), BF16),
                   jax.ShapeDtypeStruct((t, 2 * LANES), F32),
                   jax.ShapeDtypeStruct((t, 4 * LANES), BF16),
                   jax.ShapeDtypeStruct((t, LANES), F32)],
        compiler_params=_cparams(("parallel",)),
        name="inproj",
    )(x2, ln1, w_cat)


def _compress_kernel(ck_ref, cv_ref, posk_ref, posv_ref, w1k_ref, w2k_ref, w1v_ref, w2v_ref, ok_ref, ov_ref):
    half = CMP_STRIDE * HEAD_DIM
    for c_ref, pos_ref, w1_ref, w2_ref, o_ref in ((ck_ref, posk_ref, w1k_ref, w2k_ref, ok_ref),
                                                  (cv_ref, posv_ref, w1v_ref, w2v_ref, ov_ref)):
        for g in range(N_KV):
            c = c_ref[0, g]
            nc = c.shape[0]
            a = _dot((c + pos_ref[0:1, :]).astype(BF16), w1_ref[0:half, :])
            b = _dot((c + pos_ref[1:2, :]).astype(BF16), w1_ref[half:2 * half, :])
            hid = a + pltpu.roll(b, nc - 1, 0)
            out = _dot(_gelu(hid).astype(BF16), w2_ref[...])
            rows = lax.broadcasted_iota(jnp.int32, out.shape, 0)
            o_ref[0, g] = jnp.where(rows < nc - 1, out, 0.0).astype(BF16)


def _compress(ck, cv, posk, posv, w1k, w2k, w1v, w2v):
    b, g, nc, cw = ck.shape
    blk = lambda i: (i, 0, 0, 0)
    fixed = lambda i: (0, 0)
    return pl.pallas_call(
        _compress_kernel,
        grid=(b,),
        in_specs=[pl.BlockSpec((1, g, nc, cw), blk), pl.BlockSpec((1, g, nc, cw), blk),
                  pl.BlockSpec(posk.shape, fixed), pl.BlockSpec(posv.shape, fixed),
                  pl.BlockSpec(w1k.shape, fixed), pl.BlockSpec(w2k.shape, fixed),
                  pl.BlockSpec(w1v.shape, fixed), pl.BlockSpec(w2v.shape, fixed)],
        out_specs=[pl.BlockSpec((1, g, nc, HEAD_DIM), blk), pl.BlockSpec((1, g, nc, HEAD_DIM), blk)],
        out_shape=[jax.ShapeDtypeStruct((b, g, nc, HEAD_DIM), BF16)] * 2,
        compiler_params=_cparams(("parallel",)),
        name="compress",
    )(ck, cv, posk, posv, w1k, w2k, w1v, w2v)


def _nsa_kernel(q_ref, kse_ref, vs1t_ref, kw0_ref, vw1t_ref, kc0_ref, vct_ref, gt_ref, ovt_ref, o_ref,
                m_ref, acc_ref):
    qt = pl.program_id(1)
    t0 = qt * TQ
    nc = kc0_ref.shape[2]
    n_sel = ovt_ref.shape[0]
    w = HPG * TQ
    tq = t0 + (lax.broadcasted_iota(jnp.int32, (1, w), 1) & (TQ - 1))
    krow = lax.broadcasted_iota(jnp.int32, (KC, 1), 0)
    causal = (t0 + krow) <= tq
    band = (t0 - 2 * KC + krow) > (tq - WINDOW)
    eye = (lax.broadcasted_iota(jnp.int32, (TQ, TQ), 0)
           == lax.broadcasted_iota(jnp.int32, (TQ, TQ), 1)).astype(BF16)
    nrow = lax.broadcasted_iota(jnp.int32, (nc, 1), 0)
    valid_c = ((nrow * CMP_STRIDE + (CMP_BLOCK - 1)) <= tq) & (nrow < nc - 1)
    jrow = lax.broadcasted_iota(jnp.int32, (n_sel, TQ), 0)
    qblk = jnp.right_shift(t0 + lax.broadcasted_iota(jnp.int32, (n_sel, TQ), 1), int(math.log2(SEL_BLOCK)))
    forced = (jrow == 0) | (jrow == qblk) | (jrow == qblk - 1)
    lane128 = lax.broadcasted_iota(jnp.int32, (TQ, LANES), 1)
    gt = gt_ref[0]

    def online(s, vt):
        m_old = m_ref[0:1, :]
        m_new = jnp.maximum(m_old, jnp.max(s, axis=0, keepdims=True))
        alpha = jnp.exp(m_old - m_new)
        p = jnp.exp(s - m_new)
        acc_ref[...] = alpha * acc_ref[...] + _dot(vt, p.astype(BF16))
        m_ref[...] = jnp.broadcast_to(m_new, (SUBLANES, w))

    def first(s, vt):
        m = jnp.max(s, axis=0, keepdims=True)
        acc_ref[...] = _dot(vt, jnp.exp(s - m).astype(BF16))
        m_ref[...] = jnp.broadcast_to(m, (SUBLANES, w))

    def finish():
        a = acc_ref[...]
        return a[0:HEAD_DIM, :] / a[HEAD_DIM:HEAD_DIM + 1, :]

    for g in range(N_KV):
        kc0 = kc0_ref[0, g]
        q_all = jnp.concatenate([q_ref[0, :, (g * HPG + h) * LANES:(g * HPG + h + 1) * LANES]
                                 for h in range(HPG)], axis=0)
        s = jnp.where(valid_c, _dot_nt(kc0, q_all), NEG_INF)
        m = jnp.max(s, axis=0, keepdims=True)
        p = jnp.where(valid_c, jnp.exp(s - m), 0.0)
        l = jnp.sum(p, axis=0, keepdims=True)
        pn = p * (1.0 / jnp.where(l > 0.0, l, 1.0))
        o_c = _dot(vct_ref[0, g], pn.astype(BF16))
        psum = pn[:, 0:TQ]
        for h in range(1, HPG):
            psum = psum + pn[:, h * TQ:(h + 1) * TQ]
        imp_t = jnp.dot(ovt_ref[...], psum, preferred_element_type=F32,
                        precision=lax.Precision.HIGHEST)
        val = jnp.where(jrow > qblk, -FORCE, imp_t + jnp.where(forced, FORCE, 0.0))
        rank = jnp.zeros((n_sel, TQ), jnp.int32)
        for k in range(n_sel):
            vk = val[k:k + 1, :]
            ahead = (vk > val) | ((vk == val) & (jrow > k))
            rank = rank + ahead.astype(jnp.int32)
        sel_t = (rank < SEL_TOP).astype(BF16)
        pad_t = jnp.concatenate([jnp.zeros((LANES - n_sel, TQ), BF16), sel_t], axis=0)
        sel_q = _dot_nt(eye, pad_t)
        bias = jnp.where((lane128 >= HEAD_DIM) & (sel_q < 0.5), NEG_INF, 0.0).astype(BF16)
        lhs = q_all + jnp.concatenate([bias] * HPG, axis=0)
        kd = kse_ref[0, g, pl.ds(pl.multiple_of(t0, KC), KC), :]
        first(jnp.where(causal, _dot_nt(kd, lhs), NEG_INF), vs1t_ref[0, g, qt])

        def sel_body(c, carry):
            kk = kse_ref[0, g, pl.ds(pl.multiple_of(c * KC, KC), KC), :]
            online(_dot_nt(kk, lhs), vs1t_ref[0, g, c])
            return carry

        lax.fori_loop(0, qt, sel_body, 0)
        o_s = finish()
        kd = kw0_ref[0, g, pl.ds(pl.multiple_of(t0, KC), KC), :]
        first(jnp.where(causal, _dot_nt(kd, q_all), NEG_INF), vw1t_ref[0, g, qt])

        @pl.when(qt >= 1)
        def _():
            kk = kw0_ref[0, g, pl.ds(pl.multiple_of(t0 - KC, KC), KC), :]
            online(_dot_nt(kk, q_all), vw1t_ref[0, g, qt - 1])

        @pl.when(qt >= 2)
        def _():
            kk = kw0_ref[0, g, pl.ds(pl.multiple_of(t0 - 2 * KC, KC), KC), :]
            online(jnp.where(band, _dot_nt(kk, q_all), NEG_INF), vw1t_ref[0, g, qt - 2])

        o_w = finish()
        outs = []
        for h in range(HPG):
            r = 3 * (g * HPG + h)
            c0, c1 = h * TQ, (h + 1) * TQ
            outs.append(gt[r:r + 1, :] * o_c[:, c0:c1] + gt[r + 1:r + 2, :] * o_s[:, c0:c1]
                        + gt[r + 2:r + 3, :] * o_w[:, c0:c1])
        o_ref[0, :, g * HPG * HEAD_DIM:(g + 1) * HPG * HEAD_DIM] = jnp.concatenate(outs, axis=0).T


def _nsa(q, kse, vs1t, kw0, vw1t, kc0, vct, gates_t, ovt):
    b, s, _ = q.shape
    nc = kc0.shape[2]
    seq = lambda i, j: (i, 0, 0, 0)
    seq5 = lambda i, j: (i, 0, 0, 0, 0)
    tile = lambda i, j: (i, j, 0)
    fixed = lambda i, j: (0, 0)
    return pl.pallas_call(
        _nsa_kernel,
        grid=(b, s // TQ),
        in_specs=[pl.BlockSpec((1, TQ, N_HEADS * LANES), tile),
                  pl.BlockSpec((1, N_KV, s, LANES), seq), pl.BlockSpec((1, N_KV, s // KC, LANES, KC), seq5),
                  pl.BlockSpec((1, N_KV, s, LANES), seq), pl.BlockSpec((1, N_KV, s // KC, LANES, KC), seq5),
                  pl.BlockSpec((1, N_KV, nc, LANES), seq), pl.BlockSpec((1, N_KV, HEAD_DIM, nc), seq),
                  pl.BlockSpec((1, 4 * SUBLANES, TQ), lambda i, j: (i, 0, j)),
                  pl.BlockSpec(ovt.shape, fixed)],
        out_specs=pl.BlockSpec((1, TQ, ATTN_W), tile),
        out_shape=jax.ShapeDtypeStruct((b, s, ATTN_W), F32),
        scratch_shapes=[pltpu.VMEM((SUBLANES, HPG * TQ), F32), pltpu.VMEM((LANES, HPG * TQ), F32)],
        compiler_params=_cparams(("parallel", "arbitrary")),
        name="nsa",
    )(q, kse, vs1t, kw0, vw1t, kc0, vct, gates_t, ovt)


def _post_kernel(x_ref, conv_ref, halo_ref, attn_ref, convw_ref, gnc_ref, gna_ref, wout_ref, ln2_ref, wq_ref,
                 h1_ref, xh_ref, xl_ref, pq_ref, *, tiles_per_seq):
    i = pl.program_id(0)
    c_h = conv_ref[:, 0:CONV_CH]
    c_b = conv_ref[:, CONV_CH:2 * CONV_CH]
    c_c = conv_ref[:, 2 * CONV_CH:3 * CONV_CH]
    z = c_c * c_h
    keep = jnp.where(i % tiles_per_seq == 0, 0.0, 1.0)
    zp = halo_ref[:, 2 * CONV_CH:3 * CONV_CH] * halo_ref[:, 0:CONV_CH] * keep
    rows = lax.broadcasted_iota(jnp.int32, z.shape, 0)
    n = z.shape[0]
    z1 = jnp.where(rows == 0, zp[SUBLANES - 1:SUBLANES, :], pltpu.roll(z, 1, 0))
    z2 = jnp.where(rows == 0, zp[SUBLANES - 2:SUBLANES - 1, :],
                   jnp.where(rows == 1, zp[SUBLANES - 1:SUBLANES, :], pltpu.roll(z, 2, 0)))
    conv = convw_ref[0:1, :] * z2 + convw_ref[1:2, :] * z1 + convw_ref[2:3, :] * z
    nc = _rms(c_b * conv, gnc_ref[...]).astype(BF16)
    na = _rms(attn_ref[...], gna_ref[...]).astype(BF16)
    h1 = x_ref[...] + _dot(nc, wout_ref[0:CONV_CH, :]) + _dot(na, wout_ref[CONV_CH:CONV_CH + ATTN_W, :])
    h1_ref[...] = h1
    hn = _rms(h1, ln2_ref[...])
    xh = hn.astype(BF16)
    xh_ref[...] = xh
    xl_ref[...] = (hn - xh.astype(F32)).astype(BF16)
    pq_ref[...] = _dot(xh, wq_ref[...]).astype(BF16)


def _post(x2, conv, attn2, convw, gnc, gna, wout, ln2, wq, seq_len):
    t = x2.shape[0]
    row = lambda i: (i, 0)
    fixed = lambda i: (0, 0)
    halo = lambda i: (jnp.maximum(i * (T_PROJ // SUBLANES) - 1, 0), 0)
    nq = wq.shape[1]
    return pl.pallas_call(
        functools.partial(_post_kernel, tiles_per_seq=seq_len // T_PROJ),
        grid=(t // T_PROJ,),
        in_specs=[pl.BlockSpec((T_PROJ, D_MODEL), row),
                  pl.BlockSpec((T_PROJ, 3 * CONV_CH), row),
                  pl.BlockSpec((SUBLANES, 3 * CONV_CH), halo),
                  pl.BlockSpec((T_PROJ, ATTN_W), row),
                  pl.BlockSpec(convw.shape, fixed), pl.BlockSpec(gnc.shape, fixed), pl.BlockSpec(gna.shape, fixed),
                  pl.BlockSpec(wout.shape, fixed), pl.BlockSpec(ln2.shape, fixed), pl.BlockSpec(wq.shape, fixed)],
        out_specs=[pl.BlockSpec((T_PROJ, D_MODEL), row), pl.BlockSpec((T_PROJ, D_MODEL), row),
                   pl.BlockSpec((T_PROJ, D_MODEL), row), pl.BlockSpec((T_PROJ, nq), row)],
        out_shape=[jax.ShapeDtypeStruct((t, D_MODEL), F32), jax.ShapeDtypeStruct((t, D_MODEL), BF16),
                   jax.ShapeDtypeStruct((t, D_MODEL), BF16), jax.ShapeDtypeStruct((t, nq), BF16)],
        compiler_params=_cparams(("parallel",)),
        name="post",
    )(x2, conv, conv, attn2, convw, gnc, gna, wout, ln2, wq)


def _staircase():
    return [(a, b) for a in range(PEER_TOPK) for b in range(PEER_TOPK) if (a + 1) * (b + 1) <= PEER_TOPK]


N_CAND = 56


def _topk_kernel(pq_ref, sk_ref, e_ref, g_ref, sv_ref, si_ref, cand_ref, ce_ref, et_ref, gt_ref):
    tt = pq_ref.shape[0]
    rown = lax.broadcasted_iota(jnp.int32, (PEER_NKEYS, tt), 0)
    rowc = lax.broadcasted_iota(jnp.int32, (N_CAND, tt), 0)
    pairs = _staircase()
    for h in range(PEER_HEADS):
        for c in range(2):
            off = (h * 2 + c) * PEER_NKEYS
            x = _dot_nt(sk_ref[h, c], pq_ref[:, off:off + PEER_NKEYS])
            for it in range(PEER_TOPK):
                m = jnp.max(x, axis=0, keepdims=True)
                idx = jnp.min(jnp.where(x == m, rown, PEER_NKEYS), axis=0, keepdims=True)
                sv_ref[c, it:it + 1, :] = m
                si_ref[c, it:it + 1, :] = idx
                x = jnp.where(rown == idx, -jnp.inf, x)
        cand_ref[...] = jnp.full((N_CAND, tt), -jnp.inf, F32)
        ce_ref[...] = jnp.zeros((N_CAND, tt), jnp.int32)
        for r, (a, b) in enumerate(pairs):
            cand_ref[r:r + 1, :] = sv_ref[0, a:a + 1, :] + sv_ref[1, b:b + 1, :]
            ce_ref[r:r + 1, :] = si_ref[0, a:a + 1, :] * PEER_NKEYS + si_ref[1, b:b + 1, :]
        x = cand_ref[...]
        ce = ce_ref[...]
        best = []
        for it in range(PEER_TOPK):
            m = jnp.max(x, axis=0, keepdims=True)
            idx = jnp.min(jnp.where(x == m, rowc, N_CAND), axis=0, keepdims=True)
            hit = rowc == idx
            et_ref[h * PEER_TOPK + it:h * PEER_TOPK + it + 1, :] = jnp.max(jnp.where(hit, ce, -1), axis=0, keepdims=True)
            best.append(m)
            x = jnp.where(hit, -jnp.inf, x)
        ex = [jnp.exp(v - best[0]) for v in best]
        tot = ex[0]
        for v in ex[1:]:
            tot = tot + v
        inv = 1.0 / tot
        for it in range(PEER_TOPK):
            gt_ref[h * PEER_TOPK + it:h * PEER_TOPK + it + 1, :] = ex[it] * inv
    e_ref[...] = (et_ref[...] * ROWS_PER_EXPERT).T
    g_ref[...] = gt_ref[...].T


def _topk(pq, sk):
    t, nq = pq.shape
    nk = PEER_HEADS * PEER_TOPK
    row = lambda i: (i, 0)
    return pl.pallas_call(
        _topk_kernel,
        grid=(t // T_TOPK,),
        in_specs=[pl.BlockSpec((T_TOPK, nq), row), pl.BlockSpec(sk.shape, lambda i: (0, 0, 0, 0))],
        out_specs=[pl.BlockSpec((T_TOPK, nk), row), pl.BlockSpec((T_TOPK, nk), row)],
        out_shape=[jax.ShapeDtypeStruct((t, nk), jnp.int32), jax.ShapeDtypeStruct((t, nk), F32)],
        scratch_shapes=[pltpu.VMEM((2, PEER_TOPK, T_TOPK), F32), pltpu.VMEM((2, PEER_TOPK, T_TOPK), jnp.int32),
                        pltpu.VMEM((N_CAND, T_TOPK), F32), pltpu.VMEM((N_CAND, T_TOPK), jnp.int32),
                        pltpu.VMEM((nk, T_TOPK), jnp.int32), pltpu.VMEM((nk, T_TOPK), F32)],
        compiler_params=_cparams(("parallel",)),
        name="topk",
    )(pq, sk)


def _gather_rows(e_ref, tab_ref, w_ref, t):
    for k in range(e_ref.shape[1]):
        e0 = pl.multiple_of(e_ref[t, k], ROWS_PER_EXPERT)
        w_ref[k * ROWS_PER_EXPERT:(k + 1) * ROWS_PER_EXPERT, :] = tab_ref[pl.ds(e0, ROWS_PER_EXPERT), :]


def _for_each_token(e_ref, tab_ref, w_refs, compute):
    n = e_ref.shape[0]
    _gather_rows(e_ref, tab_ref, w_refs[0], 0)
    for t in range(n):
        if t + 1 < n:
            _gather_rows(e_ref, tab_ref, w_refs[(t + 1) % 2], t + 1)
        compute(t, pltpu.bitcast(w_refs[t % 2][...], BF16))


def _diag16(n):
    lane = lax.broadcasted_iota(jnp.int32, (2 * SUBLANES, n), 1)
    row = lax.broadcasted_iota(jnp.int32, (2 * SUBLANES, n), 0)
    return (lane & (SUBLANES - 1)) == (row & (SUBLANES - 1)), row < SUBLANES


def _peer_u_kernel(e_ref, tab_ref, x_ref, gate_ref, g8_ref, a_ref, w0_ref, w1_ref, hs_ref):
    diag, _ = _diag16(SUBLANES * e_ref.shape[1])

    def token(t, w):
        x16 = pltpu.bitcast(x_ref[t * SUBLANES:(t + 1) * SUBLANES, :], BF16)
        r = _dot_nt(x16, w)
        hs_ref[t:t + 1, :] = jnp.sum(jnp.where(diag, r, 0.0), axis=0, keepdims=True)

    _for_each_token(e_ref, tab_ref, (w0_ref, w1_ref), token)
    h = jnp.dot(hs_ref[...], g8_ref[...], preferred_element_type=F32, precision=lax.Precision.HIGHEST)
    a_ref[...] = gate_ref[...] * _gelu(h)


def _peer_u(eidx, tab, xw, gate, g8):
    t, nk = eidx.shape
    row = lambda i: (i, 0)
    fixed = lambda i: (0, 0)
    return pl.pallas_call(
        _peer_u_kernel,
        grid=(t // T_PEER,),
        in_specs=[pl.BlockSpec((T_PEER, nk), row, memory_space=pltpu.SMEM),
                  pl.BlockSpec(tab.shape, fixed, pipeline_mode=pl.Buffered(1)),
                  pl.BlockSpec((T_PEER * SUBLANES, LANES), row),
                  pl.BlockSpec((T_PEER, nk), row),
                  pl.BlockSpec(g8.shape, fixed)],
        out_specs=pl.BlockSpec((T_PEER, nk), row),
        out_shape=jax.ShapeDtypeStruct((t, nk), F32),
        scratch_shapes=[pltpu.VMEM((ROWS_PER_EXPERT * nk, LANES), jnp.uint32),
                        pltpu.VMEM((ROWS_PER_EXPERT * nk, LANES), jnp.uint32),
                        pltpu.VMEM((T_PEER, SUBLANES * nk), F32)],
        compiler_params=_cparams(("arbitrary",)),
        name="peer_u",
    )(eidx, tab, xw, gate, g8)


def _peer_v_kernel(e_ref, tab_ref, a_ref, rep_ref, o_ref, w0_ref, w1_ref, arep_ref):
    n = SUBLANES * e_ref.shape[1]
    diag, top = _diag16(n)
    arep_ref[...] = jnp.dot(a_ref[...], rep_ref[...], preferred_element_type=F32, precision=lax.Precision.HIGHEST)

    def token(t, w):
        l32 = jnp.where(diag, jnp.broadcast_to(arep_ref[t:t + 1, :], (2 * SUBLANES, n)), 0.0)
        hi = l32.astype(BF16).astype(F32)
        lhs = jnp.where(top, hi, l32 - hi).astype(BF16)
        out = _dot(lhs, w)
        o_ref[t * SUBLANES:(t + 1) * SUBLANES, :] = out[0:SUBLANES] + out[SUBLANES:2 * SUBLANES]

    _for_each_token(e_ref, tab_ref, (w0_ref, w1_ref), token)


def _peer_v(eidx, tab, a, rep):
    t, nk = eidx.shape
    row = lambda i: (i, 0)
    fixed = lambda i: (0, 0)
    return pl.pallas_call(
        _peer_v_kernel,
        grid=(t // T_PEER,),
        in_specs=[pl.BlockSpec((T_PEER, nk), row, memory_space=pltpu.SMEM),
                  pl.BlockSpec(tab.shape, fixed, pipeline_mode=pl.Buffered(1)),
                  pl.BlockSpec((T_PEER, nk), row),
                  pl.BlockSpec(rep.shape, fixed)],
        out_specs=pl.BlockSpec((T_PEER * SUBLANES, LANES), row),
        out_shape=jax.ShapeDtypeStruct((t * SUBLANES, LANES), F32),
        scratch_shapes=[pltpu.VMEM((ROWS_PER_EXPERT * nk, LANES), jnp.uint32),
                        pltpu.VMEM((ROWS_PER_EXPERT * nk, LANES), jnp.uint32),
                        pltpu.VMEM((T_PEER, SUBLANES * nk), F32)],
        compiler_params=_cparams(("arbitrary",)),
        name="peer_v",
    )(eidx, tab, a, rep)


def _final_kernel(h1_ref, p_ref, g_ref, o_ref):
    o_ref[...] = _rms(h1_ref[...] + p_ref[...], g_ref[...])


def _final(h1, peer, g):
    t = h1.shape[0]
    row = lambda i: (i, 0)
    return pl.pallas_call(
        _final_kernel,
        grid=(t // T_PROJ,),
        in_specs=[pl.BlockSpec((T_PROJ, D_MODEL), row), pl.BlockSpec((T_PROJ, D_MODEL), row),
                  pl.BlockSpec((1, D_MODEL), lambda i: (0, 0))],
        out_specs=pl.BlockSpec((T_PROJ, D_MODEL), row),
        out_shape=jax.ShapeDtypeStruct((t, D_MODEL), F32),
        compiler_params=_cparams(("parallel",)),
        name="final",
    )(h1, peer, g)


def _pack_rows(a):
    r, n, _ = a.shape
    bits = lax.bitcast_convert_type(a, jnp.uint16).astype(jnp.uint32).reshape(r, n // 2, 2, LANES)
    return (bits[:, :, 0] | (bits[:, :, 1] << 16)).reshape(r * n // 2, LANES)


def _pad_lanes(a, left):
    z = jnp.zeros_like(a)
    return jnp.concatenate([a, z] if left else [z, a], axis=-1)


def _layer(h, l, ln1, w_in, conv_w, cmp_pos_k, cmp_pos_v, cmp_k_w1, cmp_k_w2, cmp_v_w1, cmp_v_w2,
           gn_conv, gn_attn, w_out, ln2, peer_wq, peer_subkeys, peer_u, peer_v):
    b, s, _ = h.shape
    t = b * s
    x2 = h.reshape(t, D_MODEL)
    w = w_in[l]
    o_q = 3 * CONV_CH
    o_kv = o_q + ATTN_W
    o_g = o_kv + 6 * N_KV * HEAD_DIM
    wq_heads = w[:, o_q:o_kv].reshape(D_MODEL, N_HEADS, HEAD_DIM)
    wq_pad = _pad_lanes(wq_heads, True).reshape(D_MODEL, N_HEADS * LANES)
    wg_pad = jnp.pad(w[:, o_g:], ((0, 0), (0, LANES - N_GATES)))
    w_cat = jnp.concatenate([w[:, :o_q], wq_pad, w[:, o_kv:o_g], wg_pad], axis=1).astype(BF16)
    conv, q, kcvc, kv4, gates = _inproj(x2, ln1[l][None, :], w_cat)

    nc = s // CMP_STRIDE

    def chunks(a):
        a = a.reshape(b, nc, CMP_STRIDE, N_KV, HEAD_DIM).transpose(0, 3, 1, 2, 4)
        return a.reshape(b, N_KV, nc, CMP_STRIDE * HEAD_DIM)

    pos2 = lambda p: p.reshape(2, CMP_STRIDE * HEAD_DIM)
    kcc, vcc = _compress(chunks(kcvc[:, :LANES]), chunks(kcvc[:, LANES:]), pos2(cmp_pos_k[l]), pos2(cmp_pos_v[l]),
                         cmp_k_w1[l].astype(BF16), cmp_k_w2[l].astype(BF16),
                         cmp_v_w1[l].astype(BF16), cmp_v_w2[l].astype(BF16))

    attn = _attention(b, s, q, kv4, kcc, vcc, gates)

    h1, xh, xl, pq = _post(x2, conv, attn.reshape(t, ATTN_W), conv_w[l], gn_conv[l][None, :], gn_attn[l][None, :],
                           w_out[l].astype(BF16), ln2[l][None, :], peer_wq[l].astype(BF16), s)

    eidx, gate = _topk(pq, peer_subkeys[l].astype(BF16))
    nk = PEER_HEADS * PEER_TOPK
    g8 = (jnp.arange(SUBLANES * nk)[:, None] // SUBLANES == jnp.arange(nk)[None, :]).astype(F32)
    table = lambda w: _pack_rows(w.astype(BF16).reshape(PEER_EXPERTS, SUBLANES, LANES))
    xw = _pack_rows(jnp.concatenate([xh.reshape(t, SUBLANES, LANES), xl.reshape(t, SUBLANES, LANES)], axis=1))
    a = _peer_u(eidx, table(peer_u[l]), xw, gate, g8)
    peer = _peer_v(eidx, table(peer_v[l]), a, g8.T).reshape(t, D_MODEL)
    return h1, peer


def _attention(b, s, q, kv4, kcc, vcc, gates):
    nc = s // CMP_STRIDE

    def group(a):
        return a.reshape(b, s, N_KV, HEAD_DIM).transpose(0, 2, 1, 3)

    ks, vs, kw, vw = (group(kv4[:, i * LANES:(i + 1) * LANES]) for i in range(4))
    n_sel = s // SEL_BLOCK
    blk_of_key = jnp.arange(s) // SEL_BLOCK
    onehot = (blk_of_key[:, None] == jnp.arange(n_sel)[None, :]).astype(BF16)
    onehot = jnp.pad(onehot, ((0, 0), (0, HEAD_DIM - n_sel))) if n_sel < HEAD_DIM else onehot
    ones = jnp.ones((b, N_KV, s, HEAD_DIM), BF16)
    kse = jnp.concatenate([ks, jnp.broadcast_to(onehot, (b, N_KV, s, HEAD_DIM))], axis=-1)
    kw0 = _pad_lanes(kw, True)
    kc0 = _pad_lanes(kcc, True)

    def values_t(v):
        v1 = jnp.concatenate([v, ones], axis=-1).reshape(b, N_KV, s // KC, KC, LANES)
        return v1.transpose(0, 1, 2, 4, 3)

    vs1t, vw1t = values_t(vs), values_t(vw)
    vct = vcc.transpose(0, 1, 3, 2)
    gates_t = gates.reshape(b, s, LANES)[:, :, :4 * SUBLANES].transpose(0, 2, 1)
    n_cmp = (s - CMP_BLOCK) // CMP_STRIDE + 1
    cs = np.arange(nc) * CMP_STRIDE
    ss = np.arange(HEAD_DIM) * SEL_BLOCK
    ov = ((cs[:, None] < ss[None, :] + SEL_BLOCK) & (cs[:, None] + CMP_BLOCK > ss[None, :])
          & (np.arange(nc)[:, None] < n_cmp) & (np.arange(HEAD_DIM)[None, :] < n_sel))
    ovt = jnp.asarray(ov.T.astype(np.float32))
    return _nsa(q.reshape(b, s, N_HEADS * LANES), kse, vs1t, kw0, vw1t, kc0, vct, gates_t, ovt)


def kernel(x, ln1, w_in, conv_w, cmp_pos_k, cmp_pos_v, cmp_k_w1, cmp_k_w2, cmp_v_w1, cmp_v_w2, gn_conv, gn_attn,
           w_out, ln2, peer_wq, peer_subkeys, peer_u, peer_v, ln_f):
    b, s, _ = x.shape
    depth = w_in.shape[0]
    h = x
    for l in range(depth):
        h1, peer = _layer(h, l, ln1, w_in, conv_w, cmp_pos_k, cmp_pos_v, cmp_k_w1, cmp_k_w2, cmp_v_w1, cmp_v_w2,
                          gn_conv, gn_attn, w_out, ln2, peer_wq, peer_subkeys, peer_u, peer_v)
        if l + 1 < depth:
            h = (h1 + peer).reshape(b, s, D_MODEL)
    return _final(h1, peer, ln_f[None, :]).reshape(b, s, D_MODEL)
```

```python
import functools
import math

import jax
import jax.numpy as jnp
import numpy as np
from jax import lax
from jax.experimental import pallas as pl
from jax.experimental.pallas import tpu as pltpu

F32 = jnp.float32
BF16 = jnp.bfloat16

D_MODEL = 1024
CONV_CH = 512
CONV_K = 3
N_HEADS = 8
HEAD_DIM = 64
N_KV = 2
HPG = N_HEADS // N_KV
ATTN_W = N_HEADS * HEAD_DIM
CMP_BLOCK = 32
CMP_STRIDE = 16
CMP_HIDDEN = 256
SEL_BLOCK = 64
SEL_TOP = 16
WINDOW = 512
N_GATES = 3 * N_HEADS
PEER_HEADS = 8
PEER_NKEYS = 128
PEER_EXPERTS = PEER_NKEYS * PEER_NKEYS
PEER_DKEY = 256
PEER_TOPK = 16
EPS = 1e-6
NEG_INF = -1e30
FORCE = 1e4

LANES = 128
SUBLANES = 8
VMEM_LIMIT = 56 * 1024 * 1024

TQ = 256
KC = 256
T_PROJ = 256
T_TOPK = 256
T_PEER = 64
HALF = D_MODEL // 2
ROWS_PER_EXPERT = HALF // LANES


def _cparams(sem):
    return pltpu.CompilerParams(dimension_semantics=sem, vmem_limit_bytes=VMEM_LIMIT)


def _dot_nt(a, b, precision=None):
    return lax.dot_general(a, b, (((1,), (1,)), ((), ())), preferred_element_type=F32, precision=precision)


def _dot(a, b):
    return jnp.dot(a, b, preferred_element_type=F32)


def _rms(x, g):
    return x * lax.rsqrt(jnp.mean(x * x, axis=-1, keepdims=True) + EPS) * g


def _gelu(x):
    c = math.sqrt(2.0 / math.pi)
    return 0.5 * x * (1.0 + jnp.tanh(c * (x + 0.044715 * (x * x * x))))


def _inproj_kernel(x_ref, ln1_ref, w_ref, conv_ref, q_ref, kcvc_ref, kv4_ref, gates_ref):
    xn = _rms(x_ref[...], ln1_ref[...]).astype(BF16)
    n_conv, n_q = 3 * CONV_CH, N_HEADS * LANES
    o = 0
    conv_ref[...] = _dot(xn, w_ref[:, o:o + n_conv])
    o += n_conv
    q_ref[...] = (_dot(xn, w_ref[:, o:o + n_q]) * (HEAD_DIM ** -0.5)).astype(BF16)
    o += n_q
    kcvc_ref[...] = _dot(xn, w_ref[:, o:o + 2 * LANES])
    o += 2 * LANES
    kv4_ref[...] = _dot(xn, w_ref[:, o:o + 4 * LANES]).astype(BF16)
    o += 4 * LANES
    gates_ref[...] = jax.nn.sigmoid(_dot(xn, w_ref[:, o:o + LANES]))


def _inproj(x2, ln1, w_cat):
    t = x2.shape[0]
    ncols = w_cat.shape[1]
    row = lambda i: (i, 0)
    fixed = lambda i: (0, 0)
    return pl.pallas_call(
        _inproj_kernel,
        grid=(t // T_PROJ,),
        in_specs=[pl.BlockSpec((T_PROJ, D_MODEL), row),
                  pl.BlockSpec((1, D_MODEL), fixed),
                  pl.BlockSpec((D_MODEL, ncols), fixed)],
        out_specs=[pl.BlockSpec((T_PROJ, 3 * CONV_CH), row),
                   pl.BlockSpec((T_PROJ, N_HEADS * LANES), row),
                   pl.BlockSpec((T_PROJ, 2 * LANES), row),
                   pl.BlockSpec((T_PROJ, 4 * LANES), row),
                   pl.BlockSpec((T_PROJ, LANES), row)],
        out_shape=[jax.ShapeDtypeStruct((t, 3 * CONV_CH), F32),
                   jax.ShapeDtypeStruct((t, N_HEADS * LANES), BF16),
                   jax.ShapeDtypeStruct((t, 2 * LANES), F32),
                   jax.ShapeDtypeStruct((t, 4 * LANES), BF16),
                   jax.ShapeDtypeStruct((t, LANES), F32)],
        compiler_params=_cparams(("parallel",)),
        name="inproj",
    )(x2, ln1, w_cat)


def _compress_kernel(ck_ref, cv_ref, posk_ref, posv_ref, w1k_ref, w2k_ref, w1v_ref, w2v_ref, ok_ref, ov_ref):
    half = CMP_STRIDE * HEAD_DIM
    for c_ref, pos_ref, w1_ref, w2_ref, o_ref in ((ck_ref, posk_ref, w1k_ref, w2k_ref, ok_ref),
                                                  (cv_ref, posv_ref, w1v_ref, w2v_ref, ov_ref)):
        for g in range(N_KV):
            c = c_ref[0, g]
            nc = c.shape[0]
            a = _dot((c + pos_ref[0:1, :]).astype(BF16), w1_ref[0:half, :])
            b = _dot((c + pos_ref[1:2, :]).astype(BF16), w1_ref[half:2 * half, :])
            hid = a + pltpu.roll(b, nc - 1, 0)
            out = _dot(_gelu(hid).astype(BF16), w2_ref[...])
            rows = lax.broadcasted_iota(jnp.int32, out.shape, 0)
            o_ref[0, g] = jnp.where(rows < nc - 1, out, 0.0).astype(BF16)


def _compress(ck, cv, posk, posv, w1k, w2k, w1v, w2v):
    b, g, nc, cw = ck.shape
    blk = lambda i: (i, 0, 0, 0)
    fixed = lambda i: (0, 0)
    return pl.pallas_call(
        _compress_kernel,
        grid=(b,),
        in_specs=[pl.BlockSpec((1, g, nc, cw), blk), pl.BlockSpec((1, g, nc, cw), blk),
                  pl.BlockSpec(posk.shape, fixed), pl.BlockSpec(posv.shape, fixed),
                  pl.BlockSpec(w1k.shape, fixed), pl.BlockSpec(w2k.shape, fixed),
                  pl.BlockSpec(w1v.shape, fixed), pl.BlockSpec(w2v.shape, fixed)],
        out_specs=[pl.BlockSpec((1, g, nc, HEAD_DIM), blk), pl.BlockSpec((1, g, nc, HEAD_DIM), blk)],
        out_shape=[jax.ShapeDtypeStruct((b, g, nc, HEAD_DIM), BF16)] * 2,
        compiler_params=_cparams(("parallel",)),
        name="compress",
    )(ck, cv, posk, posv, w1k, w2k, w1v, w2v)


def _nsa_kernel(q_ref, kse_ref, vs1t_ref, kw0_ref, vw1t_ref, kc0_ref, vct_ref, gt_ref, ovt_ref, o_ref,
                m_ref, acc_ref):
    qt = pl.program_id(1)
    t0 = qt * TQ
    nc = kc0_ref.shape[2]
    n_sel = ovt_ref.shape[0]
    w = HPG * TQ
    tq = t0 + (lax.broadcasted_iota(jnp.int32, (1, w), 1) & (TQ - 1))
    krow = lax.broadcasted_iota(jnp.int32, (KC, 1), 0)
    causal = (t0 + krow) <= tq
    band = (t0 - 2 * KC + krow) > (tq - WINDOW)
    eye = (lax.broadcasted_iota(jnp.int32, (TQ, TQ), 0)
           == lax.broadcasted_iota(jnp.int32, (TQ, TQ), 1)).astype(BF16)
    nrow = lax.broadcasted_iota(jnp.int32, (nc, 1), 0)
    valid_c = ((nrow * CMP_STRIDE + (CMP_BLOCK - 1)) <= tq) & (nrow < nc - 1)
    jrow = lax.broadcasted_iota(jnp.int32, (n_sel, TQ), 0)
    qblk = jnp.right_shift(t0 + lax.broadcasted_iota(jnp.int32, (n_sel, TQ), 1), int(math.log2(SEL_BLOCK)))
    forced = (jrow == 0) | (jrow == qblk) | (jrow == qblk - 1)
    lane128 = lax.broadcasted_iota(jnp.int32, (TQ, LANES), 1)
    gt = gt_ref[0]

    def online(s, vt):
        m_old = m_ref[0:1, :]
        m_new = jnp.maximum(m_old, jnp.max(s, axis=0, keepdims=True))
        alpha = jnp.exp(m_old - m_new)
        p = jnp.exp(s - m_new)
        acc_ref[...] = alpha * acc_ref[...] + _dot(vt, p.astype(BF16))
        m_ref[...] = jnp.broadcast_to(m_new, (SUBLANES, w))

    def first(s, vt):
        m = jnp.max(s, axis=0, keepdims=True)
        acc_ref[...] = _dot(vt, jnp.exp(s - m).astype(BF16))
        m_ref[...] = jnp.broadcast_to(m, (SUBLANES, w))

    def finish():
        a = acc_ref[...]
        return a[0:HEAD_DIM, :] / a[HEAD_DIM:HEAD_DIM + 1, :]

    for g in range(N_KV):
        kc0 = kc0_ref[0, g]
        q_all = jnp.concatenate([q_ref[0, :, (g * HPG + h) * LANES:(g * HPG + h + 1) * LANES]
                                 for h in range(HPG)], axis=0)
        s = jnp.where(valid_c, _dot_nt(kc0, q_all), NEG_INF)
        m = jnp.max(s, axis=0, keepdims=True)
        p = jnp.where(valid_c, jnp.exp(s - m), 0.0)
        l = jnp.sum(p, axis=0, keepdims=True)
        pn = p * (1.0 / jnp.where(l > 0.0, l, 1.0))
        o_c = _dot(vct_ref[0, g], pn.astype(BF16))
        psum = pn[:, 0:TQ]
        for h in range(1, HPG):
            psum = psum + pn[:, h * TQ:(h + 1) * TQ]
        imp_t = jnp.dot(ovt_ref[...], psum, preferred_element_type=F32,
                        precision=lax.Precision.HIGHEST)
        val = jnp.where(jrow > qblk, -FORCE, imp_t + jnp.where(forced, FORCE, 0.0))
        rank = jnp.zeros((n_sel, TQ), jnp.int32)
        for k in range(n_sel):
            vk = val[k:k + 1, :]
            ahead = (vk > val) | ((vk == val) & (jrow > k))
            rank = rank + ahead.astype(jnp.int32)
        sel_t = (rank < SEL_TOP).astype(BF16)
        pad_t = jnp.concatenate([jnp.zeros((LANES - n_sel, TQ), BF16), sel_t], axis=0)
        sel_q = _dot_nt(eye, pad_t)
        bias = jnp.where((lane128 >= HEAD_DIM) & (sel_q < 0.5), NEG_INF, 0.0).astype(BF16)
        lhs = q_all + jnp.concatenate([bias] * HPG, axis=0)
        kd = kse_ref[0, g, pl.ds(pl.multiple_of(t0, KC), KC), :]
        first(jnp.where(causal, _dot_nt(kd, lhs), NEG_INF), vs1t_ref[0, g, qt])

        def sel_body(c, carry):
            kk = kse_ref[0, g, pl.ds(pl.multiple_of(c * KC, KC), KC), :]
            online(_dot_nt(kk, lhs), vs1t_ref[0, g, c])
            return carry

        lax.fori_loop(0, qt, sel_body, 0)
        o_s = finish()
        kd = kw0_ref[0, g, pl.ds(pl.multiple_of(t0, KC), KC), :]
        first(jnp.where(causal, _dot_nt(kd, q_all), NEG_INF), vw1t_ref[0, g, qt])

        @pl.when(qt >= 1)
        def _():
            kk = kw0_ref[0, g, pl.ds(pl.multiple_of(t0 - KC, KC), KC), :]
            online(_dot_nt(kk, q_all), vw1t_ref[0, g, qt - 1])

        @pl.when(qt >= 2)
        def _():
            kk = kw0_ref[0, g, pl.ds(pl.multiple_of(t0 - 2 * KC, KC), KC), :]
            online(jnp.where(band, _dot_nt(kk, q_all), NEG_INF), vw1t_ref[0, g, qt - 2])

        o_w = finish()
        outs = []
        for h in range(HPG):
            r = 3 * (g * HPG + h)
            c0, c1 = h * TQ, (h + 1) * TQ
            outs.append(gt[r:r + 1, :] * o_c[:, c0:c1] + gt[r + 1:r + 2, :] * o_s[:, c0:c1]
                        + gt[r + 2:r + 3, :] * o_w[:, c0:c1])
        o_ref[0, :, g * HPG * HEAD_DIM:(g + 1) * HPG * HEAD_DIM] = jnp.concatenate(outs, axis=0).T


def _nsa(q, kse, vs1t, kw0, vw1t, kc0, vct, gates_t, ovt):
    b, s, _ = q.shape
    nc = kc0.shape[2]
    seq = lambda i, j: (i, 0, 0, 0)
    seq5 = lambda i, j: (i, 0, 0, 0, 0)
    tile = lambda i, j: (i, j, 0)
    fixed = lambda i, j: (0, 0)
    return pl.pallas_call(
        _nsa_kernel,
        grid=(b, s // TQ),
        in_specs=[pl.BlockSpec((1, TQ, N_HEADS * LANES), tile),
                  pl.BlockSpec((1, N_KV, s, LANES), seq), pl.BlockSpec((1, N_KV, s // KC, LANES, KC), seq5),
                  pl.BlockSpec((1, N_KV, s, LANES), seq), pl.BlockSpec((1, N_KV, s // KC, LANES, KC), seq5),
                  pl.BlockSpec((1, N_KV, nc, LANES), seq), pl.BlockSpec((1, N_KV, HEAD_DIM, nc), seq),
                  pl.BlockSpec((1, 4 * SUBLANES, TQ), lambda i, j: (i, 0, j)),
                  pl.BlockSpec(ovt.shape, fixed)],
        out_specs=pl.BlockSpec((1, TQ, ATTN_W), tile),
        out_shape=jax.ShapeDtypeStruct((b, s, ATTN_W), F32),
        scratch_shapes=[pltpu.VMEM((SUBLANES, HPG * TQ), F32), pltpu.VMEM((LANES, HPG * TQ), F32)],
        compiler_params=_cparams(("parallel", "arbitrary")),
        name="nsa",
    )(q, kse, vs1t, kw0, vw1t, kc0, vct, gates_t, ovt)


def _post_kernel(x_ref, conv_ref, halo_ref, attn_ref, convw_ref, gnc_ref, gna_ref, wout_ref, ln2_ref, wq_ref,
                 h1_ref, xh_ref, xl_ref, pq_ref, *, tiles_per_seq):
    i = pl.program_id(0)
    c_h = conv_ref[:, 0:CONV_CH]
    c_b = conv_ref[:, CONV_CH:2 * CONV_CH]
    c_c = conv_ref[:, 2 * CONV_CH:3 * CONV_CH]
    z = c_c * c_h
    keep = jnp.where(i % tiles_per_seq == 0, 0.0, 1.0)
    zp = halo_ref[:, 2 * CONV_CH:3 * CONV_CH] * halo_ref[:, 0:CONV_CH] * keep
    rows = lax.broadcasted_iota(jnp.int32, z.shape, 0)
    n = z.shape[0]
    z1 = jnp.where(rows == 0, zp[SUBLANES - 1:SUBLANES, :], pltpu.roll(z, 1, 0))
    z2 = jnp.where(rows == 0, zp[SUBLANES - 2:SUBLANES - 1, :],
                   jnp.where(rows == 1, zp[SUBLANES - 1:SUBLANES, :], pltpu.roll(z, 2, 0)))
    conv = convw_ref[0:1, :] * z2 + convw_ref[1:2, :] * z1 + convw_ref[2:3, :] * z
    nc = _rms(c_b * conv, gnc_ref[...]).astype(BF16)
    na = _rms(attn_ref[...], gna_ref[...]).astype(BF16)
    h1 = x_ref[...] + _dot(nc, wout_ref[0:CONV_CH, :]) + _dot(na, wout_ref[CONV_CH:CONV_CH + ATTN_W, :])
    h1_ref[...] = h1
    hn = _rms(h1, ln2_ref[...])
    xh = hn.astype(BF16)
    xh_ref[...] = xh
    xl_ref[...] = (hn - xh.astype(F32)).astype(BF16)
    pq_ref[...] = _dot(xh, wq_ref[...]).astype(BF16)


def _post(x2, conv, attn2, convw, gnc, gna, wout, ln2, wq, seq_len):
    t = x2.shape[0]
    row = lambda i: (i, 0)
    fixed = lambda i: (0, 0)
    halo = lambda i: (jnp.maximum(i * (T_PROJ // SUBLANES) - 1, 0), 0)
    nq = wq.shape[1]
    return pl.pallas_call(
        functools.partial(_post_kernel, tiles_per_seq=seq_len // T_PROJ),
        grid=(t // T_PROJ,),
        in_specs=[pl.BlockSpec((T_PROJ, D_MODEL), row),
                  pl.BlockSpec((T_PROJ, 3 * CONV_CH), row),
                  pl.BlockSpec((SUBLANES, 3 * CONV_CH), halo),
                  pl.BlockSpec((T_PROJ, ATTN_W), row),
                  pl.BlockSpec(convw.shape, fixed), pl.BlockSpec(gnc.shape, fixed), pl.BlockSpec(gna.shape, fixed),
                  pl.BlockSpec(wout.shape, fixed), pl.BlockSpec(ln2.shape, fixed), pl.BlockSpec(wq.shape, fixed)],
        out_specs=[pl.BlockSpec((T_PROJ, D_MODEL), row), pl.BlockSpec((T_PROJ, D_MODEL), row),
                   pl.BlockSpec((T_PROJ, D_MODEL), row), pl.BlockSpec((T_PROJ, nq), row)],
        out_shape=[jax.ShapeDtypeStruct((t, D_MODEL), F32), jax.ShapeDtypeStruct((t, D_MODEL), BF16),
                   jax.ShapeDtypeStruct((t, D_MODEL), BF16), jax.ShapeDtypeStruct((t, nq), BF16)],
        compiler_params=_cparams(("parallel",)),
        name="post",
    )(x2, conv, conv, attn2, convw, gnc, gna, wout, ln2, wq)


def _staircase():
    return [(a, b) for a in range(PEER_TOPK) for b in range(PEER_TOPK) if (a + 1) * (b + 1) <= PEER_TOPK]


N_CAND = 56


def _topk_kernel(pq_ref, sk_ref, e_ref, g_ref, sv_ref, si_ref, cand_ref, ce_ref, et_ref, gt_ref):
    tt = pq_ref.shape[0]
    rown = lax.broadcasted_iota(jnp.int32, (PEER_NKEYS, tt), 0)
    rowc = lax.broadcasted_iota(jnp.int32, (N_CAND, tt), 0)
    pairs = _staircase()
    for h in range(PEER_HEADS):
        for c in range(2):
            off = (h * 2 + c) * PEER_NKEYS
            x = _dot_nt(sk_ref[h, c], pq_ref[:, off:off + PEER_NKEYS])
            for it in range(PEER_TOPK):
                m = jnp.max(x, axis=0, keepdims=True)
                idx = jnp.min(jnp.where(x == m, rown, PEER_NKEYS), axis=0, keepdims=True)
                sv_ref[c, it:it + 1, :] = m
                si_ref[c, it:it + 1, :] = idx
                x = jnp.where(rown == idx, -jnp.inf, x)
        cand_ref[...] = jnp.full((N_CAND, tt), -jnp.inf, F32)
        ce_ref[...] = jnp.zeros((N_CAND, tt), jnp.int32)
        for r, (a, b) in enumerate(pairs):
            cand_ref[r:r + 1, :] = sv_ref[0, a:a + 1, :] + sv_ref[1, b:b + 1, :]
            ce_ref[r:r + 1, :] = si_ref[0, a:a + 1, :] * PEER_NKEYS + si_ref[1, b:b + 1, :]
        x = cand_ref[...]
        ce = ce_ref[...]
        best = []
        for it in range(PEER_TOPK):
            m = jnp.max(x, axis=0, keepdims=True)
            idx = jnp.min(jnp.where(x == m, rowc, N_CAND), axis=0, keepdims=True)
            hit = rowc == idx
            et_ref[h * PEER_TOPK + it:h * PEER_TOPK + it + 1, :] = jnp.max(jnp.where(hit, ce, -1), axis=0, keepdims=True)
            best.append(m)
            x = jnp.where(hit, -jnp.inf, x)
        ex = [jnp.exp(v - best[0]) for v in best]
        tot = ex[0]
        for v in ex[1:]:
            tot = tot + v
        inv = 1.0 / tot
        for it in range(PEER_TOPK):
            gt_ref[h * PEER_TOPK + it:h * PEER_TOPK + it + 1, :] = ex[it] * inv
    e_ref[...] = (et_ref[...] * ROWS_PER_EXPERT).T
    g_ref[...] = gt_ref[...].T


def _topk(pq, sk):
    t, nq = pq.shape
    nk = PEER_HEADS * PEER_TOPK
    row = lambda i: (i, 0)
    return pl.pallas_call(
        _topk_kernel,
        grid=(t // T_TOPK,),
        in_specs=[pl.BlockSpec((T_TOPK, nq), row), pl.BlockSpec(sk.shape, lambda i: (0, 0, 0, 0))],
        out_specs=[pl.BlockSpec((T_TOPK, nk), row), pl.BlockSpec((T_TOPK, nk), row)],
        out_shape=[jax.ShapeDtypeStruct((t, nk), jnp.int32), jax.ShapeDtypeStruct((t, nk), F32)],
        scratch_shapes=[pltpu.VMEM((2, PEER_TOPK, T_TOPK), F32), pltpu.VMEM((2, PEER_TOPK, T_TOPK), jnp.int32),
                        pltpu.VMEM((N_CAND, T_TOPK), F32), pltpu.VMEM((N_CAND, T_TOPK), jnp.int32),
                        pltpu.VMEM((nk, T_TOPK), jnp.int32), pltpu.VMEM((nk, T_TOPK), F32)],
        compiler_params=_cparams(("parallel",)),
        name="topk",
    )(pq, sk)


def _gather_rows(e_ref, tab_ref, w_ref, t):
    for k in range(e_ref.shape[1]):
        e0 = pl.multiple_of(e_ref[t, k], ROWS_PER_EXPERT)
        w_ref[k * ROWS_PER_EXPERT:(k + 1) * ROWS_PER_EXPERT, :] = tab_ref[pl.ds(e0, ROWS_PER_EXPERT), :]


def _for_each_token(e_hbm, e_refs, sem, tab_ref, w_refs, compute):
    i = pl.program_id(0)
    half = e_refs[0].shape[0]

    def copy(step, part):
        rows = pl.ds(pl.multiple_of((2 * step + part) * half, half), half)
        return pltpu.make_async_copy(e_hbm.at[rows], e_refs[part], sem.at[part])

    @pl.when(i == 0)
    def _():
        copy(0, 0).start()

    copy(i, 1).start()
    copy(i, 0).wait()
    _gather_rows(e_refs[0], tab_ref, w_refs[0], 0)
    for t in range(2 * half):
        if t + 1 == half:
            @pl.when(i + 1 < pl.num_programs(0))
            def _():
                copy(i + 1, 0).start()

            copy(i, 1).wait()
        if t + 1 < 2 * half:
            _gather_rows(e_refs[(t + 1) // half], tab_ref, w_refs[(t + 1) % 2], (t + 1) % half)
        compute(t, pltpu.bitcast(w_refs[t % 2][...], BF16))


def _diag16(n):
    lane = lax.broadcasted_iota(jnp.int32, (2 * SUBLANES, n), 1)
    row = lax.broadcasted_iota(jnp.int32, (2 * SUBLANES, n), 0)
    return (lane & (SUBLANES - 1)) == (row & (SUBLANES - 1)), row < SUBLANES


def _peer_u_kernel(e_hbm, tab_ref, x_ref, gate_ref, g8_ref, a_ref, e0_ref, e1_ref, sem, w0_ref, w1_ref, hs_ref):
    diag, _ = _diag16(SUBLANES * e0_ref.shape[1])

    def token(t, w):
        x16 = pltpu.bitcast(x_ref[t * SUBLANES:(t + 1) * SUBLANES, :], BF16)
        r = _dot_nt(x16, w)
        hs_ref[t:t + 1, :] = jnp.sum(jnp.where(diag, r, 0.0), axis=0, keepdims=True)

    _for_each_token(e_hbm, (e0_ref, e1_ref), sem, tab_ref, (w0_ref, w1_ref), token)
    h = jnp.dot(hs_ref[...], g8_ref[...], preferred_element_type=F32, precision=lax.Precision.HIGHEST)
    a_ref[...] = gate_ref[...] * _gelu(h)


def _peer_u(eidx, tab, xw, gate, g8):
    t, nk = eidx.shape
    row = lambda i: (i, 0)
    fixed = lambda i: (0, 0)
    return pl.pallas_call(
        _peer_u_kernel,
        grid=(t // T_PEER,),
        in_specs=[pl.BlockSpec(memory_space=pl.ANY),
                  pl.BlockSpec(tab.shape, fixed, pipeline_mode=pl.Buffered(1)),
                  pl.BlockSpec((T_PEER * SUBLANES, LANES), row),
                  pl.BlockSpec((T_PEER, nk), row),
                  pl.BlockSpec(g8.shape, fixed)],
        out_specs=pl.BlockSpec((T_PEER, nk), row),
        out_shape=jax.ShapeDtypeStruct((t, nk), F32),
        scratch_shapes=[pltpu.SMEM((T_PEER // 2, nk), jnp.int32), pltpu.SMEM((T_PEER // 2, nk), jnp.int32),
                        pltpu.SemaphoreType.DMA((2,)),
                        pltpu.VMEM((ROWS_PER_EXPERT * nk, LANES), jnp.uint32),
                        pltpu.VMEM((ROWS_PER_EXPERT * nk, LANES), jnp.uint32),
                        pltpu.VMEM((T_PEER, SUBLANES * nk), F32)],
        compiler_params=_cparams(("arbitrary",)),
        name="peer_u",
    )(eidx, tab, xw, gate, g8)


def _peer_v_kernel(e_hbm, tab_ref, a_ref, rep_ref, o_ref, e0_ref, e1_ref, sem, w0_ref, w1_ref, arep_ref):
    n = SUBLANES * e0_ref.shape[1]
    diag, top = _diag16(n)
    arep_ref[...] = jnp.dot(a_ref[...], rep_ref[...], preferred_element_type=F32, precision=lax.Precision.HIGHEST)

    def token(t, w):
        l32 = jnp.where(diag, jnp.broadcast_to(arep_ref[t:t + 1, :], (2 * SUBLANES, n)), 0.0)
        hi = l32.astype(BF16).astype(F32)
        lhs = jnp.where(top, hi, l32 - hi).astype(BF16)
        out = _dot(lhs, w)
        o_ref[t * SUBLANES:(t + 1) * SUBLANES, :] = out[0:SUBLANES] + out[SUBLANES:2 * SUBLANES]

    _for_each_token(e_hbm, (e0_ref, e1_ref), sem, tab_ref, (w0_ref, w1_ref), token)


def _peer_v(eidx, tab, a, rep):
    t, nk = eidx.shape
    row = lambda i: (i, 0)
    fixed = lambda i: (0, 0)
    return pl.pallas_call(
        _peer_v_kernel,
        grid=(t // T_PEER,),
        in_specs=[pl.BlockSpec(memory_space=pl.ANY),
                  pl.BlockSpec(tab.shape, fixed, pipeline_mode=pl.Buffered(1)),
                  pl.BlockSpec((T_PEER, nk), row),
                  pl.BlockSpec(rep.shape, fixed)],
        out_specs=pl.BlockSpec((T_PEER * SUBLANES, LANES), row),
        out_shape=jax.ShapeDtypeStruct((t * SUBLANES, LANES), F32),
        scratch_shapes=[pltpu.SMEM((T_PEER // 2, nk), jnp.int32), pltpu.SMEM((T_PEER // 2, nk), jnp.int32),
                        pltpu.SemaphoreType.DMA((2,)),
                        pltpu.VMEM((ROWS_PER_EXPERT * nk, LANES), jnp.uint32),
                        pltpu.VMEM((ROWS_PER_EXPERT * nk, LANES), jnp.uint32),
                        pltpu.VMEM((T_PEER, SUBLANES * nk), F32)],
        compiler_params=_cparams(("arbitrary",)),
        name="peer_v",
    )(eidx, tab, a, rep)


def _final_kernel(h1_ref, p_ref, g_ref, o_ref):
    o_ref[...] = _rms(h1_ref[...] + p_ref[...], g_ref[...])


def _final(h1, peer, g):
    t = h1.shape[0]
    row = lambda i: (i, 0)
    return pl.pallas_call(
        _final_kernel,
        grid=(t // T_PROJ,),
        in_specs=[pl.BlockSpec((T_PROJ, D_MODEL), row), pl.BlockSpec((T_PROJ, D_MODEL), row),
                  pl.BlockSpec((1, D_MODEL), lambda i: (0, 0))],
        out_specs=pl.BlockSpec((T_PROJ, D_MODEL), row),
        out_shape=jax.ShapeDtypeStruct((t, D_MODEL), F32),
        compiler_params=_cparams(("parallel",)),
        name="final",
    )(h1, peer, g)


def _pack_rows(a):
    r, n, _ = a.shape
    bits = lax.bitcast_convert_type(a, jnp.uint16).astype(jnp.uint32).reshape(r, n // 2, 2, LANES)
    return (bits[:, :, 0] | (bits[:, :, 1] << 16)).reshape(r * n // 2, LANES)


def _pad_lanes(a, left):
    z = jnp.zeros_like(a)
    return jnp.concatenate([a, z] if left else [z, a], axis=-1)


def _layer(h, l, ln1, w_in, conv_w, cmp_pos_k, cmp_pos_v, cmp_k_w1, cmp_k_w2, cmp_v_w1, cmp_v_w2,
           gn_conv, gn_attn, w_out, ln2, peer_wq, peer_subkeys, peer_u, peer_v):
    b, s, _ = h.shape
    t = b * s
    x2 = h.reshape(t, D_MODEL)
    w = w_in[l]
    o_q = 3 * CONV_CH
    o_kv = o_q + ATTN_W
    o_g = o_kv + 6 * N_KV * HEAD_DIM
    wq_heads = w[:, o_q:o_kv].reshape(D_MODEL, N_HEADS, HEAD_DIM)
    wq_pad = _pad_lanes(wq_heads, True).reshape(D_MODEL, N_HEADS * LANES)
    wg_pad = jnp.pad(w[:, o_g:], ((0, 0), (0, LANES - N_GATES)))
    w_cat = jnp.concatenate([w[:, :o_q], wq_pad, w[:, o_kv:o_g], wg_pad], axis=1).astype(BF16)
    conv, q, kcvc, kv4, gates = _inproj(x2, ln1[l][None, :], w_cat)

    nc = s // CMP_STRIDE

    def chunks(a):
        a = a.reshape(b, nc, CMP_STRIDE, N_KV, HEAD_DIM).transpose(0, 3, 1, 2, 4)
        return a.reshape(b, N_KV, nc, CMP_STRIDE * HEAD_DIM)

    pos2 = lambda p: p.reshape(2, CMP_STRIDE * HEAD_DIM)
    kcc, vcc = _compress(chunks(kcvc[:, :LANES]), chunks(kcvc[:, LANES:]), pos2(cmp_pos_k[l]), pos2(cmp_pos_v[l]),
                         cmp_k_w1[l].astype(BF16), cmp_k_w2[l].astype(BF16),
                         cmp_v_w1[l].astype(BF16), cmp_v_w2[l].astype(BF16))

    attn = _attention(b, s, q, kv4, kcc, vcc, gates)

    h1, xh, xl, pq = _post(x2, conv, attn.reshape(t, ATTN_W), conv_w[l], gn_conv[l][None, :], gn_attn[l][None, :],
                           w_out[l].astype(BF16), ln2[l][None, :], peer_wq[l].astype(BF16), s)

    eidx, gate = _topk(pq, peer_subkeys[l].astype(BF16))
    nk = PEER_HEADS * PEER_TOPK
    g8 = (jnp.arange(SUBLANES * nk)[:, None] // SUBLANES == jnp.arange(nk)[None, :]).astype(F32)
    table = lambda w: _pack_rows(w.astype(BF16).reshape(PEER_EXPERTS, SUBLANES, LANES))
    xw = _pack_rows(jnp.concatenate([xh.reshape(t, SUBLANES, LANES), xl.reshape(t, SUBLANES, LANES)], axis=1))
    a = _peer_u(eidx, table(peer_u[l]), xw, gate, g8)
    peer = _peer_v(eidx, table(peer_v[l]), a, g8.T).reshape(t, D_MODEL)
    return h1, peer


def _attention(b, s, q, kv4, kcc, vcc, gates):
    nc = s // CMP_STRIDE

    def group(a):
        return a.reshape(b, s, N_KV, HEAD_DIM).transpose(0, 2, 1, 3)

    ks, vs, kw, vw = (group(kv4[:, i * LANES:(i + 1) * LANES]) for i in range(4))
    n_sel = s // SEL_BLOCK
    blk_of_key = jnp.arange(s) // SEL_BLOCK
    onehot = (blk_of_key[:, None] == jnp.arange(n_sel)[None, :]).astype(BF16)
    onehot = jnp.pad(onehot, ((0, 0), (0, HEAD_DIM - n_sel))) if n_sel < HEAD_DIM else onehot
    ones = jnp.ones((b, N_KV, s, HEAD_DIM), BF16)
    kse = jnp.concatenate([ks, jnp.broadcast_to(onehot, (b, N_KV, s, HEAD_DIM))], axis=-1)
    kw0 = _pad_lanes(kw, True)
    kc0 = _pad_lanes(kcc, True)

    def values_t(v):
        v1 = jnp.concatenate([v, ones], axis=-1).reshape(b, N_KV, s // KC, KC, LANES)
        return v1.transpose(0, 1, 2, 4, 3)

    vs1t, vw1t = values_t(vs), values_t(vw)
    vct = vcc.transpose(0, 1, 3, 2)
    gates_t = gates.reshape(b, s, LANES)[:, :, :4 * SUBLANES].transpose(0, 2, 1)
    n_cmp = (s - CMP_BLOCK) // CMP_STRIDE + 1
    cs = np.arange(nc) * CMP_STRIDE
    ss = np.arange(HEAD_DIM) * SEL_BLOCK
    ov = ((cs[:, None] < ss[None, :] + SEL_BLOCK) & (cs[:, None] + CMP_BLOCK > ss[None, :])
          & (np.arange(nc)[:, None] < n_cmp) & (np.arange(HEAD_DIM)[None, :] < n_sel))
    ovt = jnp.asarray(ov.T.astype(np.float32))
    return _nsa(q.reshape(b, s, N_HEADS * LANES), kse, vs1t, kw0, vw1t, kc0, vct, gates_t, ovt)


def kernel(x, ln1, w_in, conv_w, cmp_pos_k, cmp_pos_v, cmp_k_w1, cmp_k_w2, cmp_v_w1, cmp_v_w2, gn_conv, gn_attn,
           w_out, ln2, peer_wq, peer_subkeys, peer_u, peer_v, ln_f):
    b, s, _ = x.shape
    depth = w_in.shape[0]
    h = x
    for l in range(depth):
        h1, peer = _layer(h, l, ln1, w_in, conv_w, cmp_pos_k, cmp_pos_v, cmp_k_w1, cmp_k_w2, cmp_v_w1, cmp_v_w2,
                          gn_conv, gn_attn, w_out, ln2, peer_wq, peer_subkeys, peer_u, peer_v)
        if l + 1 < depth:
            h = (h1 + peer).reshape(b, s, D_MODEL)
    return _final(h1, peer, ln_f[None, :]).reshape(b, s, D_MODEL)
```

```python
import functools
import math

import jax
import jax.numpy as jnp
import numpy as np
from jax import lax
from jax.experimental import pallas as pl
from jax.experimental.pallas import tpu as pltpu

F32 = jnp.float32
BF16 = jnp.bfloat16

D_MODEL = 1024
CONV_CH = 512
CONV_K = 3
N_HEADS = 8
HEAD_DIM = 64
N_KV = 2
HPG = N_HEADS // N_KV
ATTN_W = N_HEADS * HEAD_DIM
CMP_BLOCK = 32
CMP_STRIDE = 16
CMP_HIDDEN = 256
SEL_BLOCK = 64
SEL_TOP = 16
WINDOW = 512
N_GATES = 3 * N_HEADS
PEER_HEADS = 8
PEER_NKEYS = 128
PEER_EXPERTS = PEER_NKEYS * PEER_NKEYS
PEER_DKEY = 256
PEER_TOPK = 16
EPS = 1e-6
NEG_INF = -1e30
FORCE = 1e4

LANES = 128
SUBLANES = 8
VMEM_LIMIT = 56 * 1024 * 1024

TQ = 256
KC = 256
T_PROJ = 256
T_TOPK = 256
T_PEER = 64
HALF = D_MODEL // 2
ROWS_PER_EXPERT = HALF // LANES


def _cparams(sem):
    return pltpu.CompilerParams(dimension_semantics=sem, vmem_limit_bytes=VMEM_LIMIT)


def _dot_nt(a, b, precision=None):
    return lax.dot_general(a, b, (((1,), (1,)), ((), ())), preferred_element_type=F32, precision=precision)


def _dot(a, b):
    return jnp.dot(a, b, preferred_element_type=F32)


def _rms(x, g):
    return x * lax.rsqrt(jnp.mean(x * x, axis=-1, keepdims=True) + EPS) * g


def _gelu(x):
    c = math.sqrt(2.0 / math.pi)
    return 0.5 * x * (1.0 + jnp.tanh(c * (x + 0.044715 * (x * x * x))))


def _inproj_kernel(x_ref, ln1_ref, w_ref, conv_ref, q_ref, kcvc_ref, kv4_ref, gates_ref):
    xn = _rms(x_ref[...], ln1_ref[...]).astype(BF16)
    n_conv, n_q = 3 * CONV_CH, N_HEADS * LANES
    o = 0
    conv_ref[...] = _dot(xn, w_ref[:, o:o + n_conv])
    o += n_conv
    q_ref[...] = (_dot(xn, w_ref[:, o:o + n_q]) * (HEAD_DIM ** -0.5)).astype(BF16)
    o += n_q
    kcvc_ref[...] = _dot(xn, w_ref[:, o:o + 2 * LANES])
    o += 2 * LANES
    kv4_ref[...] = _dot(xn, w_ref[:, o:o + 4 * LANES]).astype(BF16)
    o += 4 * LANES
    gates_ref[...] = jax.nn.sigmoid(_dot(xn, w_ref[:, o:o + LANES]))


def _inproj(x2, ln1, w_cat):
    t = x2.shape[0]
    ncols = w_cat.shape[1]
    row = lambda i: (i, 0)
    fixed = lambda i: (0, 0)
    return pl.pallas_call(
        _inproj_kernel,
        grid=(t // T_PROJ,),
        in_specs=[pl.BlockSpec((T_PROJ, D_MODEL), row),
                  pl.BlockSpec((1, D_MODEL), fixed),
                  pl.BlockSpec((D_MODEL, ncols), fixed)],
        out_specs=[pl.BlockSpec((T_PROJ, 3 * CONV_CH), row),
                   pl.BlockSpec((T_PROJ, N_HEADS * LANES), row),
                   pl.BlockSpec((T_PROJ, 2 * LANES), row),
                   pl.BlockSpec((T_PROJ, 4 * LANES), row),
                   pl.BlockSpec((T_PROJ, LANES), row)],
        out_shape=[jax.ShapeDtypeStruct((t, 3 * CONV_CH), F32),
                   jax.ShapeDtypeStruct((t, N_HEADS * LANES), BF16),
                   jax.ShapeDtypeStruct((t, 2 * LANES), F32),
                   jax.ShapeDtypeStruct((t, 4 * LANES), BF16),
                   jax.ShapeDtypeStruct((t, LANES), F32)],
        compiler_params=_cparams(("parallel",)),
        name="inproj",
    )(x2, ln1, w_cat)


def _compress_kernel(ck_ref, cv_ref, posk_ref, posv_ref, w1k_ref, w2k_ref, w1v_ref, w2v_ref, ok_ref, ov_ref):
    half = CMP_STRIDE * HEAD_DIM
    for c_ref, pos_ref, w1_ref, w2_ref, o_ref in ((ck_ref, posk_ref, w1k_ref, w2k_ref, ok_ref),
                                                  (cv_ref, posv_ref, w1v_ref, w2v_ref, ov_ref)):
        for g in range(N_KV):
            c = c_ref[0, g]
            nc = c.shape[0]
            a = _dot((c + pos_ref[0:1, :]).astype(BF16), w1_ref[0:half, :])
            b = _dot((c + pos_ref[1:2, :]).astype(BF16), w1_ref[half:2 * half, :])
            hid = a + pltpu.roll(b, nc - 1, 0)
            out = _dot(_gelu(hid).astype(BF16), w2_ref[...])
            rows = lax.broadcasted_iota(jnp.int32, out.shape, 0)
            o_ref[0, g] = jnp.where(rows < nc - 1, out, 0.0).astype(BF16)


def _compress(ck, cv, posk, posv, w1k, w2k, w1v, w2v):
    b, g, nc, cw = ck.shape
    blk = lambda i: (i, 0, 0, 0)
    fixed = lambda i: (0, 0)
    return pl.pallas_call(
        _compress_kernel,
        grid=(b,),
        in_specs=[pl.BlockSpec((1, g, nc, cw), blk), pl.BlockSpec((1, g, nc, cw), blk),
                  pl.BlockSpec(posk.shape, fixed), pl.BlockSpec(posv.shape, fixed),
                  pl.BlockSpec(w1k.shape, fixed), pl.BlockSpec(w2k.shape, fixed),
                  pl.BlockSpec(w1v.shape, fixed), pl.BlockSpec(w2v.shape, fixed)],
        out_specs=[pl.BlockSpec((1, g, nc, HEAD_DIM), blk), pl.BlockSpec((1, g, nc, HEAD_DIM), blk)],
        out_shape=[jax.ShapeDtypeStruct((b, g, nc, HEAD_DIM), BF16)] * 2,
        compiler_params=_cparams(("parallel",)),
        name="compress",
    )(ck, cv, posk, posv, w1k, w2k, w1v, w2v)


def _nsa_kernel(q_ref, kse_ref, vs1t_ref, kw0_ref, vw1t_ref, kc0_ref, vct_ref, gt_ref, ovt_ref, o_ref,
                m_ref, acc_ref):
    qt = pl.program_id(1)
    t0 = qt * TQ
    nc = kc0_ref.shape[2]
    n_sel = ovt_ref.shape[0]
    w = HPG * TQ
    tq = t0 + (lax.broadcasted_iota(jnp.int32, (1, w), 1) & (TQ - 1))
    krow = lax.broadcasted_iota(jnp.int32, (KC, 1), 0)
    causal = (t0 + krow) <= tq
    band = (t0 - 2 * KC + krow) > (tq - WINDOW)
    eye = (lax.broadcasted_iota(jnp.int32, (TQ, TQ), 0)
           == lax.broadcasted_iota(jnp.int32, (TQ, TQ), 1)).astype(BF16)
    nrow = lax.broadcasted_iota(jnp.int32, (nc, 1), 0)
    valid_c = ((nrow * CMP_STRIDE + (CMP_BLOCK - 1)) <= tq) & (nrow < nc - 1)
    jrow = lax.broadcasted_iota(jnp.int32, (n_sel, TQ), 0)
    qblk = jnp.right_shift(t0 + lax.broadcasted_iota(jnp.int32, (n_sel, TQ), 1), int(math.log2(SEL_BLOCK)))
    forced = (jrow == 0) | (jrow == qblk) | (jrow == qblk - 1)
    lane128 = lax.broadcasted_iota(jnp.int32, (TQ, LANES), 1)
    gt = gt_ref[0]

    def online(s, vt):
        m_old = m_ref[0:1, :]
        m_new = jnp.maximum(m_old, jnp.max(s, axis=0, keepdims=True))
        alpha = jnp.exp(m_old - m_new)
        p = jnp.exp(s - m_new)
        acc_ref[...] = alpha * acc_ref[...] + _dot(vt, p.astype(BF16))
        m_ref[...] = jnp.broadcast_to(m_new, (SUBLANES, w))

    def first(s, vt):
        m = jnp.max(s, axis=0, keepdims=True)
        acc_ref[...] = _dot(vt, jnp.exp(s - m).astype(BF16))
        m_ref[...] = jnp.broadcast_to(m, (SUBLANES, w))

    def finish():
        a = acc_ref[...]
        return a[0:HEAD_DIM, :] / a[HEAD_DIM:HEAD_DIM + 1, :]

    for g in range(N_KV):
        kc0 = kc0_ref[0, g]
        q_all = jnp.concatenate([q_ref[0, :, (g * HPG + h) * LANES:(g * HPG + h + 1) * LANES]
                                 for h in range(HPG)], axis=0)
        s = jnp.where(valid_c, _dot_nt(kc0, q_all), NEG_INF)
        m = jnp.max(s, axis=0, keepdims=True)
        p = jnp.where(valid_c, jnp.exp(s - m), 0.0)
        l = jnp.sum(p, axis=0, keepdims=True)
        pn = p * (1.0 / jnp.where(l > 0.0, l, 1.0))
        o_c = _dot(vct_ref[0, g], pn.astype(BF16))
        psum = pn[:, 0:TQ]
        for h in range(1, HPG):
            psum = psum + pn[:, h * TQ:(h + 1) * TQ]
        imp_t = jnp.dot(ovt_ref[...], psum, preferred_element_type=F32,
                        precision=lax.Precision.HIGHEST)
        val = jnp.where(jrow > qblk, -FORCE, imp_t + jnp.where(forced, FORCE, 0.0))
        rank = jnp.zeros((n_sel, TQ), jnp.int32)
        for k in range(n_sel):
            vk = val[k:k + 1, :]
            ahead = (vk > val) | ((vk == val) & (jrow > k))
            rank = rank + ahead.astype(jnp.int32)
        sel_t = (rank < SEL_TOP).astype(BF16)
        pad_t = jnp.concatenate([jnp.zeros((LANES - n_sel, TQ), BF16), sel_t], axis=0)
        sel_q = _dot_nt(eye, pad_t)
        bias = jnp.where((lane128 >= HEAD_DIM) & (sel_q < 0.5), NEG_INF, 0.0).astype(BF16)
        lhs = q_all + jnp.concatenate([bias] * HPG, axis=0)
        kd = kse_ref[0, g, pl.ds(pl.multiple_of(t0, KC), KC), :]
        first(jnp.where(causal, _dot_nt(kd, lhs), NEG_INF), vs1t_ref[0, g, qt])

        def sel_body(c, carry):
            kk = kse_ref[0, g, pl.ds(pl.multiple_of(c * KC, KC), KC), :]
            online(_dot_nt(kk, lhs), vs1t_ref[0, g, c])
            return carry

        lax.fori_loop(0, qt, sel_body, 0)
        o_s = finish()
        kd = kw0_ref[0, g, pl.ds(pl.multiple_of(t0, KC), KC), :]
        first(jnp.where(causal, _dot_nt(kd, q_all), NEG_INF), vw1t_ref[0, g, qt])

        @pl.when(qt >= 1)
        def _():
            kk = kw0_ref[0, g, pl.ds(pl.multiple_of(t0 - KC, KC), KC), :]
            online(_dot_nt(kk, q_all), vw1t_ref[0, g, qt - 1])

        @pl.when(qt >= 2)
        def _():
            kk = kw0_ref[0, g, pl.ds(pl.multiple_of(t0 - 2 * KC, KC), KC), :]
            online(jnp.where(band, _dot_nt(kk, q_all), NEG_INF), vw1t_ref[0, g, qt - 2])

        o_w = finish()
        outs = []
        for h in range(HPG):
            r = 3 * (g * HPG + h)
            c0, c1 = h * TQ, (h + 1) * TQ
            outs.append(gt[r:r + 1, :] * o_c[:, c0:c1] + gt[r + 1:r + 2, :] * o_s[:, c0:c1]
                        + gt[r + 2:r + 3, :] * o_w[:, c0:c1])
        o_ref[0, :, g * HPG * HEAD_DIM:(g + 1) * HPG * HEAD_DIM] = jnp.concatenate(outs, axis=0).T


def _nsa(q, kse, vs1t, kw0, vw1t, kc0, vct, gates_t, ovt):
    b, s, _ = q.shape
    nc = kc0.shape[2]
    seq = lambda i, j: (i, 0, 0, 0)
    seq5 = lambda i, j: (i, 0, 0, 0, 0)
    tile = lambda i, j: (i, j, 0)
    fixed = lambda i, j: (0, 0)
    return pl.pallas_call(
        _nsa_kernel,
        grid=(b, s // TQ),
        in_specs=[pl.BlockSpec((1, TQ, N_HEADS * LANES), tile),
                  pl.BlockSpec((1, N_KV, s, LANES), seq), pl.BlockSpec((1, N_KV, s // KC, LANES, KC), seq5),
                  pl.BlockSpec((1, N_KV, s, LANES), seq), pl.BlockSpec((1, N_KV, s // KC, LANES, KC), seq5),
                  pl.BlockSpec((1, N_KV, nc, LANES), seq), pl.BlockSpec((1, N_KV, HEAD_DIM, nc), seq),
                  pl.BlockSpec((1, 4 * SUBLANES, TQ), lambda i, j: (i, 0, j)),
                  pl.BlockSpec(ovt.shape, fixed)],
        out_specs=pl.BlockSpec((1, TQ, ATTN_W), tile),
        out_shape=jax.ShapeDtypeStruct((b, s, ATTN_W), F32),
        scratch_shapes=[pltpu.VMEM((SUBLANES, HPG * TQ), F32), pltpu.VMEM((LANES, HPG * TQ), F32)],
        compiler_params=_cparams(("parallel", "arbitrary")),
        name="nsa",
    )(q, kse, vs1t, kw0, vw1t, kc0, vct, gates_t, ovt)


def _post_kernel(x_ref, conv_ref, halo_ref, attn_ref, convw_ref, gnc_ref, gna_ref, wout_ref, ln2_ref, wq_ref,
                 h1_ref, xh_ref, xl_ref, pq_ref, *, tiles_per_seq):
    i = pl.program_id(0)
    c_h = conv_ref[:, 0:CONV_CH]
    c_b = conv_ref[:, CONV_CH:2 * CONV_CH]
    c_c = conv_ref[:, 2 * CONV_CH:3 * CONV_CH]
    z = c_c * c_h
    keep = jnp.where(i % tiles_per_seq == 0, 0.0, 1.0)
    zp = halo_ref[:, 2 * CONV_CH:3 * CONV_CH] * halo_ref[:, 0:CONV_CH] * keep
    rows = lax.broadcasted_iota(jnp.int32, z.shape, 0)
    n = z.shape[0]
    z1 = jnp.where(rows == 0, zp[SUBLANES - 1:SUBLANES, :], pltpu.roll(z, 1, 0))
    z2 = jnp.where(rows == 0, zp[SUBLANES - 2:SUBLANES - 1, :],
                   jnp.where(rows == 1, zp[SUBLANES - 1:SUBLANES, :], pltpu.roll(z, 2, 0)))
    conv = convw_ref[0:1, :] * z2 + convw_ref[1:2, :] * z1 + convw_ref[2:3, :] * z
    nc = _rms(c_b * conv, gnc_ref[...]).astype(BF16)
    na = _rms(attn_ref[...], gna_ref[...]).astype(BF16)
    h1 = x_ref[...] + _dot(nc, wout_ref[0:CONV_CH, :]) + _dot(na, wout_ref[CONV_CH:CONV_CH + ATTN_W, :])
    h1_ref[...] = h1
    hn = _rms(h1, ln2_ref[...])
    xh = hn.astype(BF16)
    xh_ref[...] = xh
    xl_ref[...] = (hn - xh.astype(F32)).astype(BF16)
    pq_ref[...] = _dot(xh, wq_ref[...]).astype(BF16)


def _post(x2, conv, attn2, convw, gnc, gna, wout, ln2, wq, seq_len):
    t = x2.shape[0]
    row = lambda i: (i, 0)
    fixed = lambda i: (0, 0)
    halo = lambda i: (jnp.maximum(i * (T_PROJ // SUBLANES) - 1, 0), 0)
    nq = wq.shape[1]
    return pl.pallas_call(
        functools.partial(_post_kernel, tiles_per_seq=seq_len // T_PROJ),
        grid=(t // T_PROJ,),
        in_specs=[pl.BlockSpec((T_PROJ, D_MODEL), row),
                  pl.BlockSpec((T_PROJ, 3 * CONV_CH), row),
                  pl.BlockSpec((SUBLANES, 3 * CONV_CH), halo),
                  pl.BlockSpec((T_PROJ, ATTN_W), row),
                  pl.BlockSpec(convw.shape, fixed), pl.BlockSpec(gnc.shape, fixed), pl.BlockSpec(gna.shape, fixed),
                  pl.BlockSpec(wout.shape, fixed), pl.BlockSpec(ln2.shape, fixed), pl.BlockSpec(wq.shape, fixed)],
        out_specs=[pl.BlockSpec((T_PROJ, D_MODEL), row), pl.BlockSpec((T_PROJ, D_MODEL), row),
                   pl.BlockSpec((T_PROJ, D_MODEL), row), pl.BlockSpec((T_PROJ, nq), row)],
        out_shape=[jax.ShapeDtypeStruct((t, D_MODEL), F32), jax.ShapeDtypeStruct((t, D_MODEL), BF16),
                   jax.ShapeDtypeStruct((t, D_MODEL), BF16), jax.ShapeDtypeStruct((t, nq), BF16)],
        compiler_params=_cparams(("parallel",)),
        name="post",
    )(x2, conv, conv, attn2, convw, gnc, gna, wout, ln2, wq)


def _staircase():
    return [(a, b) for a in range(PEER_TOPK) for b in range(PEER_TOPK) if (a + 1) * (b + 1) <= PEER_TOPK]


N_CAND = 56


def _topk_kernel(pq_ref, sk_ref, e_ref, g_ref, sv_ref, si_ref, cand_ref, ce_ref, et_ref, gt_ref):
    tt = pq_ref.shape[0]
    rown = lax.broadcasted_iota(jnp.int32, (PEER_NKEYS, tt), 0)
    rowc = lax.broadcasted_iota(jnp.int32, (N_CAND, tt), 0)
    pairs = _staircase()
    for h in range(PEER_HEADS):
        for c in range(2):
            off = (h * 2 + c) * PEER_NKEYS
            x = _dot_nt(sk_ref[h, c], pq_ref[:, off:off + PEER_NKEYS])
            for it in range(PEER_TOPK):
                m = jnp.max(x, axis=0, keepdims=True)
                idx = jnp.min(jnp.where(x == m, rown, PEER_NKEYS), axis=0, keepdims=True)
                sv_ref[c, it:it + 1, :] = m
                si_ref[c, it:it + 1, :] = idx
                x = jnp.where(rown == idx, -jnp.inf, x)
        cand_ref[...] = jnp.full((N_CAND, tt), -jnp.inf, F32)
        ce_ref[...] = jnp.zeros((N_CAND, tt), jnp.int32)
        for r, (a, b) in enumerate(pairs):
            cand_ref[r:r + 1, :] = sv_ref[0, a:a + 1, :] + sv_ref[1, b:b + 1, :]
            ce_ref[r:r + 1, :] = si_ref[0, a:a + 1, :] * PEER_NKEYS + si_ref[1, b:b + 1, :]
        x = cand_ref[...]
        ce = ce_ref[...]
        best = []
        for it in range(PEER_TOPK):
            m = jnp.max(x, axis=0, keepdims=True)
            idx = jnp.min(jnp.where(x == m, rowc, N_CAND), axis=0, keepdims=True)
            hit = rowc == idx
            et_ref[h * PEER_TOPK + it:h * PEER_TOPK + it + 1, :] = jnp.max(jnp.where(hit, ce, -1), axis=0, keepdims=True)
            best.append(m)
            x = jnp.where(hit, -jnp.inf, x)
        ex = [jnp.exp(v - best[0]) for v in best]
        tot = ex[0]
        for v in ex[1:]:
            tot = tot + v
        inv = 1.0 / tot
        for it in range(PEER_TOPK):
            gt_ref[h * PEER_TOPK + it:h * PEER_TOPK + it + 1, :] = ex[it] * inv
    e_ref[...] = (et_ref[...] * ROWS_PER_EXPERT).T
    g_ref[...] = gt_ref[...].T


def _topk(pq, sk):
    t, nq = pq.shape
    nk = PEER_HEADS * PEER_TOPK
    row = lambda i: (i, 0)
    return pl.pallas_call(
        _topk_kernel,
        grid=(t // T_TOPK,),
        in_specs=[pl.BlockSpec((T_TOPK, nq), row), pl.BlockSpec(sk.shape, lambda i: (0, 0, 0, 0))],
        out_specs=[pl.BlockSpec((T_TOPK, nk), row), pl.BlockSpec((T_TOPK, nk), row)],
        out_shape=[jax.ShapeDtypeStruct((t, nk), jnp.int32), jax.ShapeDtypeStruct((t, nk), F32)],
        scratch_shapes=[pltpu.VMEM((2, PEER_TOPK, T_TOPK), F32), pltpu.VMEM((2, PEER_TOPK, T_TOPK), jnp.int32),
                        pltpu.VMEM((N_CAND, T_TOPK), F32), pltpu.VMEM((N_CAND, T_TOPK), jnp.int32),
                        pltpu.VMEM((nk, T_TOPK), jnp.int32), pltpu.VMEM((nk, T_TOPK), F32)],
        compiler_params=_cparams(("parallel",)),
        name="topk",
    )(pq, sk)


def _gather_rows(e_ref, tab_ref, t):
    rows = []
    for k in range(e_ref.shape[1]):
        e0 = pl.multiple_of(e_ref[t, k], ROWS_PER_EXPERT)
        rows.append(tab_ref[pl.ds(e0, ROWS_PER_EXPERT), :])
    return pltpu.bitcast(jnp.concatenate(rows, axis=0), BF16)


def _for_each_token(e_hbm, e_refs, sem, tab_ref, compute):
    i = pl.program_id(0)
    half = e_refs[0].shape[0]

    def copy(step, part):
        rows = pl.ds(pl.multiple_of((2 * step + part) * half, half), half)
        return pltpu.make_async_copy(e_hbm.at[rows], e_refs[part], sem.at[part])

    @pl.when(i == 0)
    def _():
        copy(0, 0).start()

    copy(i, 1).start()
    copy(i, 0).wait()
    w_next = _gather_rows(e_refs[0], tab_ref, 0)
    for t in range(2 * half):
        w_cur = w_next
        if t + 1 == half:
            @pl.when(i + 1 < pl.num_programs(0))
            def _():
                copy(i + 1, 0).start()

            copy(i, 1).wait()
        if t + 1 < 2 * half:
            w_next = _gather_rows(e_refs[(t + 1) // half], tab_ref, (t + 1) % half)
        compute(t, w_cur)


def _diag16(n):
    lane = lax.broadcasted_iota(jnp.int32, (2 * SUBLANES, n), 1)
    row = lax.broadcasted_iota(jnp.int32, (2 * SUBLANES, n), 0)
    return (lane & (SUBLANES - 1)) == (row & (SUBLANES - 1)), row < SUBLANES


def _peer_u_kernel(e_hbm, tab_ref, x_ref, gate_ref, g8_ref, a_ref, e0_ref, e1_ref, sem, hs_ref):
    diag, _ = _diag16(SUBLANES * e0_ref.shape[1])

    def token(t, w):
        x16 = pltpu.bitcast(x_ref[t * SUBLANES:(t + 1) * SUBLANES, :], BF16)
        r = _dot_nt(x16, w)
        hs_ref[t:t + 1, :] = jnp.sum(jnp.where(diag, r, 0.0), axis=0, keepdims=True)

    _for_each_token(e_hbm, (e0_ref, e1_ref), sem, tab_ref, token)
    h = jnp.dot(hs_ref[...], g8_ref[...], preferred_element_type=F32, precision=lax.Precision.HIGHEST)
    a_ref[...] = gate_ref[...] * _gelu(h)


def _peer_u(eidx, tab, xw, gate, g8):
    t, nk = eidx.shape
    row = lambda i: (i, 0)
    fixed = lambda i: (0, 0)
    return pl.pallas_call(
        _peer_u_kernel,
        grid=(t // T_PEER,),
        in_specs=[pl.BlockSpec(memory_space=pl.ANY),
                  pl.BlockSpec(tab.shape, fixed, pipeline_mode=pl.Buffered(1)),
                  pl.BlockSpec((T_PEER * SUBLANES, LANES), row),
                  pl.BlockSpec((T_PEER, nk), row),
                  pl.BlockSpec(g8.shape, fixed)],
        out_specs=pl.BlockSpec((T_PEER, nk), row),
        out_shape=jax.ShapeDtypeStruct((t, nk), F32),
        scratch_shapes=[pltpu.SMEM((T_PEER // 2, nk), jnp.int32), pltpu.SMEM((T_PEER // 2, nk), jnp.int32),
                        pltpu.SemaphoreType.DMA((2,)),
                        pltpu.VMEM((T_PEER, SUBLANES * nk), F32)],
        compiler_params=_cparams(("arbitrary",)),
        name="peer_u",
    )(eidx, tab, xw, gate, g8)


def _peer_v_kernel(e_hbm, tab_ref, a_ref, rep_ref, o_ref, e0_ref, e1_ref, sem, arep_ref):
    n = SUBLANES * e0_ref.shape[1]
    diag, top = _diag16(n)
    arep_ref[...] = jnp.dot(a_ref[...], rep_ref[...], preferred_element_type=F32, precision=lax.Precision.HIGHEST)

    def token(t, w):
        l32 = jnp.where(diag, jnp.broadcast_to(arep_ref[t:t + 1, :], (2 * SUBLANES, n)), 0.0)
        hi = l32.astype(BF16).astype(F32)
        lhs = jnp.where(top, hi, l32 - hi).astype(BF16)
        out = _dot(lhs, w)
        o_ref[t * SUBLANES:(t + 1) * SUBLANES, :] = out[0:SUBLANES] + out[SUBLANES:2 * SUBLANES]

    _for_each_token(e_hbm, (e0_ref, e1_ref), sem, tab_ref, token)


def _peer_v(eidx, tab, a, rep):
    t, nk = eidx.shape
    row = lambda i: (i, 0)
    fixed = lambda i: (0, 0)
    return pl.pallas_call(
        _peer_v_kernel,
        grid=(t // T_PEER,),
        in_specs=[pl.BlockSpec(memory_space=pl.ANY),
                  pl.BlockSpec(tab.shape, fixed, pipeline_mode=pl.Buffered(1)),
                  pl.BlockSpec((T_PEER, nk), row),
                  pl.BlockSpec(rep.shape, fixed)],
        out_specs=pl.BlockSpec((T_PEER * SUBLANES, LANES), row),
        out_shape=jax.ShapeDtypeStruct((t * SUBLANES, LANES), F32),
        scratch_shapes=[pltpu.SMEM((T_PEER // 2, nk), jnp.int32), pltpu.SMEM((T_PEER // 2, nk), jnp.int32),
                        pltpu.SemaphoreType.DMA((2,)),
                        pltpu.VMEM((T_PEER, SUBLANES * nk), F32)],
        compiler_params=_cparams(("arbitrary",)),
        name="peer_v",
    )(eidx, tab, a, rep)


def _final_kernel(h1_ref, p_ref, g_ref, o_ref):
    o_ref[...] = _rms(h1_ref[...] + p_ref[...], g_ref[...])


def _final(h1, peer, g):
    t = h1.shape[0]
    row = lambda i: (i, 0)
    return pl.pallas_call(
        _final_kernel,
        grid=(t // T_PROJ,),
        in_specs=[pl.BlockSpec((T_PROJ, D_MODEL), row), pl.BlockSpec((T_PROJ, D_MODEL), row),
                  pl.BlockSpec((1, D_MODEL), lambda i: (0, 0))],
        out_specs=pl.BlockSpec((T_PROJ, D_MODEL), row),
        out_shape=jax.ShapeDtypeStruct((t, D_MODEL), F32),
        compiler_params=_cparams(("parallel",)),
        name="final",
    )(h1, peer, g)


def _pack_rows(a):
    r, n, _ = a.shape
    bits = lax.bitcast_convert_type(a, jnp.uint16).astype(jnp.uint32).reshape(r, n // 2, 2, LANES)
    return (bits[:, :, 0] | (bits[:, :, 1] << 16)).reshape(r * n // 2, LANES)


def _pad_lanes(a, left):
    z = jnp.zeros_like(a)
    return jnp.concatenate([a, z] if left else [z, a], axis=-1)


def _layer(h, l, ln1, w_in, conv_w, cmp_pos_k, cmp_pos_v, cmp_k_w1, cmp_k_w2, cmp_v_w1, cmp_v_w2,
           gn_conv, gn_attn, w_out, ln2, peer_wq, peer_subkeys, peer_u, peer_v):
    b, s, _ = h.shape
    t = b * s
    x2 = h.reshape(t, D_MODEL)
    w = w_in[l]
    o_q = 3 * CONV_CH
    o_kv = o_q + ATTN_W
    o_g = o_kv + 6 * N_KV * HEAD_DIM
    wq_heads = w[:, o_q:o_kv].reshape(D_MODEL, N_HEADS, HEAD_DIM)
    wq_pad = _pad_lanes(wq_heads, True).reshape(D_MODEL, N_HEADS * LANES)
    wg_pad = jnp.pad(w[:, o_g:], ((0, 0), (0, LANES - N_GATES)))
    w_cat = jnp.concatenate([w[:, :o_q], wq_pad, w[:, o_kv:o_g], wg_pad], axis=1).astype(BF16)
    conv, q, kcvc, kv4, gates = _inproj(x2, ln1[l][None, :], w_cat)

    nc = s // CMP_STRIDE

    def chunks(a):
        a = a.reshape(b, nc, CMP_STRIDE, N_KV, HEAD_DIM).transpose(0, 3, 1, 2, 4)
        return a.reshape(b, N_KV, nc, CMP_STRIDE * HEAD_DIM)

    pos2 = lambda p: p.reshape(2, CMP_STRIDE * HEAD_DIM)
    kcc, vcc = _compress(chunks(kcvc[:, :LANES]), chunks(kcvc[:, LANES:]), pos2(cmp_pos_k[l]), pos2(cmp_pos_v[l]),
                         cmp_k_w1[l].astype(BF16), cmp_k_w2[l].astype(BF16),
                         cmp_v_w1[l].astype(BF16), cmp_v_w2[l].astype(BF16))

    attn = _attention(b, s, q, kv4, kcc, vcc, gates)

    h1, xh, xl, pq = _post(x2, conv, attn.reshape(t, ATTN_W), conv_w[l], gn_conv[l][None, :], gn_attn[l][None, :],
                           w_out[l].astype(BF16), ln2[l][None, :], peer_wq[l].astype(BF16), s)

    eidx, gate = _topk(pq, peer_subkeys[l].astype(BF16))
    nk = PEER_HEADS * PEER_TOPK
    g8 = (jnp.arange(SUBLANES * nk)[:, None] // SUBLANES == jnp.arange(nk)[None, :]).astype(F32)
    table = lambda w: _pack_rows(w.astype(BF16).reshape(PEER_EXPERTS, SUBLANES, LANES))
    xw = _pack_rows(jnp.concatenate([xh.reshape(t, SUBLANES, LANES), xl.reshape(t, SUBLANES, LANES)], axis=1))
    a = _peer_u(eidx, table(peer_u[l]), xw, gate, g8)
    peer = _peer_v(eidx, table(peer_v[l]), a, g8.T).reshape(t, D_MODEL)
    return h1, peer


def _attention(b, s, q, kv4, kcc, vcc, gates):
    nc = s // CMP_STRIDE

    def group(a):
        return a.reshape(b, s, N_KV, HEAD_DIM).transpose(0, 2, 1, 3)

    ks, vs, kw, vw = (group(kv4[:, i * LANES:(i + 1) * LANES]) for i in range(4))
    n_sel = s // SEL_BLOCK
    blk_of_key = jnp.arange(s) // SEL_BLOCK
    onehot = (blk_of_key[:, None] == jnp.arange(n_sel)[None, :]).astype(BF16)
    onehot = jnp.pad(onehot, ((0, 0), (0, HEAD_DIM - n_sel))) if n_sel < HEAD_DIM else onehot
    ones = jnp.ones((b, N_KV, s, HEAD_DIM), BF16)
    kse = jnp.concatenate([ks, jnp.broadcast_to(onehot, (b, N_KV, s, HEAD_DIM))], axis=-1)
    kw0 = _pad_lanes(kw, True)
    kc0 = _pad_lanes(kcc, True)

    def values_t(v):
        v1 = jnp.concatenate([v, ones], axis=-1).reshape(b, N_KV, s // KC, KC, LANES)
        return v1.transpose(0, 1, 2, 4, 3)

    vs1t, vw1t = values_t(vs), values_t(vw)
    vct = vcc.transpose(0, 1, 3, 2)
    gates_t = gates.reshape(b, s, LANES)[:, :, :4 * SUBLANES].transpose(0, 2, 1)
    n_cmp = (s - CMP_BLOCK) // CMP_STRIDE + 1
    cs = np.arange(nc) * CMP_STRIDE
    ss = np.arange(HEAD_DIM) * SEL_BLOCK
    ov = ((cs[:, None] < ss[None, :] + SEL_BLOCK) & (cs[:, None] + CMP_BLOCK > ss[None, :])
          & (np.arange(nc)[:, None] < n_cmp) & (np.arange(HEAD_DIM)[None, :] < n_sel))
    ovt = jnp.asarray(ov.T.astype(np.float32))
    return _nsa(q.reshape(b, s, N_HEADS * LANES), kse, vs1t, kw0, vw1t, kc0, vct, gates_t, ovt)


def kernel(x, ln1, w_in, conv_w, cmp_pos_k, cmp_pos_v, cmp_k_w1, cmp_k_w2, cmp_v_w1, cmp_v_w2, gn_conv, gn_attn,
           w_out, ln2, peer_wq, peer_subkeys, peer_u, peer_v, ln_f):
    b, s, _ = x.shape
    depth = w_in.shape[0]
    h = x
    for l in range(depth):
        h1, peer = _layer(h, l, ln1, w_in, conv_w, cmp_pos_k, cmp_pos_v, cmp_k_w1, cmp_k_w2, cmp_v_w1, cmp_v_w2,
                          gn_conv, gn_attn, w_out, ln2, peer_wq, peer_subkeys, peer_u, peer_v)
        if l + 1 < depth:
            h = (h1 + peer).reshape(b, s, D_MODEL)
    return _final(h1, peer, ln_f[None, :]).reshape(b, s, D_MODEL)
```

```python
import functools
import math

import jax
import jax.numpy as jnp
import numpy as np
from jax import lax
from jax.experimental import pallas as pl
from jax.experimental.pallas import tpu as pltpu

F32 = jnp.float32
BF16 = jnp.bfloat16

D_MODEL = 1024
CONV_CH = 512
CONV_K = 3
N_HEADS = 8
HEAD_DIM = 64
N_KV = 2
HPG = N_HEADS // N_KV
ATTN_W = N_HEADS * HEAD_DIM
CMP_BLOCK = 32
CMP_STRIDE = 16
CMP_HIDDEN = 256
SEL_BLOCK = 64
SEL_TOP = 16
WINDOW = 512
N_GATES = 3 * N_HEADS
PEER_HEADS = 8
PEER_NKEYS = 128
PEER_EXPERTS = PEER_NKEYS * PEER_NKEYS
PEER_DKEY = 256
PEER_TOPK = 16
EPS = 1e-6
NEG_INF = -1e30
FORCE = 1e4

LANES = 128
SUBLANES = 8
VMEM_LIMIT = 56 * 1024 * 1024

TQ = 256
KC = 256
T_PROJ = 256
T_TOPK = 256
T_PEER = 64
assert T_PROJ == KC == TQ
HALF = D_MODEL // 2
ROWS_PER_EXPERT = HALF // LANES


def _cparams(sem):
    return pltpu.CompilerParams(dimension_semantics=sem, vmem_limit_bytes=VMEM_LIMIT)


def _dot_nt(a, b, precision=None):
    return lax.dot_general(a, b, (((1,), (1,)), ((), ())), preferred_element_type=F32, precision=precision)


def _dot(a, b):
    return jnp.dot(a, b, preferred_element_type=F32)


def _rms(x, g):
    return x * lax.rsqrt(jnp.mean(x * x, axis=-1, keepdims=True) + EPS) * g


def _gelu(x):
    c = math.sqrt(2.0 / math.pi)
    return 0.5 * x * (1.0 + jnp.tanh(c * (x + 0.044715 * (x * x * x))))


def _inproj_kernel(x_ref, ln1_ref, w_ref, wt_ref, conv_ref, q_ref, kcvc_ref, kse_ref, kw0_ref, vt_ref, gt_ref, *,
                   tiles_per_seq):
    xn = _rms(x_ref[...], ln1_ref[...]).astype(BF16)
    n_conv, n_q, n_k = 3 * CONV_CH, N_HEADS * LANES, N_KV * LANES
    o = 0
    conv_ref[...] = _dot(xn, w_ref[:, o:o + n_conv])
    o += n_conv
    q_ref[...] = (_dot(xn, w_ref[:, o:o + n_q]) * (HEAD_DIM ** -0.5)).astype(BF16)
    o += n_q
    kcvc_ref[...] = _dot(xn, w_ref[:, o:o + n_k])
    o += n_k
    pos = (pl.program_id(0) % tiles_per_seq) * T_PROJ + lax.broadcasted_iota(jnp.int32, (T_PROJ, n_k), 0)
    lane = lax.broadcasted_iota(jnp.int32, (T_PROJ, n_k), 1) & (LANES - 1)
    onehot = (lane - HEAD_DIM) == jnp.right_shift(pos, int(math.log2(SEL_BLOCK)))
    kse_ref[...] = jnp.where(onehot, 1.0, _dot(xn, w_ref[:, o:o + n_k])).astype(BF16)
    o += n_k
    kw0_ref[...] = _dot(xn, w_ref[:, o:o + n_k]).astype(BF16)
    ones_rows = lax.broadcasted_iota(jnp.int32, (LANES, T_PROJ), 0) >= HEAD_DIM
    for j in range(2 * N_KV):
        vt = _dot_nt(wt_ref[j * LANES:(j + 1) * LANES, :], xn)
        vt_ref[0, j] = jnp.where(ones_rows, 1.0, vt).astype(BF16)
    r0 = 2 * N_KV * LANES
    gt_ref[0] = jax.nn.sigmoid(_dot_nt(wt_ref[r0:r0 + 4 * SUBLANES, :], xn))


def _inproj(x2, ln1, w_cat, w_t, seq_len):
    t = x2.shape[0]
    row = lambda i: (i, 0)
    fixed = lambda i: (0, 0)
    n_k = N_KV * LANES
    return pl.pallas_call(
        functools.partial(_inproj_kernel, tiles_per_seq=seq_len // T_PROJ),
        grid=(t // T_PROJ,),
        in_specs=[pl.BlockSpec((T_PROJ, D_MODEL), row),
                  pl.BlockSpec((1, D_MODEL), fixed),
                  pl.BlockSpec(w_cat.shape, fixed),
                  pl.BlockSpec(w_t.shape, fixed)],
        out_specs=[pl.BlockSpec((T_PROJ, 3 * CONV_CH), row),
                   pl.BlockSpec((T_PROJ, N_HEADS * LANES), row),
                   pl.BlockSpec((T_PROJ, n_k), row),
                   pl.BlockSpec((T_PROJ, n_k), row),
                   pl.BlockSpec((T_PROJ, n_k), row),
                   pl.BlockSpec((1, 2 * N_KV, LANES, T_PROJ), lambda i: (i, 0, 0, 0)),
                   pl.BlockSpec((1, 4 * SUBLANES, T_PROJ), lambda i: (i, 0, 0))],
        out_shape=[jax.ShapeDtypeStruct((t, 3 * CONV_CH), F32),
                   jax.ShapeDtypeStruct((t, N_HEADS * LANES), BF16),
                   jax.ShapeDtypeStruct((t, n_k), F32),
                   jax.ShapeDtypeStruct((t, n_k), BF16),
                   jax.ShapeDtypeStruct((t, n_k), BF16),
                   jax.ShapeDtypeStruct((t // T_PROJ, 2 * N_KV, LANES, T_PROJ), BF16),
                   jax.ShapeDtypeStruct((t // T_PROJ, 4 * SUBLANES, T_PROJ), F32)],
        compiler_params=_cparams(("parallel",)),
        name="inproj",
    )(x2, ln1, w_cat, w_t)


def _compress_kernel(ck_ref, cv_ref, posk_ref, posv_ref, w1k_ref, w2k_ref, w1v_ref, w2v_ref, ok_ref, ov_ref):
    half = CMP_STRIDE * HEAD_DIM
    for c_ref, pos_ref, w1_ref, w2_ref, o_ref in ((ck_ref, posk_ref, w1k_ref, w2k_ref, ok_ref),
                                                  (cv_ref, posv_ref, w1v_ref, w2v_ref, ov_ref)):
        for g in range(N_KV):
            c = c_ref[0, g]
            nc = c.shape[0]
            a = _dot((c + pos_ref[0:1, :]).astype(BF16), w1_ref[0:half, :])
            b = _dot((c + pos_ref[1:2, :]).astype(BF16), w1_ref[half:2 * half, :])
            hid = a + pltpu.roll(b, nc - 1, 0)
            out = _dot(_gelu(hid).astype(BF16), w2_ref[...])
            rows = lax.broadcasted_iota(jnp.int32, out.shape, 0)
            o_ref[0, g] = jnp.where(rows < nc - 1, out, 0.0).astype(BF16)


def _compress(ck, cv, posk, posv, w1k, w2k, w1v, w2v):
    b, g, nc, cw = ck.shape
    blk = lambda i: (i, 0, 0, 0)
    fixed = lambda i: (0, 0)
    return pl.pallas_call(
        _compress_kernel,
        grid=(b,),
        in_specs=[pl.BlockSpec((1, g, nc, cw), blk), pl.BlockSpec((1, g, nc, cw), blk),
                  pl.BlockSpec(posk.shape, fixed), pl.BlockSpec(posv.shape, fixed),
                  pl.BlockSpec(w1k.shape, fixed), pl.BlockSpec(w2k.shape, fixed),
                  pl.BlockSpec(w1v.shape, fixed), pl.BlockSpec(w2v.shape, fixed)],
        out_specs=[pl.BlockSpec((1, g, nc, HEAD_DIM), blk), pl.BlockSpec((1, g, nc, HEAD_DIM), blk)],
        out_shape=[jax.ShapeDtypeStruct((b, g, nc, HEAD_DIM), BF16)] * 2,
        compiler_params=_cparams(("parallel",)),
        name="compress",
    )(ck, cv, posk, posv, w1k, w2k, w1v, w2v)


def _nsa_kernel(q_ref, kse_ref, kw0_ref, vt_ref, kc0_ref, vct_ref, gt_ref, ovt_ref, o_ref, m_ref, acc_ref):
    qt = pl.program_id(1)
    t0 = qt * TQ
    nc = kc0_ref.shape[2]
    n_sel = ovt_ref.shape[0]
    w = HPG * TQ
    tq = t0 + (lax.broadcasted_iota(jnp.int32, (1, w), 1) & (TQ - 1))
    krow = lax.broadcasted_iota(jnp.int32, (KC, 1), 0)
    causal = (t0 + krow) <= tq
    band = (t0 - 2 * KC + krow) > (tq - WINDOW)
    eye = (lax.broadcasted_iota(jnp.int32, (TQ, TQ), 0)
           == lax.broadcasted_iota(jnp.int32, (TQ, TQ), 1)).astype(BF16)
    nrow = lax.broadcasted_iota(jnp.int32, (nc, 1), 0)
    valid_c = ((nrow * CMP_STRIDE + (CMP_BLOCK - 1)) <= tq) & (nrow < nc - 1)
    jrow = lax.broadcasted_iota(jnp.int32, (n_sel, TQ), 0)
    qblk = jnp.right_shift(t0 + lax.broadcasted_iota(jnp.int32, (n_sel, TQ), 1), int(math.log2(SEL_BLOCK)))
    forced = (jrow == 0) | (jrow == qblk) | (jrow == qblk - 1)
    lane128 = lax.broadcasted_iota(jnp.int32, (TQ, LANES), 1)
    gt = gt_ref[0, 0]

    def online(s, vt):
        m_old = m_ref[0:1, :]
        m_new = jnp.maximum(m_old, jnp.max(s, axis=0, keepdims=True))
        alpha = jnp.exp(m_old - m_new)
        p = jnp.exp(s - m_new)
        acc_ref[...] = alpha * acc_ref[...] + _dot(vt, p.astype(BF16))
        m_ref[...] = jnp.broadcast_to(m_new, (SUBLANES, w))

    def first(s, vt):
        m = jnp.max(s, axis=0, keepdims=True)
        acc_ref[...] = _dot(vt, jnp.exp(s - m).astype(BF16))
        m_ref[...] = jnp.broadcast_to(m, (SUBLANES, w))

    def finish():
        a = acc_ref[...]
        return a[0:HEAD_DIM, :] / a[HEAD_DIM:HEAD_DIM + 1, :]

    for g in range(N_KV):
        kc0 = kc0_ref[0, g]
        q_all = jnp.concatenate([q_ref[0, :, (g * HPG + h) * LANES:(g * HPG + h + 1) * LANES]
                                 for h in range(HPG)], axis=0)
        s = jnp.where(valid_c, _dot_nt(kc0, q_all), NEG_INF)
        m = jnp.max(s, axis=0, keepdims=True)
        p = jnp.where(valid_c, jnp.exp(s - m), 0.0)
        l = jnp.sum(p, axis=0, keepdims=True)
        pn = p * (1.0 / jnp.where(l > 0.0, l, 1.0))
        o_c = _dot(vct_ref[0, g], pn.astype(BF16))
        psum = pn[:, 0:TQ]
        for h in range(1, HPG):
            psum = psum + pn[:, h * TQ:(h + 1) * TQ]
        imp_t = jnp.dot(ovt_ref[...], psum, preferred_element_type=F32,
                        precision=lax.Precision.HIGHEST)
        val = jnp.where(jrow > qblk, -FORCE, imp_t + jnp.where(forced, FORCE, 0.0))
        rank = jnp.zeros((n_sel, TQ), jnp.int32)
        for k in range(n_sel):
            vk = val[k:k + 1, :]
            ahead = (vk > val) | ((vk == val) & (jrow > k))
            rank = rank + ahead.astype(jnp.int32)
        sel_t = (rank < SEL_TOP).astype(BF16)
        pad_t = jnp.concatenate([jnp.zeros((LANES - n_sel, TQ), BF16), sel_t], axis=0)
        sel_q = _dot_nt(eye, pad_t)
        bias = jnp.where((lane128 >= HEAD_DIM) & (sel_q < 0.5), NEG_INF, 0.0).astype(BF16)
        lhs = q_all + jnp.concatenate([bias] * HPG, axis=0)
        gl = slice(g * LANES, (g + 1) * LANES)
        kd = kse_ref[0, pl.ds(pl.multiple_of(t0, KC), KC), gl]
        first(jnp.where(causal, _dot_nt(kd, lhs), NEG_INF), vt_ref[0, qt, g])

        def sel_body(c, carry):
            kk = kse_ref[0, pl.ds(pl.multiple_of(c * KC, KC), KC), gl]
            online(_dot_nt(kk, lhs), vt_ref[0, c, g])
            return carry

        lax.fori_loop(0, qt, sel_body, 0)
        o_s = finish()
        kd = kw0_ref[0, pl.ds(pl.multiple_of(t0, KC), KC), gl]
        first(jnp.where(causal, _dot_nt(kd, q_all), NEG_INF), vt_ref[0, qt, N_KV + g])

        @pl.when(qt >= 1)
        def _():
            kk = kw0_ref[0, pl.ds(pl.multiple_of(t0 - KC, KC), KC), gl]
            online(_dot_nt(kk, q_all), vt_ref[0, qt - 1, N_KV + g])

        @pl.when(qt >= 2)
        def _():
            kk = kw0_ref[0, pl.ds(pl.multiple_of(t0 - 2 * KC, KC), KC), gl]
            online(jnp.where(band, _dot_nt(kk, q_all), NEG_INF), vt_ref[0, qt - 2, N_KV + g])

        o_w = finish()
        outs = []
        for h in range(HPG):
            r = 3 * (g * HPG + h)
            c0, c1 = h * TQ, (h + 1) * TQ
            outs.append(gt[r:r + 1, :] * o_c[:, c0:c1] + gt[r + 1:r + 2, :] * o_s[:, c0:c1]
                        + gt[r + 2:r + 3, :] * o_w[:, c0:c1])
        o_ref[0, :, g * HPG * HEAD_DIM:(g + 1) * HPG * HEAD_DIM] = jnp.concatenate(outs, axis=0).T


def _nsa(q, kse, kw0, vt, kc0, vct, gates_t, ovt):
    b, s, _ = q.shape
    nc = kc0.shape[2]
    seq3 = lambda i, j: (i, 0, 0)
    seq = lambda i, j: (i, 0, 0, 0)
    seq5 = lambda i, j: (i, 0, 0, 0, 0)
    tile = lambda i, j: (i, j, 0)
    fixed = lambda i, j: (0, 0)
    return pl.pallas_call(
        _nsa_kernel,
        grid=(b, s // TQ),
        in_specs=[pl.BlockSpec((1, TQ, N_HEADS * LANES), tile),
                  pl.BlockSpec((1, s, N_KV * LANES), seq3), pl.BlockSpec((1, s, N_KV * LANES), seq3),
                  pl.BlockSpec((1, s // KC, 2 * N_KV, LANES, KC), seq5),
                  pl.BlockSpec((1, N_KV, nc, LANES), seq), pl.BlockSpec((1, N_KV, HEAD_DIM, nc), seq),
                  pl.BlockSpec((1, 1, 4 * SUBLANES, TQ), lambda i, j: (i, j, 0, 0)),
                  pl.BlockSpec(ovt.shape, fixed)],
        out_specs=pl.BlockSpec((1, TQ, ATTN_W), tile),
        out_shape=jax.ShapeDtypeStruct((b, s, ATTN_W), F32),
        scratch_shapes=[pltpu.VMEM((SUBLANES, HPG * TQ), F32), pltpu.VMEM((LANES, HPG * TQ), F32)],
        compiler_params=_cparams(("parallel", "arbitrary")),
        name="nsa",
    )(q, kse, kw0, vt, kc0, vct, gates_t, ovt)


def _post_kernel(x_ref, conv_ref, halo_ref, attn_ref, convw_ref, gnc_ref, gna_ref, wout_ref, ln2_ref, wq_ref,
                 h1_ref, hn_ref, pq_ref, *, tiles_per_seq):
    i = pl.program_id(0)
    c_h = conv_ref[:, 0:CONV_CH]
    c_b = conv_ref[:, CONV_CH:2 * CONV_CH]
    c_c = conv_ref[:, 2 * CONV_CH:3 * CONV_CH]
    z = c_c * c_h
    keep = jnp.where(i % tiles_per_seq == 0, 0.0, 1.0)
    zp = halo_ref[:, 2 * CONV_CH:3 * CONV_CH] * halo_ref[:, 0:CONV_CH] * keep
    rows = lax.broadcasted_iota(jnp.int32, z.shape, 0)
    n = z.shape[0]
    z1 = jnp.where(rows == 0, zp[SUBLANES - 1:SUBLANES, :], pltpu.roll(z, 1, 0))
    z2 = jnp.where(rows == 0, zp[SUBLANES - 2:SUBLANES - 1, :],
                   jnp.where(rows == 1, zp[SUBLANES - 1:SUBLANES, :], pltpu.roll(z, 2, 0)))
    conv = convw_ref[0:1, :] * z2 + convw_ref[1:2, :] * z1 + convw_ref[2:3, :] * z
    nc = _rms(c_b * conv, gnc_ref[...]).astype(BF16)
    na = _rms(attn_ref[...], gna_ref[...]).astype(BF16)
    h1 = x_ref[...] + _dot(nc, wout_ref[0:CONV_CH, :]) + _dot(na, wout_ref[CONV_CH:CONV_CH + ATTN_W, :])
    h1_ref[...] = h1
    hn = _rms(h1, ln2_ref[...])
    hn_ref[...] = hn
    pq_ref[...] = _dot(hn.astype(BF16), wq_ref[...]).astype(BF16)


def _post(x2, conv, attn2, convw, gnc, gna, wout, ln2, wq, seq_len):
    t = x2.shape[0]
    row = lambda i: (i, 0)
    fixed = lambda i: (0, 0)
    halo = lambda i: (jnp.maximum(i * (T_PROJ // SUBLANES) - 1, 0), 0)
    nq = wq.shape[1]
    return pl.pallas_call(
        functools.partial(_post_kernel, tiles_per_seq=seq_len // T_PROJ),
        grid=(t // T_PROJ,),
        in_specs=[pl.BlockSpec((T_PROJ, D_MODEL), row),
                  pl.BlockSpec((T_PROJ, 3 * CONV_CH), row),
                  pl.BlockSpec((SUBLANES, 3 * CONV_CH), halo),
                  pl.BlockSpec((T_PROJ, ATTN_W), row),
                  pl.BlockSpec(convw.shape, fixed), pl.BlockSpec(gnc.shape, fixed), pl.BlockSpec(gna.shape, fixed),
                  pl.BlockSpec(wout.shape, fixed), pl.BlockSpec(ln2.shape, fixed), pl.BlockSpec(wq.shape, fixed)],
        out_specs=[pl.BlockSpec((T_PROJ, D_MODEL), row), pl.BlockSpec((T_PROJ, D_MODEL), row),
                   pl.BlockSpec((T_PROJ, nq), row)],
        out_shape=[jax.ShapeDtypeStruct((t, D_MODEL), F32), jax.ShapeDtypeStruct((t, D_MODEL), F32),
                   jax.ShapeDtypeStruct((t, nq), BF16)],
        compiler_params=_cparams(("parallel",)),
        name="post",
    )(x2, conv, conv, attn2, convw, gnc, gna, wout, ln2, wq)


def _staircase():
    return [(a, b) for a in range(PEER_TOPK) for b in range(PEER_TOPK) if (a + 1) * (b + 1) <= PEER_TOPK]


N_CAND = 56


def _topk_kernel(pq_ref, sk_ref, e_ref, g_ref, sv_ref, si_ref, cand_ref, ce_ref, et_ref, gt_ref):
    tt = pq_ref.shape[0]
    rown = lax.broadcasted_iota(jnp.int32, (PEER_NKEYS, tt), 0)
    rowc = lax.broadcasted_iota(jnp.int32, (N_CAND, tt), 0)
    pairs = _staircase()
    for h in range(PEER_HEADS):
        for c in range(2):
            off = (h * 2 + c) * PEER_NKEYS
            x = _dot_nt(sk_ref[h, c], pq_ref[:, off:off + PEER_NKEYS])
            for it in range(PEER_TOPK):
                m = jnp.max(x, axis=0, keepdims=True)
                idx = jnp.min(jnp.where(x == m, rown, PEER_NKEYS), axis=0, keepdims=True)
                sv_ref[c, it:it + 1, :] = m
                si_ref[c, it:it + 1, :] = idx
                x = jnp.where(rown == idx, -jnp.inf, x)
        cand_ref[...] = jnp.full((N_CAND, tt), -jnp.inf, F32)
        ce_ref[...] = jnp.zeros((N_CAND, tt), jnp.int32)
        for r, (a, b) in enumerate(pairs):
            cand_ref[r:r + 1, :] = sv_ref[0, a:a + 1, :] + sv_ref[1, b:b + 1, :]
            ce_ref[r:r + 1, :] = si_ref[0, a:a + 1, :] * PEER_NKEYS + si_ref[1, b:b + 1, :]
        x = cand_ref[...]
        ce = ce_ref[...]
        best = []
        for it in range(PEER_TOPK):
            m = jnp.max(x, axis=0, keepdims=True)
            idx = jnp.min(jnp.where(x == m, rowc, N_CAND), axis=0, keepdims=True)
            hit = rowc == idx
            et_ref[h * PEER_TOPK + it:h * PEER_TOPK + it + 1, :] = jnp.max(jnp.where(hit, ce, -1), axis=0, keepdims=True)
            best.append(m)
            x = jnp.where(hit, -jnp.inf, x)
        ex = [jnp.exp(v - best[0]) for v in best]
        tot = ex[0]
        for v in ex[1:]:
            tot = tot + v
        inv = 1.0 / tot
        for it in range(PEER_TOPK):
            gt_ref[h * PEER_TOPK + it:h * PEER_TOPK + it + 1, :] = ex[it] * inv
    e_ref[...] = (et_ref[...] * ROWS_PER_EXPERT).T
    g_ref[...] = gt_ref[...].T


def _topk(pq, sk):
    t, nq = pq.shape
    nk = PEER_HEADS * PEER_TOPK
    row = lambda i: (i, 0)
    return pl.pallas_call(
        _topk_kernel,
        grid=(t // T_TOPK,),
        in_specs=[pl.BlockSpec((T_TOPK, nq), row), pl.BlockSpec(sk.shape, lambda i: (0, 0, 0, 0))],
        out_specs=[pl.BlockSpec((T_TOPK, nk), row), pl.BlockSpec((T_TOPK, nk), row)],
        out_shape=[jax.ShapeDtypeStruct((t, nk), jnp.int32), jax.ShapeDtypeStruct((t, nk), F32)],
        scratch_shapes=[pltpu.VMEM((2, PEER_TOPK, T_TOPK), F32), pltpu.VMEM((2, PEER_TOPK, T_TOPK), jnp.int32),
                        pltpu.VMEM((N_CAND, T_TOPK), F32), pltpu.VMEM((N_CAND, T_TOPK), jnp.int32),
                        pltpu.VMEM((nk, T_TOPK), jnp.int32), pltpu.VMEM((nk, T_TOPK), F32)],
        compiler_params=_cparams(("parallel",)),
        name="topk",
    )(pq, sk)


def _gather_rows(e_ref, tab_ref, t):
    rows = []
    for k in range(e_ref.shape[1]):
        e0 = pl.multiple_of(e_ref[t, k], ROWS_PER_EXPERT)
        rows.append(tab_ref[pl.ds(e0, ROWS_PER_EXPERT), :])
    return pltpu.bitcast(jnp.concatenate(rows, axis=0), BF16)


def _for_each_token(e_hbm, e_refs, sem, tab_ref, compute):
    i = pl.program_id(0)
    half = e_refs[0].shape[0]

    def copy(step, part):
        rows = pl.ds(pl.multiple_of((2 * step + part) * half, half), half)
        return pltpu.make_async_copy(e_hbm.at[rows], e_refs[part], sem.at[part])

    @pl.when(i == 0)
    def _():
        copy(0, 0).start()

    copy(i, 1).start()
    copy(i, 0).wait()
    w_next = _gather_rows(e_refs[0], tab_ref, 0)
    for t in range(2 * half):
        w_cur = w_next
        if t + 1 == half:
            @pl.when(i + 1 < pl.num_programs(0))
            def _():
                copy(i + 1, 0).start()

            copy(i, 1).wait()
        if t + 1 < 2 * half:
            w_next = _gather_rows(e_refs[(t + 1) // half], tab_ref, (t + 1) % half)
        compute(t, w_cur)


def _diag16(n):
    lane = lax.broadcasted_iota(jnp.int32, (2 * SUBLANES, n), 1)
    row = lax.broadcasted_iota(jnp.int32, (2 * SUBLANES, n), 0)
    return (lane & (SUBLANES - 1)) == (row & (SUBLANES - 1)), row < SUBLANES


def _peer_u_kernel(e_hbm, tab_ref, x_ref, gate_ref, g8_ref, a_ref, e0_ref, e1_ref, sem, hs_ref):
    diag, _ = _diag16(SUBLANES * e0_ref.shape[1])
    top = lax.broadcasted_iota(jnp.int32, (2 * SUBLANES, LANES), 0) < SUBLANES

    def token(t, w):
        parts = [x_ref[t:t + 1, i * LANES:(i + 1) * LANES] for i in range(SUBLANES)]
        xx = jnp.concatenate(parts + parts, axis=0)
        hi = xx.astype(BF16).astype(F32)
        x16 = jnp.where(top, hi, xx - hi).astype(BF16)
        r = _dot_nt(x16, w)
        hs_ref[t:t + 1, :] = jnp.sum(jnp.where(diag, r, 0.0), axis=0, keepdims=True)

    _for_each_token(e_hbm, (e0_ref, e1_ref), sem, tab_ref, token)
    h = jnp.dot(hs_ref[...], g8_ref[...], preferred_element_type=F32, precision=lax.Precision.HIGHEST)
    a_ref[...] = gate_ref[...] * _gelu(h)


def _peer_u(eidx, tab, hn, gate, g8):
    t, nk = eidx.shape
    row = lambda i: (i, 0)
    fixed = lambda i: (0, 0)
    return pl.pallas_call(
        _peer_u_kernel,
        grid=(t // T_PEER,),
        in_specs=[pl.BlockSpec(memory_space=pl.ANY),
                  pl.BlockSpec(tab.shape, fixed, pipeline_mode=pl.Buffered(1)),
                  pl.BlockSpec((T_PEER, D_MODEL), row),
                  pl.BlockSpec((T_PEER, nk), row),
                  pl.BlockSpec(g8.shape, fixed)],
        out_specs=pl.BlockSpec((T_PEER, nk), row),
        out_shape=jax.ShapeDtypeStruct((t, nk), F32),
        scratch_shapes=[pltpu.SMEM((T_PEER // 2, nk), jnp.int32), pltpu.SMEM((T_PEER // 2, nk), jnp.int32),
                        pltpu.SemaphoreType.DMA((2,)),
                        pltpu.VMEM((T_PEER, SUBLANES * nk), F32)],
        compiler_params=_cparams(("arbitrary",)),
        name="peer_u",
    )(eidx, tab, hn, gate, g8)


def _peer_v_kernel(e_hbm, tab_ref, a_ref, rep_ref, h1_ref, lnf_ref, o_ref, e0_ref, e1_ref, sem, arep_ref, po_ref, *,
                   last_layer):
    n = SUBLANES * e0_ref.shape[1]
    diag, top = _diag16(n)
    arep_ref[...] = jnp.dot(a_ref[...], rep_ref[...], preferred_element_type=F32, precision=lax.Precision.HIGHEST)

    def token(t, w):
        l32 = jnp.where(diag, jnp.broadcast_to(arep_ref[t:t + 1, :], (2 * SUBLANES, n)), 0.0)
        hi = l32.astype(BF16).astype(F32)
        lhs = jnp.where(top, hi, l32 - hi).astype(BF16)
        out = _dot(lhs, w)
        out = out[0:SUBLANES] + out[SUBLANES:2 * SUBLANES]
        for i in range(SUBLANES):
            po_ref[t:t + 1, i * LANES:(i + 1) * LANES] = out[i:i + 1, :]

    _for_each_token(e_hbm, (e0_ref, e1_ref), sem, tab_ref, token)
    h = h1_ref[...] + po_ref[...]
    o_ref[...] = _rms(h, lnf_ref[...]) if last_layer else h


def _peer_v(eidx, tab, a, rep, h1, lnf, last_layer):
    t, nk = eidx.shape
    row = lambda i: (i, 0)
    fixed = lambda i: (0, 0)
    return pl.pallas_call(
        functools.partial(_peer_v_kernel, last_layer=last_layer),
        grid=(t // T_PEER,),
        in_specs=[pl.BlockSpec(memory_space=pl.ANY),
                  pl.BlockSpec(tab.shape, fixed, pipeline_mode=pl.Buffered(1)),
                  pl.BlockSpec((T_PEER, nk), row),
                  pl.BlockSpec(rep.shape, fixed),
                  pl.BlockSpec((T_PEER, D_MODEL), row),
                  pl.BlockSpec((1, D_MODEL), fixed)],
        out_specs=pl.BlockSpec((T_PEER, D_MODEL), row),
        out_shape=jax.ShapeDtypeStruct((t, D_MODEL), F32),
        scratch_shapes=[pltpu.SMEM((T_PEER // 2, nk), jnp.int32), pltpu.SMEM((T_PEER // 2, nk), jnp.int32),
                        pltpu.SemaphoreType.DMA((2,)),
                        pltpu.VMEM((T_PEER, SUBLANES * nk), F32),
                        pltpu.VMEM((T_PEER, D_MODEL), F32)],
        compiler_params=_cparams(("arbitrary",)),
        name="peer_v",
    )(eidx, tab, a, rep, h1, lnf)


def _pack_rows(a):
    r, n, _ = a.shape
    bits = lax.bitcast_convert_type(a, jnp.uint16).astype(jnp.uint32).reshape(r, n // 2, 2, LANES)
    return (bits[:, :, 0] | (bits[:, :, 1] << 16)).reshape(r * n // 2, LANES)


def _pad_lanes(a, left):
    z = jnp.zeros_like(a)
    return jnp.concatenate([a, z] if left else [z, a], axis=-1)


def _layer(h, l, ln1, w_in, conv_w, cmp_pos_k, cmp_pos_v, cmp_k_w1, cmp_k_w2, cmp_v_w1, cmp_v_w2,
           gn_conv, gn_attn, w_out, ln2, peer_wq, peer_subkeys, peer_u, peer_v, ln_f, last_layer):
    b, s, _ = h.shape
    t = b * s
    x2 = h.reshape(t, D_MODEL)
    w = w_in[l]
    o_q = 3 * CONV_CH
    o_kv = o_q + ATTN_W
    o_g = o_kv + 6 * N_KV * HEAD_DIM
    kvw = N_KV * HEAD_DIM
    part = lambda i: w[:, o_kv + i * kvw:o_kv + (i + 1) * kvw]
    pad_heads = lambda a, n: _pad_lanes(a.reshape(D_MODEL, n, HEAD_DIM), True).reshape(D_MODEL, n * LANES)
    w_cat = jnp.concatenate([w[:, :o_q], pad_heads(w[:, o_q:o_kv], N_HEADS), part(0), part(1),
                             pad_heads(part(2), N_KV), pad_heads(part(4), N_KV)], axis=1).astype(BF16)
    w_t = jnp.concatenate([pad_heads(part(3), N_KV).T, pad_heads(part(5), N_KV).T,
                           jnp.pad(w[:, o_g:], ((0, 0), (0, 4 * SUBLANES - N_GATES))).T], axis=0).astype(BF16)
    conv, q, kcvc, kse, kw0, vt, gates_t = _inproj(x2, ln1[l][None, :], w_cat, w_t, s)

    nc = s // CMP_STRIDE

    def chunks(a):
        a = a.reshape(b, nc, CMP_STRIDE, N_KV, HEAD_DIM).transpose(0, 3, 1, 2, 4)
        return a.reshape(b, N_KV, nc, CMP_STRIDE * HEAD_DIM)

    pos2 = lambda p: p.reshape(2, CMP_STRIDE * HEAD_DIM)
    kcc, vcc = _compress(chunks(kcvc[:, :LANES]), chunks(kcvc[:, LANES:]), pos2(cmp_pos_k[l]), pos2(cmp_pos_v[l]),
                         cmp_k_w1[l].astype(BF16), cmp_k_w2[l].astype(BF16),
                         cmp_v_w1[l].astype(BF16), cmp_v_w2[l].astype(BF16))

    attn = _attention(b, s, q, kse, kw0, vt, kcc, vcc, gates_t)

    h1, hn, pq = _post(x2, conv, attn.reshape(t, ATTN_W), conv_w[l], gn_conv[l][None, :], gn_attn[l][None, :],
                           w_out[l].astype(BF16), ln2[l][None, :], peer_wq[l].astype(BF16), s)

    eidx, gate = _topk(pq, peer_subkeys[l].astype(BF16))
    nk = PEER_HEADS * PEER_TOPK
    g8 = (jnp.arange(SUBLANES * nk)[:, None] // SUBLANES == jnp.arange(nk)[None, :]).astype(F32)
    table = lambda w: _pack_rows(w.astype(BF16).reshape(PEER_EXPERTS, SUBLANES, LANES))
    a = _peer_u(eidx, table(peer_u[l]), hn, gate, g8)
    return _peer_v(eidx, table(peer_v[l]), a, g8.T, h1, ln_f[None, :], last_layer).reshape(b, s, D_MODEL)


def _attention(b, s, q, kse, kw0, vt, kcc, vcc, gates_t):
    nc = s // CMP_STRIDE
    n_sel = s // SEL_BLOCK
    kc0 = _pad_lanes(kcc, True)
    vct = vcc.transpose(0, 1, 3, 2)
    kse, kw0 = kse.reshape(b, s, N_KV * LANES), kw0.reshape(b, s, N_KV * LANES)
    vt = vt.reshape(b, s // KC, 2 * N_KV, LANES, KC)
    gates_t = gates_t.reshape(b, s // TQ, 4 * SUBLANES, TQ)
    n_cmp = (s - CMP_BLOCK) // CMP_STRIDE + 1
    cs = np.arange(nc) * CMP_STRIDE
    ss = np.arange(HEAD_DIM) * SEL_BLOCK
    ov = ((cs[:, None] < ss[None, :] + SEL_BLOCK) & (cs[:, None] + CMP_BLOCK > ss[None, :])
          & (np.arange(nc)[:, None] < n_cmp) & (np.arange(HEAD_DIM)[None, :] < n_sel))
    ovt = jnp.asarray(ov.T.astype(np.float32))
    return _nsa(q.reshape(b, s, N_HEADS * LANES), kse, kw0, vt, kc0, vct, gates_t, ovt)


def kernel(x, ln1, w_in, conv_w, cmp_pos_k, cmp_pos_v, cmp_k_w1, cmp_k_w2, cmp_v_w1, cmp_v_w2, gn_conv, gn_attn,
           w_out, ln2, peer_wq, peer_subkeys, peer_u, peer_v, ln_f):
    b, s, _ = x.shape
    depth = w_in.shape[0]
    h = x
    for l in range(depth):
        h = _layer(h, l, ln1, w_in, conv_w, cmp_pos_k, cmp_pos_v, cmp_k_w1, cmp_k_w2, cmp_v_w1, cmp_v_w2,
                   gn_conv, gn_attn, w_out, ln2, peer_wq, peer_subkeys, peer_u, peer_v, ln_f, l + 1 == depth)
    return h
```

```python
import functools
import math

import jax
import jax.numpy as jnp
import numpy as np
from jax import lax
from jax.experimental import pallas as pl
from jax.experimental.pallas import tpu as pltpu

F32 = jnp.float32
BF16 = jnp.bfloat16

D_MODEL = 1024
CONV_CH = 512
CONV_K = 3
N_HEADS = 8
HEAD_DIM = 64
N_KV = 2
HPG = N_HEADS // N_KV
ATTN_W = N_HEADS * HEAD_DIM
CMP_BLOCK = 32
CMP_STRIDE = 16
CMP_HIDDEN = 256
SEL_BLOCK = 64
SEL_TOP = 16
WINDOW = 512
N_GATES = 3 * N_HEADS
PEER_HEADS = 8
PEER_NKEYS = 128
PEER_EXPERTS = PEER_NKEYS * PEER_NKEYS
PEER_DKEY = 256
PEER_TOPK = 16
EPS = 1e-6
NEG_INF = -1e30
FORCE = 1e4

LANES = 128
SUBLANES = 8
VMEM_LIMIT = 56 * 1024 * 1024

TQ = 256
KC = 256
SEL_UNROLL = 4
T_PROJ = 256
T_TOPK = 256
T_PEER = 64
assert T_PROJ == KC == TQ
HALF = D_MODEL // 2
ROWS_PER_EXPERT = HALF // LANES


def _cparams(sem):
    return pltpu.CompilerParams(dimension_semantics=sem, vmem_limit_bytes=VMEM_LIMIT)


def _dot_nt(a, b, precision=None):
    return lax.dot_general(a, b, (((1,), (1,)), ((), ())), preferred_element_type=F32, precision=precision)


def _dot(a, b):
    return jnp.dot(a, b, preferred_element_type=F32)


def _rms(x, g):
    return x * lax.rsqrt(jnp.mean(x * x, axis=-1, keepdims=True) + EPS) * g


def _gelu(x):
    c = math.sqrt(2.0 / math.pi)
    return 0.5 * x * (1.0 + jnp.tanh(c * (x + 0.044715 * (x * x * x))))


def _inproj_kernel(x_ref, ln1_ref, w_ref, wt_ref, conv_ref, q_ref, kcvc_ref, kse_ref, kw0_ref, vt_ref, gt_ref, *,
                   tiles_per_seq):
    xn = _rms(x_ref[...], ln1_ref[...]).astype(BF16)
    n_conv, n_q, n_k = 3 * CONV_CH, N_HEADS * LANES, N_KV * LANES
    o = 0
    conv_ref[...] = _dot(xn, w_ref[:, o:o + n_conv])
    o += n_conv
    q_ref[...] = (_dot(xn, w_ref[:, o:o + n_q]) * (HEAD_DIM ** -0.5)).astype(BF16)
    o += n_q
    kcvc_ref[...] = _dot(xn, w_ref[:, o:o + n_k])
    o += n_k
    pos = (pl.program_id(0) % tiles_per_seq) * T_PROJ + lax.broadcasted_iota(jnp.int32, (T_PROJ, n_k), 0)
    lane = lax.broadcasted_iota(jnp.int32, (T_PROJ, n_k), 1) & (LANES - 1)
    onehot = (lane - HEAD_DIM) == jnp.right_shift(pos, int(math.log2(SEL_BLOCK)))
    kse_ref[...] = jnp.where(onehot, 1.0, _dot(xn, w_ref[:, o:o + n_k])).astype(BF16)
    o += n_k
    kw0_ref[...] = _dot(xn, w_ref[:, o:o + n_k]).astype(BF16)
    ones_rows = lax.broadcasted_iota(jnp.int32, (LANES, T_PROJ), 0) >= HEAD_DIM
    for j in range(2 * N_KV):
        vt = _dot_nt(wt_ref[j * LANES:(j + 1) * LANES, :], xn)
        vt_ref[0, j] = jnp.where(ones_rows, 1.0, vt).astype(BF16)
    r0 = 2 * N_KV * LANES
    gt_ref[0] = jax.nn.sigmoid(_dot_nt(wt_ref[r0:r0 + 4 * SUBLANES, :], xn))


def _inproj(x2, ln1, w_cat, w_t, seq_len):
    t = x2.shape[0]
    row = lambda i: (i, 0)
    fixed = lambda i: (0, 0)
    n_k = N_KV * LANES
    return pl.pallas_call(
        functools.partial(_inproj_kernel, tiles_per_seq=seq_len // T_PROJ),
        grid=(t // T_PROJ,),
        in_specs=[pl.BlockSpec((T_PROJ, D_MODEL), row),
                  pl.BlockSpec((1, D_MODEL), fixed),
                  pl.BlockSpec(w_cat.shape, fixed),
                  pl.BlockSpec(w_t.shape, fixed)],
        out_specs=[pl.BlockSpec((T_PROJ, 3 * CONV_CH), row),
                   pl.BlockSpec((T_PROJ, N_HEADS * LANES), row),
                   pl.BlockSpec((T_PROJ, n_k), row),
                   pl.BlockSpec((T_PROJ, n_k), row),
                   pl.BlockSpec((T_PROJ, n_k), row),
                   pl.BlockSpec((1, 2 * N_KV, LANES, T_PROJ), lambda i: (i, 0, 0, 0)),
                   pl.BlockSpec((1, 4 * SUBLANES, T_PROJ), lambda i: (i, 0, 0))],
        out_shape=[jax.ShapeDtypeStruct((t, 3 * CONV_CH), F32),
                   jax.ShapeDtypeStruct((t, N_HEADS * LANES), BF16),
                   jax.ShapeDtypeStruct((t, n_k), F32),
                   jax.ShapeDtypeStruct((t, n_k), BF16),
                   jax.ShapeDtypeStruct((t, n_k), BF16),
                   jax.ShapeDtypeStruct((t // T_PROJ, 2 * N_KV, LANES, T_PROJ), BF16),
                   jax.ShapeDtypeStruct((t // T_PROJ, 4 * SUBLANES, T_PROJ), F32)],
        compiler_params=_cparams(("parallel",)),
        name="inproj",
    )(x2, ln1, w_cat, w_t)


def _compress_kernel(ck_ref, cv_ref, posk_ref, posv_ref, w1k_ref, w2k_ref, w1v_ref, w2v_ref, ok_ref, ov_ref):
    half = CMP_STRIDE * HEAD_DIM
    for c_ref, pos_ref, w1_ref, w2_ref, o_ref in ((ck_ref, posk_ref, w1k_ref, w2k_ref, ok_ref),
                                                  (cv_ref, posv_ref, w1v_ref, w2v_ref, ov_ref)):
        for g in range(N_KV):
            c = c_ref[0, g]
            nc = c.shape[0]
            a = _dot((c + pos_ref[0:1, :]).astype(BF16), w1_ref[0:half, :])
            b = _dot((c + pos_ref[1:2, :]).astype(BF16), w1_ref[half:2 * half, :])
            hid = a + pltpu.roll(b, nc - 1, 0)
            out = _dot(_gelu(hid).astype(BF16), w2_ref[...])
            rows = lax.broadcasted_iota(jnp.int32, out.shape, 0)
            o_ref[0, g] = jnp.where(rows < nc - 1, out, 0.0).astype(BF16)


def _compress(ck, cv, posk, posv, w1k, w2k, w1v, w2v):
    b, g, nc, cw = ck.shape
    blk = lambda i: (i, 0, 0, 0)
    fixed = lambda i: (0, 0)
    return pl.pallas_call(
        _compress_kernel,
        grid=(b,),
        in_specs=[pl.BlockSpec((1, g, nc, cw), blk), pl.BlockSpec((1, g, nc, cw), blk),
                  pl.BlockSpec(posk.shape, fixed), pl.BlockSpec(posv.shape, fixed),
                  pl.BlockSpec(w1k.shape, fixed), pl.BlockSpec(w2k.shape, fixed),
                  pl.BlockSpec(w1v.shape, fixed), pl.BlockSpec(w2v.shape, fixed)],
        out_specs=[pl.BlockSpec((1, g, nc, HEAD_DIM), blk), pl.BlockSpec((1, g, nc, HEAD_DIM), blk)],
        out_shape=[jax.ShapeDtypeStruct((b, g, nc, HEAD_DIM), BF16)] * 2,
        compiler_params=_cparams(("parallel",)),
        name="compress",
    )(ck, cv, posk, posv, w1k, w2k, w1v, w2v)


def _nsa_kernel(q_ref, kse_ref, kw0_ref, vt_ref, kc0_ref, vct_ref, gt_ref, ovt_ref, o_ref, m_ref, acc_ref):
    qt = pl.program_id(1)
    t0 = qt * TQ
    nc = kc0_ref.shape[2]
    n_sel = ovt_ref.shape[0]
    w = HPG * TQ
    tq = t0 + (lax.broadcasted_iota(jnp.int32, (1, w), 1) & (TQ - 1))
    krow = lax.broadcasted_iota(jnp.int32, (KC, 1), 0)
    causal = (t0 + krow) <= tq
    band = (t0 - 2 * KC + krow) > (tq - WINDOW)
    eye = (lax.broadcasted_iota(jnp.int32, (TQ, TQ), 0)
           == lax.broadcasted_iota(jnp.int32, (TQ, TQ), 1)).astype(BF16)
    nrow = lax.broadcasted_iota(jnp.int32, (nc, 1), 0)
    valid_c = ((nrow * CMP_STRIDE + (CMP_BLOCK - 1)) <= tq) & (nrow < nc - 1)
    jrow = lax.broadcasted_iota(jnp.int32, (n_sel, TQ), 0)
    qblk = jnp.right_shift(t0 + lax.broadcasted_iota(jnp.int32, (n_sel, TQ), 1), int(math.log2(SEL_BLOCK)))
    forced = (jrow == 0) | (jrow == qblk) | (jrow == qblk - 1)
    lane128 = lax.broadcasted_iota(jnp.int32, (TQ, LANES), 1)
    gt = gt_ref[0, 0]

    def online(s, vt):
        m_old = m_ref[0:1, :]
        m_new = jnp.maximum(m_old, jnp.max(s, axis=0, keepdims=True))
        alpha = jnp.exp(m_old - m_new)
        p = jnp.exp(s - m_new)
        acc_ref[...] = alpha * acc_ref[...] + _dot(vt, p.astype(BF16))
        m_ref[...] = jnp.broadcast_to(m_new, (SUBLANES, w))

    def first(s, vt):
        m = jnp.max(s, axis=0, keepdims=True)
        acc_ref[...] = _dot(vt, jnp.exp(s - m).astype(BF16))
        m_ref[...] = jnp.broadcast_to(m, (SUBLANES, w))

    def finish():
        a = acc_ref[...]
        return a[0:HEAD_DIM, :] / a[HEAD_DIM:HEAD_DIM + 1, :]

    for g in range(N_KV):
        kc0 = kc0_ref[0, g]
        q_all = jnp.concatenate([q_ref[0, :, (g * HPG + h) * LANES:(g * HPG + h + 1) * LANES]
                                 for h in range(HPG)], axis=0)
        s = jnp.where(valid_c, _dot_nt(kc0, q_all), NEG_INF)
        m = jnp.max(s, axis=0, keepdims=True)
        p = jnp.where(valid_c, jnp.exp(s - m), 0.0)
        l = jnp.sum(p, axis=0, keepdims=True)
        pn = p * (1.0 / jnp.where(l > 0.0, l, 1.0))
        o_c = _dot(vct_ref[0, g], pn.astype(BF16))
        psum = pn[:, 0:TQ]
        for h in range(1, HPG):
            psum = psum + pn[:, h * TQ:(h + 1) * TQ]
        imp_t = jnp.dot(ovt_ref[...], psum, preferred_element_type=F32,
                        precision=lax.Precision.HIGHEST)
        val = jnp.where(jrow > qblk, -FORCE, imp_t + jnp.where(forced, FORCE, 0.0))
        rank = jnp.zeros((n_sel, TQ), jnp.int32)
        for k in range(n_sel):
            vk = val[k:k + 1, :]
            ahead = (vk > val) | ((vk == val) & (jrow > k))
            rank = rank + ahead.astype(jnp.int32)
        sel_t = (rank < SEL_TOP).astype(BF16)
        pad_t = jnp.concatenate([jnp.zeros((LANES - n_sel, TQ), BF16), sel_t], axis=0)
        sel_q = _dot_nt(eye, pad_t)
        bias = jnp.where((lane128 >= HEAD_DIM) & (sel_q < 0.5), NEG_INF, 0.0).astype(BF16)
        lhs = q_all + jnp.concatenate([bias] * HPG, axis=0)
        gl = slice(g * LANES, (g + 1) * LANES)
        kd = kse_ref[0, pl.ds(pl.multiple_of(t0, KC), KC), gl]
        first(jnp.where(causal, _dot_nt(kd, lhs), NEG_INF), vt_ref[0, qt, g])

        def scores(c):
            return _dot_nt(kse_ref[0, pl.ds(pl.multiple_of(c * KC, KC), KC), gl], lhs)

        def sel_body(j, carry):
            cs = [SEL_UNROLL * j + u for u in range(SEL_UNROLL)]
            ss = [scores(c) for c in cs]
            for c, s in zip(cs, ss):
                online(s, vt_ref[0, c, g])
            return carry

        def sel_tail(c, carry):
            online(scores(c), vt_ref[0, c, g])
            return carry

        n_main = qt // SEL_UNROLL
        lax.fori_loop(0, n_main, sel_body, 0)
        lax.fori_loop(n_main * SEL_UNROLL, qt, sel_tail, 0)
        o_s = finish()
        c1, c2 = jnp.maximum(qt - 1, 0), jnp.maximum(qt - 2, 0)
        wscores = lambda c: _dot_nt(kw0_ref[0, pl.ds(pl.multiple_of(c * KC, KC), KC), gl], q_all)
        s0 = jnp.where(causal, wscores(qt), NEG_INF)
        s1 = jnp.where(qt >= 1, wscores(c1), NEG_INF)
        s2 = jnp.where(band & (qt >= 2), wscores(c2), NEG_INF)
        first(s0, vt_ref[0, qt, N_KV + g])
        online(s1, vt_ref[0, c1, N_KV + g])
        online(s2, vt_ref[0, c2, N_KV + g])
        o_w = finish()
        outs = []
        for h in range(HPG):
            r = 3 * (g * HPG + h)
            c0, c1 = h * TQ, (h + 1) * TQ
            outs.append(gt[r:r + 1, :] * o_c[:, c0:c1] + gt[r + 1:r + 2, :] * o_s[:, c0:c1]
                        + gt[r + 2:r + 3, :] * o_w[:, c0:c1])
        o_ref[0, :, g * HPG * HEAD_DIM:(g + 1) * HPG * HEAD_DIM] = jnp.concatenate(outs, axis=0).T


def _nsa(q, kse, kw0, vt, kc0, vct, gates_t, ovt):
    b, s, _ = q.shape
    nc = kc0.shape[2]
    seq3 = lambda i, j: (i, 0, 0)
    seq = lambda i, j: (i, 0, 0, 0)
    seq5 = lambda i, j: (i, 0, 0, 0, 0)
    tile = lambda i, j: (i, j, 0)
    fixed = lambda i, j: (0, 0)
    return pl.pallas_call(
        _nsa_kernel,
        grid=(b, s // TQ),
        in_specs=[pl.BlockSpec((1, TQ, N_HEADS * LANES), tile),
                  pl.BlockSpec((1, s, N_KV * LANES), seq3), pl.BlockSpec((1, s, N_KV * LANES), seq3),
                  pl.BlockSpec((1, s // KC, 2 * N_KV, LANES, KC), seq5),
                  pl.BlockSpec((1, N_KV, nc, LANES), seq), pl.BlockSpec((1, N_KV, HEAD_DIM, nc), seq),
                  pl.BlockSpec((1, 1, 4 * SUBLANES, TQ), lambda i, j: (i, j, 0, 0)),
                  pl.BlockSpec(ovt.shape, fixed)],
        out_specs=pl.BlockSpec((1, TQ, ATTN_W), tile),
        out_shape=jax.ShapeDtypeStruct((b, s, ATTN_W), F32),
        scratch_shapes=[pltpu.VMEM((SUBLANES, HPG * TQ), F32), pltpu.VMEM((LANES, HPG * TQ), F32)],
        compiler_params=_cparams(("parallel", "arbitrary")),
        name="nsa",
    )(q, kse, kw0, vt, kc0, vct, gates_t, ovt)


def _post_kernel(x_ref, conv_ref, halo_ref, attn_ref, convw_ref, gnc_ref, gna_ref, wout_ref, ln2_ref, wq_ref,
                 h1_ref, hn_ref, pq_ref, *, tiles_per_seq):
    i = pl.program_id(0)
    c_h = conv_ref[:, 0:CONV_CH]
    c_b = conv_ref[:, CONV_CH:2 * CONV_CH]
    c_c = conv_ref[:, 2 * CONV_CH:3 * CONV_CH]
    z = c_c * c_h
    keep = jnp.where(i % tiles_per_seq == 0, 0.0, 1.0)
    zp = halo_ref[:, 2 * CONV_CH:3 * CONV_CH] * halo_ref[:, 0:CONV_CH] * keep
    rows = lax.broadcasted_iota(jnp.int32, z.shape, 0)
    n = z.shape[0]
    z1 = jnp.where(rows == 0, zp[SUBLANES - 1:SUBLANES, :], pltpu.roll(z, 1, 0))
    z2 = jnp.where(rows == 0, zp[SUBLANES - 2:SUBLANES - 1, :],
                   jnp.where(rows == 1, zp[SUBLANES - 1:SUBLANES, :], pltpu.roll(z, 2, 0)))
    conv = convw_ref[0:1, :] * z2 + convw_ref[1:2, :] * z1 + convw_ref[2:3, :] * z
    nc = _rms(c_b * conv, gnc_ref[...]).astype(BF16)
    na = _rms(attn_ref[...], gna_ref[...]).astype(BF16)
    h1 = x_ref[...] + _dot(nc, wout_ref[0:CONV_CH, :]) + _dot(na, wout_ref[CONV_CH:CONV_CH + ATTN_W, :])
    h1_ref[...] = h1
    hn = _rms(h1, ln2_ref[...])
    hn_ref[...] = hn
    pq_ref[...] = _dot(hn.astype(BF16), wq_ref[...]).astype(BF16)


def _post(x2, conv, attn2, convw, gnc, gna, wout, ln2, wq, seq_len):
    t = x2.shape[0]
    row = lambda i: (i, 0)
    fixed = lambda i: (0, 0)
    halo = lambda i: (jnp.maximum(i * (T_PROJ // SUBLANES) - 1, 0), 0)
    nq = wq.shape[1]
    return pl.pallas_call(
        functools.partial(_post_kernel, tiles_per_seq=seq_len // T_PROJ),
        grid=(t // T_PROJ,),
        in_specs=[pl.BlockSpec((T_PROJ, D_MODEL), row),
                  pl.BlockSpec((T_PROJ, 3 * CONV_CH), row),
                  pl.BlockSpec((SUBLANES, 3 * CONV_CH), halo),
                  pl.BlockSpec((T_PROJ, ATTN_W), row),
                  pl.BlockSpec(convw.shape, fixed), pl.BlockSpec(gnc.shape, fixed), pl.BlockSpec(gna.shape, fixed),
                  pl.BlockSpec(wout.shape, fixed), pl.BlockSpec(ln2.shape, fixed), pl.BlockSpec(wq.shape, fixed)],
        out_specs=[pl.BlockSpec((T_PROJ, D_MODEL), row), pl.BlockSpec((T_PROJ, D_MODEL), row),
                   pl.BlockSpec((T_PROJ, nq), row)],
        out_shape=[jax.ShapeDtypeStruct((t, D_MODEL), F32), jax.ShapeDtypeStruct((t, D_MODEL), F32),
                   jax.ShapeDtypeStruct((t, nq), BF16)],
        compiler_params=_cparams(("parallel",)),
        name="post",
    )(x2, conv, conv, attn2, convw, gnc, gna, wout, ln2, wq)


def _staircase():
    return [(a, b) for a in range(PEER_TOPK) for b in range(PEER_TOPK) if (a + 1) * (b + 1) <= PEER_TOPK]


N_CAND = 56


def _topk_kernel(pq_ref, sk_ref, e_ref, g_ref, sv_ref, si_ref, cand_ref, ce_ref, et_ref, gt_ref):
    tt = pq_ref.shape[0]
    rown = lax.broadcasted_iota(jnp.int32, (PEER_NKEYS, tt), 0)
    rowc = lax.broadcasted_iota(jnp.int32, (N_CAND, tt), 0)
    pairs = _staircase()
    for h in range(PEER_HEADS):
        for c in range(2):
            off = (h * 2 + c) * PEER_NKEYS
            x = _dot_nt(sk_ref[h, c], pq_ref[:, off:off + PEER_NKEYS])
            for it in range(PEER_TOPK):
                m = jnp.max(x, axis=0, keepdims=True)
                idx = jnp.min(jnp.where(x == m, rown, PEER_NKEYS), axis=0, keepdims=True)
                sv_ref[c, it:it + 1, :] = m
                si_ref[c, it:it + 1, :] = idx
                x = jnp.where(rown == idx, -jnp.inf, x)
        cand_ref[...] = jnp.full((N_CAND, tt), -jnp.inf, F32)
        ce_ref[...] = jnp.zeros((N_CAND, tt), jnp.int32)
        for r, (a, b) in enumerate(pairs):
            cand_ref[r:r + 1, :] = sv_ref[0, a:a + 1, :] + sv_ref[1, b:b + 1, :]
            ce_ref[r:r + 1, :] = si_ref[0, a:a + 1, :] * PEER_NKEYS + si_ref[1, b:b + 1, :]
        x = cand_ref[...]
        ce = ce_ref[...]
        best = []
        for it in range(PEER_TOPK):
            m = jnp.max(x, axis=0, keepdims=True)
            idx = jnp.min(jnp.where(x == m, rowc, N_CAND), axis=0, keepdims=True)
            hit = rowc == idx
            et_ref[h * PEER_TOPK + it:h * PEER_TOPK + it + 1, :] = jnp.max(jnp.where(hit, ce, -1), axis=0, keepdims=True)
            best.append(m)
            x = jnp.where(hit, -jnp.inf, x)
        ex = [jnp.exp(v - best[0]) for v in best]
        tot = ex[0]
        for v in ex[1:]:
            tot = tot + v
        inv = 1.0 / tot
        for it in range(PEER_TOPK):
            gt_ref[h * PEER_TOPK + it:h * PEER_TOPK + it + 1, :] = ex[it] * inv
    e_ref[...] = (et_ref[...] * ROWS_PER_EXPERT).T
    g_ref[...] = gt_ref[...].T


def _topk(pq, sk):
    t, nq = pq.shape
    nk = PEER_HEADS * PEER_TOPK
    row = lambda i: (i, 0)
    return pl.pallas_call(
        _topk_kernel,
        grid=(t // T_TOPK,),
        in_specs=[pl.BlockSpec((T_TOPK, nq), row), pl.BlockSpec(sk.shape, lambda i: (0, 0, 0, 0))],
        out_specs=[pl.BlockSpec((T_TOPK, nk), row), pl.BlockSpec((T_TOPK, nk), row)],
        out_shape=[jax.ShapeDtypeStruct((t, nk), jnp.int32), jax.ShapeDtypeStruct((t, nk), F32)],
        scratch_shapes=[pltpu.VMEM((2, PEER_TOPK, T_TOPK), F32), pltpu.VMEM((2, PEER_TOPK, T_TOPK), jnp.int32),
                        pltpu.VMEM((N_CAND, T_TOPK), F32), pltpu.VMEM((N_CAND, T_TOPK), jnp.int32),
                        pltpu.VMEM((nk, T_TOPK), jnp.int32), pltpu.VMEM((nk, T_TOPK), F32)],
        compiler_params=_cparams(("parallel",)),
        name="topk",
    )(pq, sk)


def _gather_rows(e_ref, tab_ref, t):
    rows = []
    for k in range(e_ref.shape[1]):
        e0 = pl.multiple_of(e_ref[t, k], ROWS_PER_EXPERT)
        rows.append(tab_ref[pl.ds(e0, ROWS_PER_EXPERT), :])
    return pltpu.bitcast(jnp.concatenate(rows, axis=0), BF16)


def _for_each_token(e_hbm, e_refs, sem, tab_ref, compute):
    i = pl.program_id(0)
    half = e_refs[0].shape[0]

    def copy(step, part):
        rows = pl.ds(pl.multiple_of((2 * step + part) * half, half), half)
        return pltpu.make_async_copy(e_hbm.at[rows], e_refs[part], sem.at[part])

    @pl.when(i == 0)
    def _():
        copy(0, 0).start()

    copy(i, 1).start()
    copy(i, 0).wait()
    w_next = _gather_rows(e_refs[0], tab_ref, 0)
    for t in range(2 * half):
        w_cur = w_next
        if t + 1 == half:
            @pl.when(i + 1 < pl.num_programs(0))
            def _():
                copy(i + 1, 0).start()

            copy(i, 1).wait()
        if t + 1 < 2 * half:
            w_next = _gather_rows(e_refs[(t + 1) // half], tab_ref, (t + 1) % half)
        compute(t, w_cur)


def _diag16(n):
    lane = lax.broadcasted_iota(jnp.int32, (2 * SUBLANES, n), 1)
    row = lax.broadcasted_iota(jnp.int32, (2 * SUBLANES, n), 0)
    return (lane & (SUBLANES - 1)) == (row & (SUBLANES - 1)), row < SUBLANES


def _peer_u_kernel(e_hbm, tab_ref, x_ref, gate_ref, g8_ref, a_ref, e0_ref, e1_ref, sem, hs_ref):
    diag, _ = _diag16(SUBLANES * e0_ref.shape[1])
    top = lax.broadcasted_iota(jnp.int32, (2 * SUBLANES, LANES), 0) < SUBLANES

    def token(t, w):
        parts = [x_ref[t:t + 1, i * LANES:(i + 1) * LANES] for i in range(SUBLANES)]
        xx = jnp.concatenate(parts + parts, axis=0)
        hi = xx.astype(BF16).astype(F32)
        x16 = jnp.where(top, hi, xx - hi).astype(BF16)
        r = _dot_nt(x16, w)
        hs_ref[t:t + 1, :] = jnp.sum(jnp.where(diag, r, 0.0), axis=0, keepdims=True)

    _for_each_token(e_hbm, (e0_ref, e1_ref), sem, tab_ref, token)
    h = jnp.dot(hs_ref[...], g8_ref[...], preferred_element_type=F32, precision=lax.Precision.HIGHEST)
    a_ref[...] = gate_ref[...] * _gelu(h)


def _peer_u(eidx, tab, hn, gate, g8):
    t, nk = eidx.shape
    row = lambda i: (i, 0)
    fixed = lambda i: (0, 0)
    return pl.pallas_call(
        _peer_u_kernel,
        grid=(t // T_PEER,),
        in_specs=[pl.BlockSpec(memory_space=pl.ANY),
                  pl.BlockSpec(tab.shape, fixed, pipeline_mode=pl.Buffered(1)),
                  pl.BlockSpec((T_PEER, D_MODEL), row),
                  pl.BlockSpec((T_PEER, nk), row),
                  pl.BlockSpec(g8.shape, fixed)],
        out_specs=pl.BlockSpec((T_PEER, nk), row),
        out_shape=jax.ShapeDtypeStruct((t, nk), F32),
        scratch_shapes=[pltpu.SMEM((T_PEER // 2, nk), jnp.int32), pltpu.SMEM((T_PEER // 2, nk), jnp.int32),
                        pltpu.SemaphoreType.DMA((2,)),
                        pltpu.VMEM((T_PEER, SUBLANES * nk), F32)],
        compiler_params=_cparams(("arbitrary",)),
        name="peer_u",
    )(eidx, tab, hn, gate, g8)


def _peer_v_kernel(e_hbm, tab_ref, a_ref, rep_ref, h1_ref, lnf_ref, o_ref, e0_ref, e1_ref, sem, arep_ref, po_ref, *,
                   last_layer):
    n = SUBLANES * e0_ref.shape[1]
    diag, top = _diag16(n)
    arep_ref[...] = jnp.dot(a_ref[...], rep_ref[...], preferred_element_type=F32, precision=lax.Precision.HIGHEST)

    def token(t, w):
        l32 = jnp.where(diag, jnp.broadcast_to(arep_ref[t:t + 1, :], (2 * SUBLANES, n)), 0.0)
        hi = l32.astype(BF16).astype(F32)
        lhs = jnp.where(top, hi, l32 - hi).astype(BF16)
        out = _dot(lhs, w)
        out = out[0:SUBLANES] + out[SUBLANES:2 * SUBLANES]
        for i in range(SUBLANES):
            po_ref[t:t + 1, i * LANES:(i + 1) * LANES] = out[i:i + 1, :]

    _for_each_token(e_hbm, (e0_ref, e1_ref), sem, tab_ref, token)
    h = h1_ref[...] + po_ref[...]
    o_ref[...] = _rms(h, lnf_ref[...]) if last_layer else h


def _peer_v(eidx, tab, a, rep, h1, lnf, last_layer):
    t, nk = eidx.shape
    row = lambda i: (i, 0)
    fixed = lambda i: (0, 0)
    return pl.pallas_call(
        functools.partial(_peer_v_kernel, last_layer=last_layer),
        grid=(t // T_PEER,),
        in_specs=[pl.BlockSpec(memory_space=pl.ANY),
                  pl.BlockSpec(tab.shape, fixed, pipeline_mode=pl.Buffered(1)),
                  pl.BlockSpec((T_PEER, nk), row),
                  pl.BlockSpec(rep.shape, fixed),
                  pl.BlockSpec((T_PEER, D_MODEL), row),
                  pl.BlockSpec((1, D_MODEL), fixed)],
        out_specs=pl.BlockSpec((T_PEER, D_MODEL), row),
        out_shape=jax.ShapeDtypeStruct((t, D_MODEL), F32),
        scratch_shapes=[pltpu.SMEM((T_PEER // 2, nk), jnp.int32), pltpu.SMEM((T_PEER // 2, nk), jnp.int32),
                        pltpu.SemaphoreType.DMA((2,)),
                        pltpu.VMEM((T_PEER, SUBLANES * nk), F32),
                        pltpu.VMEM((T_PEER, D_MODEL), F32)],
        compiler_params=_cparams(("arbitrary",)),
        name="peer_v",
    )(eidx, tab, a, rep, h1, lnf)


def _pack_rows(a):
    r, n, _ = a.shape
    bits = lax.bitcast_convert_type(a, jnp.uint16).astype(jnp.uint32).reshape(r, n // 2, 2, LANES)
    return (bits[:, :, 0] | (bits[:, :, 1] << 16)).reshape(r * n // 2, LANES)


def _pad_lanes(a, left):
    z = jnp.zeros_like(a)
    return jnp.concatenate([a, z] if left else [z, a], axis=-1)


def _layer(h, l, ln1, w_in, conv_w, cmp_pos_k, cmp_pos_v, cmp_k_w1, cmp_k_w2, cmp_v_w1, cmp_v_w2,
           gn_conv, gn_attn, w_out, ln2, peer_wq, peer_subkeys, peer_u, peer_v, ln_f, last_layer):
    b, s, _ = h.shape
    t = b * s
    x2 = h.reshape(t, D_MODEL)
    w = w_in[l]
    o_q = 3 * CONV_CH
    o_kv = o_q + ATTN_W
    o_g = o_kv + 6 * N_KV * HEAD_DIM
    kvw = N_KV * HEAD_DIM
    part = lambda i: w[:, o_kv + i * kvw:o_kv + (i + 1) * kvw]
    pad_heads = lambda a, n: _pad_lanes(a.reshape(D_MODEL, n, HEAD_DIM), True).reshape(D_MODEL, n * LANES)
    w_cat = jnp.concatenate([w[:, :o_q], pad_heads(w[:, o_q:o_kv], N_HEADS), part(0), part(1),
                             pad_heads(part(2), N_KV), pad_heads(part(4), N_KV)], axis=1).astype(BF16)
    w_t = jnp.concatenate([pad_heads(part(3), N_KV).T, pad_heads(part(5), N_KV).T,
                           jnp.pad(w[:, o_g:], ((0, 0), (0, 4 * SUBLANES - N_GATES))).T], axis=0).astype(BF16)
    conv, q, kcvc, kse, kw0, vt, gates_t = _inproj(x2, ln1[l][None, :], w_cat, w_t, s)

    nc = s // CMP_STRIDE

    def chunks(a):
        a = a.reshape(b, nc, CMP_STRIDE, N_KV, HEAD_DIM).transpose(0, 3, 1, 2, 4)
        return a.reshape(b, N_KV, nc, CMP_STRIDE * HEAD_DIM)

    pos2 = lambda p: p.reshape(2, CMP_STRIDE * HEAD_DIM)
    kcc, vcc = _compress(chunks(kcvc[:, :LANES]), chunks(kcvc[:, LANES:]), pos2(cmp_pos_k[l]), pos2(cmp_pos_v[l]),
                         cmp_k_w1[l].astype(BF16), cmp_k_w2[l].astype(BF16),
                         cmp_v_w1[l].astype(BF16), cmp_v_w2[l].astype(BF16))

    attn = _attention(b, s, q, kse, kw0, vt, kcc, vcc, gates_t)

    h1, hn, pq = _post(x2, conv, attn.reshape(t, ATTN_W), conv_w[l], gn_conv[l][None, :], gn_attn[l][None, :],
                           w_out[l].astype(BF16), ln2[l][None, :], peer_wq[l].astype(BF16), s)

    eidx, gate = _topk(pq, peer_subkeys[l].astype(BF16))
    nk = PEER_HEADS * PEER_TOPK
    g8 = (jnp.arange(SUBLANES * nk)[:, None] // SUBLANES == jnp.arange(nk)[None, :]).astype(F32)
    table = lambda w: _pack_rows(w.astype(BF16).reshape(PEER_EXPERTS, SUBLANES, LANES))
    a = _peer_u(eidx, table(peer_u[l]), hn, gate, g8)
    return _peer_v(eidx, table(peer_v[l]), a, g8.T, h1, ln_f[None, :], last_layer).reshape(b, s, D_MODEL)


def _attention(b, s, q, kse, kw0, vt, kcc, vcc, gates_t):
    nc = s // CMP_STRIDE
    n_sel = s // SEL_BLOCK
    kc0 = _pad_lanes(kcc, True)
    vct = vcc.transpose(0, 1, 3, 2)
    kse, kw0 = kse.reshape(b, s, N_KV * LANES), kw0.reshape(b, s, N_KV * LANES)
    vt = vt.reshape(b, s // KC, 2 * N_KV, LANES, KC)
    gates_t = gates_t.reshape(b, s // TQ, 4 * SUBLANES, TQ)
    n_cmp = (s - CMP_BLOCK) // CMP_STRIDE + 1
    cs = np.arange(nc) * CMP_STRIDE
    ss = np.arange(HEAD_DIM) * SEL_BLOCK
    ov = ((cs[:, None] < ss[None, :] + SEL_BLOCK) & (cs[:, None] + CMP_BLOCK > ss[None, :])
          & (np.arange(nc)[:, None] < n_cmp) & (np.arange(HEAD_DIM)[None, :] < n_sel))
    ovt = jnp.asarray(ov.T.astype(np.float32))
    return _nsa(q.reshape(b, s, N_HEADS * LANES), kse, kw0, vt, kc0, vct, gates_t, ovt)


def kernel(x, ln1, w_in, conv_w, cmp_pos_k, cmp_pos_v, cmp_k_w1, cmp_k_w2, cmp_v_w1, cmp_v_w2, gn_conv, gn_attn,
           w_out, ln2, peer_wq, peer_subkeys, peer_u, peer_v, ln_f):
    b, s, _ = x.shape
    depth = w_in.shape[0]
    h = x
    for l in range(depth):
        h = _layer(h, l, ln1, w_in, conv_w, cmp_pos_k, cmp_pos_v, cmp_k_w1, cmp_k_w2, cmp_v_w1, cmp_v_w2,
                   gn_conv, gn_attn, w_out, ln2, peer_wq, peer_subkeys, peer_u, peer_v, ln_f, l + 1 == depth)
    return h
```

```python
import functools
import math

import jax
import jax.numpy as jnp
import numpy as np
from jax import lax
from jax.experimental import pallas as pl
from jax.experimental.pallas import tpu as pltpu

F32 = jnp.float32
BF16 = jnp.bfloat16

D_MODEL = 1024
CONV_CH = 512
CONV_K = 3
N_HEADS = 8
HEAD_DIM = 64
N_KV = 2
HPG = N_HEADS // N_KV
ATTN_W = N_HEADS * HEAD_DIM
CMP_BLOCK = 32
CMP_STRIDE = 16
CMP_HIDDEN = 256
SEL_BLOCK = 64
SEL_TOP = 16
WINDOW = 512
N_GATES = 3 * N_HEADS
PEER_HEADS = 8
PEER_NKEYS = 128
PEER_EXPERTS = PEER_NKEYS * PEER_NKEYS
PEER_DKEY = 256
PEER_TOPK = 16
EPS = 1e-6
NEG_INF = -1e30
FORCE = 1e4

LANES = 128
SUBLANES = 8
VMEM_LIMIT = 56 * 1024 * 1024

TQ = 256
KC = 256
SEL_UNROLL = 4
T_PROJ = 256
T_TOPK = 256
T_PEER = 256
assert T_PROJ == KC == TQ
HALF = D_MODEL // 2
ROWS_PER_EXPERT = HALF // LANES


def _cparams(sem):
    return pltpu.CompilerParams(dimension_semantics=sem, vmem_limit_bytes=VMEM_LIMIT)


def _dot_nt(a, b, precision=None):
    return lax.dot_general(a, b, (((1,), (1,)), ((), ())), preferred_element_type=F32, precision=precision)


def _dot(a, b):
    return jnp.dot(a, b, preferred_element_type=F32)


def _rms(x, g):
    return x * lax.rsqrt(jnp.mean(x * x, axis=-1, keepdims=True) + EPS) * g


def _gelu(x):
    c = math.sqrt(2.0 / math.pi)
    return 0.5 * x * (1.0 + jnp.tanh(c * (x + 0.044715 * (x * x * x))))


def _inproj_kernel(x_ref, ln1_ref, w_ref, wt_ref, conv_ref, q_ref, kcvc_ref, kse_ref, kw0_ref, vt_ref, gt_ref, *,
                   tiles_per_seq):
    xn = _rms(x_ref[...], ln1_ref[...]).astype(BF16)
    n_conv, n_q, n_k = 3 * CONV_CH, N_HEADS * LANES, N_KV * LANES
    o = 0
    conv_ref[...] = _dot(xn, w_ref[:, o:o + n_conv])
    o += n_conv
    q_ref[...] = (_dot(xn, w_ref[:, o:o + n_q]) * (HEAD_DIM ** -0.5)).astype(BF16)
    o += n_q
    kcvc_ref[...] = _dot(xn, w_ref[:, o:o + n_k])
    o += n_k
    pos = (pl.program_id(0) % tiles_per_seq) * T_PROJ + lax.broadcasted_iota(jnp.int32, (T_PROJ, n_k), 0)
    lane = lax.broadcasted_iota(jnp.int32, (T_PROJ, n_k), 1) & (LANES - 1)
    onehot = (lane - HEAD_DIM) == jnp.right_shift(pos, int(math.log2(SEL_BLOCK)))
    kse_ref[...] = jnp.where(onehot, 1.0, _dot(xn, w_ref[:, o:o + n_k])).astype(BF16)
    o += n_k
    kw0_ref[...] = _dot(xn, w_ref[:, o:o + n_k]).astype(BF16)
    ones_rows = lax.broadcasted_iota(jnp.int32, (LANES, T_PROJ), 0) >= HEAD_DIM
    for j in range(2 * N_KV):
        vt = _dot_nt(wt_ref[j * LANES:(j + 1) * LANES, :], xn)
        vt_ref[0, j] = jnp.where(ones_rows, 1.0, vt).astype(BF16)
    r0 = 2 * N_KV * LANES
    gt_ref[0] = jax.nn.sigmoid(_dot_nt(wt_ref[r0:r0 + 4 * SUBLANES, :], xn))


def _inproj(x2, ln1, w_cat, w_t, seq_len):
    t = x2.shape[0]
    row = lambda i: (i, 0)
    fixed = lambda i: (0, 0)
    n_k = N_KV * LANES
    return pl.pallas_call(
        functools.partial(_inproj_kernel, tiles_per_seq=seq_len // T_PROJ),
        grid=(t // T_PROJ,),
        in_specs=[pl.BlockSpec((T_PROJ, D_MODEL), row),
                  pl.BlockSpec((1, D_MODEL), fixed),
                  pl.BlockSpec(w_cat.shape, fixed),
                  pl.BlockSpec(w_t.shape, fixed)],
        out_specs=[pl.BlockSpec((T_PROJ, 3 * CONV_CH), row),
                   pl.BlockSpec((T_PROJ, N_HEADS * LANES), row),
                   pl.BlockSpec((T_PROJ, n_k), row),
                   pl.BlockSpec((T_PROJ, n_k), row),
                   pl.BlockSpec((T_PROJ, n_k), row),
                   pl.BlockSpec((1, 2 * N_KV, LANES, T_PROJ), lambda i: (i, 0, 0, 0)),
                   pl.BlockSpec((1, 4 * SUBLANES, T_PROJ), lambda i: (i, 0, 0))],
        out_shape=[jax.ShapeDtypeStruct((t, 3 * CONV_CH), F32),
                   jax.ShapeDtypeStruct((t, N_HEADS * LANES), BF16),
                   jax.ShapeDtypeStruct((t, n_k), F32),
                   jax.ShapeDtypeStruct((t, n_k), BF16),
                   jax.ShapeDtypeStruct((t, n_k), BF16),
                   jax.ShapeDtypeStruct((t // T_PROJ, 2 * N_KV, LANES, T_PROJ), BF16),
                   jax.ShapeDtypeStruct((t // T_PROJ, 4 * SUBLANES, T_PROJ), F32)],
        compiler_params=_cparams(("parallel",)),
        name="inproj",
    )(x2, ln1, w_cat, w_t)


def _compress_kernel(ck_ref, cv_ref, posk_ref, posv_ref, w1k_ref, w2k_ref, w1v_ref, w2v_ref, ok_ref, ov_ref):
    half = CMP_STRIDE * HEAD_DIM
    for c_ref, pos_ref, w1_ref, w2_ref, o_ref in ((ck_ref, posk_ref, w1k_ref, w2k_ref, ok_ref),
                                                  (cv_ref, posv_ref, w1v_ref, w2v_ref, ov_ref)):
        for g in range(N_KV):
            c = c_ref[0, g]
            nc = c.shape[0]
            a = _dot((c + pos_ref[0:1, :]).astype(BF16), w1_ref[0:half, :])
            b = _dot((c + pos_ref[1:2, :]).astype(BF16), w1_ref[half:2 * half, :])
            hid = a + pltpu.roll(b, nc - 1, 0)
            out = _dot(_gelu(hid).astype(BF16), w2_ref[...])
            rows = lax.broadcasted_iota(jnp.int32, out.shape, 0)
            o_ref[0, g] = jnp.where(rows < nc - 1, out, 0.0).astype(BF16)


def _compress(ck, cv, posk, posv, w1k, w2k, w1v, w2v):
    b, g, nc, cw = ck.shape
    blk = lambda i: (i, 0, 0, 0)
    fixed = lambda i: (0, 0)
    return pl.pallas_call(
        _compress_kernel,
        grid=(b,),
        in_specs=[pl.BlockSpec((1, g, nc, cw), blk), pl.BlockSpec((1, g, nc, cw), blk),
                  pl.BlockSpec(posk.shape, fixed), pl.BlockSpec(posv.shape, fixed),
                  pl.BlockSpec(w1k.shape, fixed), pl.BlockSpec(w2k.shape, fixed),
                  pl.BlockSpec(w1v.shape, fixed), pl.BlockSpec(w2v.shape, fixed)],
        out_specs=[pl.BlockSpec((1, g, nc, HEAD_DIM), blk), pl.BlockSpec((1, g, nc, HEAD_DIM), blk)],
        out_shape=[jax.ShapeDtypeStruct((b, g, nc, HEAD_DIM), BF16)] * 2,
        compiler_params=_cparams(("parallel",)),
        name="compress",
    )(ck, cv, posk, posv, w1k, w2k, w1v, w2v)


def _nsa_kernel(q_ref, kse_ref, kw0_ref, vt_ref, kc0_ref, vct_ref, gt_ref, ovt_ref, o_ref, m_ref, acc_ref):
    qt = pl.program_id(1)
    t0 = qt * TQ
    nc = kc0_ref.shape[2]
    n_sel = ovt_ref.shape[0]
    w = HPG * TQ
    tq = t0 + (lax.broadcasted_iota(jnp.int32, (1, w), 1) & (TQ - 1))
    krow = lax.broadcasted_iota(jnp.int32, (KC, 1), 0)
    causal = (t0 + krow) <= tq
    band = (t0 - 2 * KC + krow) > (tq - WINDOW)
    eye = (lax.broadcasted_iota(jnp.int32, (TQ, TQ), 0)
           == lax.broadcasted_iota(jnp.int32, (TQ, TQ), 1)).astype(BF16)
    nrow = lax.broadcasted_iota(jnp.int32, (nc, 1), 0)
    valid_c = ((nrow * CMP_STRIDE + (CMP_BLOCK - 1)) <= tq) & (nrow < nc - 1)
    jrow = lax.broadcasted_iota(jnp.int32, (n_sel, TQ), 0)
    qblk = jnp.right_shift(t0 + lax.broadcasted_iota(jnp.int32, (n_sel, TQ), 1), int(math.log2(SEL_BLOCK)))
    forced = (jrow == 0) | (jrow == qblk) | (jrow == qblk - 1)
    lane128 = lax.broadcasted_iota(jnp.int32, (TQ, LANES), 1)
    gt = gt_ref[0, 0]

    def online(s, vt):
        m_old = m_ref[0:1, :]
        m_new = jnp.maximum(m_old, jnp.max(s, axis=0, keepdims=True))
        alpha = jnp.exp(m_old - m_new)
        p = jnp.exp(s - m_new)
        acc_ref[...] = alpha * acc_ref[...] + _dot(vt, p.astype(BF16))
        m_ref[...] = jnp.broadcast_to(m_new, (SUBLANES, w))

    def first(s, vt):
        m = jnp.max(s, axis=0, keepdims=True)
        acc_ref[...] = _dot(vt, jnp.exp(s - m).astype(BF16))
        m_ref[...] = jnp.broadcast_to(m, (SUBLANES, w))

    def finish():
        a = acc_ref[...]
        return a[0:HEAD_DIM, :] / a[HEAD_DIM:HEAD_DIM + 1, :]

    for g in range(N_KV):
        kc0 = kc0_ref[0, g]
        q_all = jnp.concatenate([q_ref[0, :, (g * HPG + h) * LANES:(g * HPG + h + 1) * LANES]
                                 for h in range(HPG)], axis=0)
        s = jnp.where(valid_c, _dot_nt(kc0, q_all), NEG_INF)
        m = jnp.max(s, axis=0, keepdims=True)
        p = jnp.where(valid_c, jnp.exp(s - m), 0.0)
        l = jnp.sum(p, axis=0, keepdims=True)
        pn = p * (1.0 / jnp.where(l > 0.0, l, 1.0))
        o_c = _dot(vct_ref[0, g], pn.astype(BF16))
        psum = pn[:, 0:TQ]
        for h in range(1, HPG):
            psum = psum + pn[:, h * TQ:(h + 1) * TQ]
        imp_t = jnp.dot(ovt_ref[...], psum, preferred_element_type=F32,
                        precision=lax.Precision.HIGHEST)
        val = jnp.where(jrow > qblk, -FORCE, imp_t + jnp.where(forced, FORCE, 0.0))
        rank = jnp.zeros((n_sel, TQ), jnp.int32)
        for k in range(n_sel):
            vk = val[k:k + 1, :]
            ahead = (vk > val) | ((vk == val) & (jrow > k))
            rank = rank + ahead.astype(jnp.int32)
        sel_t = (rank < SEL_TOP).astype(BF16)
        pad_t = jnp.concatenate([jnp.zeros((LANES - n_sel, TQ), BF16), sel_t], axis=0)
        sel_q = _dot_nt(eye, pad_t)
        bias = jnp.where((lane128 >= HEAD_DIM) & (sel_q < 0.5), NEG_INF, 0.0).astype(BF16)
        lhs = q_all + jnp.concatenate([bias] * HPG, axis=0)
        gl = slice(g * LANES, (g + 1) * LANES)
        kd = kse_ref[0, pl.ds(pl.multiple_of(t0, KC), KC), gl]
        first(jnp.where(causal, _dot_nt(kd, lhs), NEG_INF), vt_ref[0, qt, g])

        def scores(c):
            return _dot_nt(kse_ref[0, pl.ds(pl.multiple_of(c * KC, KC), KC), gl], lhs)

        def sel_body(j, carry):
            cs = [SEL_UNROLL * j + u for u in range(SEL_UNROLL)]
            ss = [scores(c) for c in cs]
            for c, s in zip(cs, ss):
                online(s, vt_ref[0, c, g])
            return carry

        def sel_tail(c, carry):
            online(scores(c), vt_ref[0, c, g])
            return carry

        n_main = qt // SEL_UNROLL
        lax.fori_loop(0, n_main, sel_body, 0)
        lax.fori_loop(n_main * SEL_UNROLL, qt, sel_tail, 0)
        o_s = finish()
        c1, c2 = jnp.maximum(qt - 1, 0), jnp.maximum(qt - 2, 0)
        wscores = lambda c: _dot_nt(kw0_ref[0, pl.ds(pl.multiple_of(c * KC, KC), KC), gl], q_all)
        s0 = jnp.where(causal, wscores(qt), NEG_INF)
        s1 = jnp.where(qt >= 1, wscores(c1), NEG_INF)
        s2 = jnp.where(band & (qt >= 2), wscores(c2), NEG_INF)
        first(s0, vt_ref[0, qt, N_KV + g])
        online(s1, vt_ref[0, c1, N_KV + g])
        online(s2, vt_ref[0, c2, N_KV + g])
        o_w = finish()
        outs = []
        for h in range(HPG):
            r = 3 * (g * HPG + h)
            c0, c1 = h * TQ, (h + 1) * TQ
            outs.append(gt[r:r + 1, :] * o_c[:, c0:c1] + gt[r + 1:r + 2, :] * o_s[:, c0:c1]
                        + gt[r + 2:r + 3, :] * o_w[:, c0:c1])
        o_ref[0, :, g * HPG * HEAD_DIM:(g + 1) * HPG * HEAD_DIM] = jnp.concatenate(outs, axis=0).T


def _nsa(q, kse, kw0, vt, kc0, vct, gates_t, ovt):
    b, s, _ = q.shape
    nc = kc0.shape[2]
    seq3 = lambda i, j: (i, 0, 0)
    seq = lambda i, j: (i, 0, 0, 0)
    seq5 = lambda i, j: (i, 0, 0, 0, 0)
    tile = lambda i, j: (i, j, 0)
    fixed = lambda i, j: (0, 0)
    return pl.pallas_call(
        _nsa_kernel,
        grid=(b, s // TQ),
        in_specs=[pl.BlockSpec((1, TQ, N_HEADS * LANES), tile),
                  pl.BlockSpec((1, s, N_KV * LANES), seq3), pl.BlockSpec((1, s, N_KV * LANES), seq3),
                  pl.BlockSpec((1, s // KC, 2 * N_KV, LANES, KC), seq5),
                  pl.BlockSpec((1, N_KV, nc, LANES), seq), pl.BlockSpec((1, N_KV, HEAD_DIM, nc), seq),
                  pl.BlockSpec((1, 1, 4 * SUBLANES, TQ), lambda i, j: (i, j, 0, 0)),
                  pl.BlockSpec(ovt.shape, fixed)],
        out_specs=pl.BlockSpec((1, TQ, ATTN_W), tile),
        out_shape=jax.ShapeDtypeStruct((b, s, ATTN_W), F32),
        scratch_shapes=[pltpu.VMEM((SUBLANES, HPG * TQ), F32), pltpu.VMEM((LANES, HPG * TQ), F32)],
        compiler_params=_cparams(("parallel", "arbitrary")),
        name="nsa",
    )(q, kse, kw0, vt, kc0, vct, gates_t, ovt)


def _post_kernel(x_ref, conv_ref, halo_ref, attn_ref, convw_ref, gnc_ref, gna_ref, wout_ref, ln2_ref, wq_ref,
                 h1_ref, hn_ref, pq_ref, *, tiles_per_seq):
    i = pl.program_id(0)
    c_h = conv_ref[:, 0:CONV_CH]
    c_b = conv_ref[:, CONV_CH:2 * CONV_CH]
    c_c = conv_ref[:, 2 * CONV_CH:3 * CONV_CH]
    z = c_c * c_h
    keep = jnp.where(i % tiles_per_seq == 0, 0.0, 1.0)
    zp = halo_ref[:, 2 * CONV_CH:3 * CONV_CH] * halo_ref[:, 0:CONV_CH] * keep
    rows = lax.broadcasted_iota(jnp.int32, z.shape, 0)
    n = z.shape[0]
    z1 = jnp.where(rows == 0, zp[SUBLANES - 1:SUBLANES, :], pltpu.roll(z, 1, 0))
    z2 = jnp.where(rows == 0, zp[SUBLANES - 2:SUBLANES - 1, :],
                   jnp.where(rows == 1, zp[SUBLANES - 1:SUBLANES, :], pltpu.roll(z, 2, 0)))
    conv = convw_ref[0:1, :] * z2 + convw_ref[1:2, :] * z1 + convw_ref[2:3, :] * z
    nc = _rms(c_b * conv, gnc_ref[...]).astype(BF16)
    na = _rms(attn_ref[...], gna_ref[...]).astype(BF16)
    h1 = x_ref[...] + _dot(nc, wout_ref[0:CONV_CH, :]) + _dot(na, wout_ref[CONV_CH:CONV_CH + ATTN_W, :])
    h1_ref[...] = h1
    hn = _rms(h1, ln2_ref[...])
    hn_ref[...] = hn
    pq_ref[...] = _dot(hn.astype(BF16), wq_ref[...]).astype(BF16)


def _post(x2, conv, attn2, convw, gnc, gna, wout, ln2, wq, seq_len):
    t = x2.shape[0]
    row = lambda i: (i, 0)
    fixed = lambda i: (0, 0)
    halo = lambda i: (jnp.maximum(i * (T_PROJ // SUBLANES) - 1, 0), 0)
    nq = wq.shape[1]
    return pl.pallas_call(
        functools.partial(_post_kernel, tiles_per_seq=seq_len // T_PROJ),
        grid=(t // T_PROJ,),
        in_specs=[pl.BlockSpec((T_PROJ, D_MODEL), row),
                  pl.BlockSpec((T_PROJ, 3 * CONV_CH), row),
                  pl.BlockSpec((SUBLANES, 3 * CONV_CH), halo),
                  pl.BlockSpec((T_PROJ, ATTN_W), row),
                  pl.BlockSpec(convw.shape, fixed), pl.BlockSpec(gnc.shape, fixed), pl.BlockSpec(gna.shape, fixed),
                  pl.BlockSpec(wout.shape, fixed), pl.BlockSpec(ln2.shape, fixed), pl.BlockSpec(wq.shape, fixed)],
        out_specs=[pl.BlockSpec((T_PROJ, D_MODEL), row), pl.BlockSpec((T_PROJ, D_MODEL), row),
                   pl.BlockSpec((T_PROJ, nq), row)],
        out_shape=[jax.ShapeDtypeStruct((t, D_MODEL), F32), jax.ShapeDtypeStruct((t, D_MODEL), F32),
                   jax.ShapeDtypeStruct((t, nq), BF16)],
        compiler_params=_cparams(("parallel",)),
        name="post",
    )(x2, conv, conv, attn2, convw, gnc, gna, wout, ln2, wq)


def _staircase():
    return [(a, b) for a in range(PEER_TOPK) for b in range(PEER_TOPK) if (a + 1) * (b + 1) <= PEER_TOPK]


N_CAND = 56


def _topk_kernel(pq_ref, sk_ref, e_ref, g_ref, sv_ref, si_ref, cand_ref, ce_ref, et_ref, gt_ref):
    tt = pq_ref.shape[0]
    rown = lax.broadcasted_iota(jnp.int32, (PEER_NKEYS, tt), 0).astype(F32)
    rowc = lax.broadcasted_iota(jnp.int32, (N_CAND, tt), 0).astype(F32)
    pairs = _staircase()
    for h in range(PEER_HEADS):
        for c in range(2):
            off = (h * 2 + c) * PEER_NKEYS
            x = _dot_nt(sk_ref[h, c], pq_ref[:, off:off + PEER_NKEYS])
            for it in range(PEER_TOPK):
                m = jnp.max(x, axis=0, keepdims=True)
                idx = jnp.min(jnp.where(x == m, rown, float(PEER_NKEYS)), axis=0, keepdims=True)
                sv_ref[c, it:it + 1, :] = m
                si_ref[c, it:it + 1, :] = idx
                x = jnp.where(rown == idx, -jnp.inf, x)
        cand_ref[...] = jnp.full((N_CAND, tt), -jnp.inf, F32)
        ce_ref[...] = jnp.zeros((N_CAND, tt), F32)
        for r, (a, b) in enumerate(pairs):
            cand_ref[r:r + 1, :] = sv_ref[0, a:a + 1, :] + sv_ref[1, b:b + 1, :]
            ce_ref[r:r + 1, :] = si_ref[0, a:a + 1, :] * float(PEER_NKEYS) + si_ref[1, b:b + 1, :]
        x = cand_ref[...]
        ce = ce_ref[...]
        best = []
        for it in range(PEER_TOPK):
            m = jnp.max(x, axis=0, keepdims=True)
            idx = jnp.min(jnp.where(x == m, rowc, float(N_CAND)), axis=0, keepdims=True)
            hit = rowc == idx
            et_ref[h * PEER_TOPK + it:h * PEER_TOPK + it + 1, :] = jnp.max(jnp.where(hit, ce, -1.0), axis=0, keepdims=True)
            best.append(m)
            x = jnp.where(hit, -jnp.inf, x)
        ex = [jnp.exp(v - best[0]) for v in best]
        tot = ex[0]
        for v in ex[1:]:
            tot = tot + v
        inv = 1.0 / tot
        for it in range(PEER_TOPK):
            gt_ref[h * PEER_TOPK + it:h * PEER_TOPK + it + 1, :] = ex[it] * inv
    e_ref[...] = (et_ref[...] * float(ROWS_PER_EXPERT)).astype(jnp.int32).T
    g_ref[...] = gt_ref[...].T


def _topk(pq, sk):
    t, nq = pq.shape
    nk = PEER_HEADS * PEER_TOPK
    row = lambda i: (i, 0)
    return pl.pallas_call(
        _topk_kernel,
        grid=(t // T_TOPK,),
        in_specs=[pl.BlockSpec((T_TOPK, nq), row), pl.BlockSpec(sk.shape, lambda i: (0, 0, 0, 0))],
        out_specs=[pl.BlockSpec((T_TOPK, nk), row), pl.BlockSpec((T_TOPK, nk), row)],
        out_shape=[jax.ShapeDtypeStruct((t, nk), jnp.int32), jax.ShapeDtypeStruct((t, nk), F32)],
        scratch_shapes=[pltpu.VMEM((2, PEER_TOPK, T_TOPK), F32), pltpu.VMEM((2, PEER_TOPK, T_TOPK), F32),
                        pltpu.VMEM((N_CAND, T_TOPK), F32), pltpu.VMEM((N_CAND, T_TOPK), F32),
                        pltpu.VMEM((nk, T_TOPK), F32), pltpu.VMEM((nk, T_TOPK), F32)],
        compiler_params=_cparams(("parallel",)),
        name="topk",
    )(pq, sk)


def _gather_rows(e_ref, tab_ref, t):
    rows = []
    for k in range(e_ref.shape[1]):
        e0 = pl.multiple_of(e_ref[t, k], ROWS_PER_EXPERT)
        rows.append(tab_ref[pl.ds(e0, ROWS_PER_EXPERT), :])
    return pltpu.bitcast(jnp.concatenate(rows, axis=0), BF16)


def _for_each_token(e_hbm, e_refs, sem, tab_ref, compute):
    i = pl.program_id(0)
    half = e_refs[0].shape[0]

    def copy(step, part):
        rows = pl.ds(pl.multiple_of((2 * step + part) * half, half), half)
        return pltpu.make_async_copy(e_hbm.at[rows], e_refs[part], sem.at[part])

    @pl.when(i == 0)
    def _():
        copy(0, 0).start()

    copy(i, 1).start()
    copy(i, 0).wait()
    w_next = _gather_rows(e_refs[0], tab_ref, 0)
    for t in range(2 * half):
        w_cur = w_next
        if t + 1 == half:
            @pl.when(i + 1 < pl.num_programs(0))
            def _():
                copy(i + 1, 0).start()

            copy(i, 1).wait()
        if t + 1 < 2 * half:
            w_next = _gather_rows(e_refs[(t + 1) // half], tab_ref, (t + 1) % half)
        compute(t, w_cur)


def _diag16(n):
    lane = lax.broadcasted_iota(jnp.int32, (2 * SUBLANES, n), 1)
    row = lax.broadcasted_iota(jnp.int32, (2 * SUBLANES, n), 0)
    return (lane & (SUBLANES - 1)) == (row & (SUBLANES - 1)), row < SUBLANES


def _peer_u_kernel(e_hbm, tab_ref, x_ref, gate_ref, g8_ref, a_ref, e0_ref, e1_ref, sem, hs_ref):
    diag, _ = _diag16(SUBLANES * e0_ref.shape[1])
    top = lax.broadcasted_iota(jnp.int32, (2 * SUBLANES, LANES), 0) < SUBLANES

    def token(t, w):
        parts = [x_ref[t:t + 1, i * LANES:(i + 1) * LANES] for i in range(SUBLANES)]
        xx = jnp.concatenate(parts + parts, axis=0)
        hi = xx.astype(BF16).astype(F32)
        x16 = jnp.where(top, hi, xx - hi).astype(BF16)
        r = _dot_nt(x16, w)
        hs_ref[t:t + 1, :] = jnp.sum(jnp.where(diag, r, 0.0), axis=0, keepdims=True)

    _for_each_token(e_hbm, (e0_ref, e1_ref), sem, tab_ref, token)
    h = jnp.dot(hs_ref[...], g8_ref[...], preferred_element_type=F32, precision=lax.Precision.HIGHEST)
    a_ref[...] = gate_ref[...] * _gelu(h)


def _peer_u(eidx, tab, hn, gate, g8):
    t, nk = eidx.shape
    row = lambda i: (i, 0)
    fixed = lambda i: (0, 0)
    return pl.pallas_call(
        _peer_u_kernel,
        grid=(t // T_PEER,),
        in_specs=[pl.BlockSpec(memory_space=pl.ANY),
                  pl.BlockSpec(tab.shape, fixed, pipeline_mode=pl.Buffered(1)),
                  pl.BlockSpec((T_PEER, D_MODEL), row),
                  pl.BlockSpec((T_PEER, nk), row),
                  pl.BlockSpec(g8.shape, fixed)],
        out_specs=pl.BlockSpec((T_PEER, nk), row),
        out_shape=jax.ShapeDtypeStruct((t, nk), F32),
        scratch_shapes=[pltpu.SMEM((T_PEER // 2, nk), jnp.int32), pltpu.SMEM((T_PEER // 2, nk), jnp.int32),
                        pltpu.SemaphoreType.DMA((2,)),
                        pltpu.VMEM((T_PEER, SUBLANES * nk), F32)],
        compiler_params=_cparams(("arbitrary",)),
        name="peer_u",
    )(eidx, tab, hn, gate, g8)


def _peer_v_kernel(e_hbm, tab_ref, a_ref, rep_ref, h1_ref, lnf_ref, o_ref, e0_ref, e1_ref, sem, arep_ref, po_ref, *,
                   last_layer):
    n = SUBLANES * e0_ref.shape[1]
    diag, top = _diag16(n)
    arep_ref[...] = jnp.dot(a_ref[...], rep_ref[...], preferred_element_type=F32, precision=lax.Precision.HIGHEST)

    def token(t, w):
        l32 = jnp.where(diag, jnp.broadcast_to(arep_ref[t:t + 1, :], (2 * SUBLANES, n)), 0.0)
        hi = l32.astype(BF16).astype(F32)
        lhs = jnp.where(top, hi, l32 - hi).astype(BF16)
        out = _dot(lhs, w)
        out = out[0:SUBLANES] + out[SUBLANES:2 * SUBLANES]
        for i in range(SUBLANES):
            po_ref[t:t + 1, i * LANES:(i + 1) * LANES] = out[i:i + 1, :]

    _for_each_token(e_hbm, (e0_ref, e1_ref), sem, tab_ref, token)
    h = h1_ref[...] + po_ref[...]
    o_ref[...] = _rms(h, lnf_ref[...]) if last_layer else h


def _peer_v(eidx, tab, a, rep, h1, lnf, last_layer):
    t, nk = eidx.shape
    row = lambda i: (i, 0)
    fixed = lambda i: (0, 0)
    return pl.pallas_call(
        functools.partial(_peer_v_kernel, last_layer=last_layer),
        grid=(t // T_PEER,),
        in_specs=[pl.BlockSpec(memory_space=pl.ANY),
                  pl.BlockSpec(tab.shape, fixed, pipeline_mode=pl.Buffered(1)),
                  pl.BlockSpec((T_PEER, nk), row),
                  pl.BlockSpec(rep.shape, fixed),
                  pl.BlockSpec((T_PEER, D_MODEL), row),
                  pl.BlockSpec((1, D_MODEL), fixed)],
        out_specs=pl.BlockSpec((T_PEER, D_MODEL), row),
        out_shape=jax.ShapeDtypeStruct((t, D_MODEL), F32),
        scratch_shapes=[pltpu.SMEM((T_PEER // 2, nk), jnp.int32), pltpu.SMEM((T_PEER // 2, nk), jnp.int32),
                        pltpu.SemaphoreType.DMA((2,)),
                        pltpu.VMEM((T_PEER, SUBLANES * nk), F32),
                        pltpu.VMEM((T_PEER, D_MODEL), F32)],
        compiler_params=_cparams(("arbitrary",)),
        name="peer_v",
    )(eidx, tab, a, rep, h1, lnf)


def _pack_rows(a):
    r, n, _ = a.shape
    bits = lax.bitcast_convert_type(a, jnp.uint16).astype(jnp.uint32).reshape(r, n // 2, 2, LANES)
    return (bits[:, :, 0] | (bits[:, :, 1] << 16)).reshape(r * n // 2, LANES)


def _pad_lanes(a, left):
    z = jnp.zeros_like(a)
    return jnp.concatenate([a, z] if left else [z, a], axis=-1)


def _layer(h, l, ln1, w_in, conv_w, cmp_pos_k, cmp_pos_v, cmp_k_w1, cmp_k_w2, cmp_v_w1, cmp_v_w2,
           gn_conv, gn_attn, w_out, ln2, peer_wq, peer_subkeys, peer_u, peer_v, ln_f, last_layer):
    b, s, _ = h.shape
    t = b * s
    x2 = h.reshape(t, D_MODEL)
    w = w_in[l]
    o_q = 3 * CONV_CH
    o_kv = o_q + ATTN_W
    o_g = o_kv + 6 * N_KV * HEAD_DIM
    kvw = N_KV * HEAD_DIM
    part = lambda i: w[:, o_kv + i * kvw:o_kv + (i + 1) * kvw]
    pad_heads = lambda a, n: _pad_lanes(a.reshape(D_MODEL, n, HEAD_DIM), True).reshape(D_MODEL, n * LANES)
    w_cat = jnp.concatenate([w[:, :o_q], pad_heads(w[:, o_q:o_kv], N_HEADS), part(0), part(1),
                             pad_heads(part(2), N_KV), pad_heads(part(4), N_KV)], axis=1).astype(BF16)
    w_t = jnp.concatenate([pad_heads(part(3), N_KV).T, pad_heads(part(5), N_KV).T,
                           jnp.pad(w[:, o_g:], ((0, 0), (0, 4 * SUBLANES - N_GATES))).T], axis=0).astype(BF16)
    conv, q, kcvc, kse, kw0, vt, gates_t = _inproj(x2, ln1[l][None, :], w_cat, w_t, s)

    nc = s // CMP_STRIDE

    def chunks(a):
        a = a.reshape(b, nc, CMP_STRIDE, N_KV, HEAD_DIM).transpose(0, 3, 1, 2, 4)
        return a.reshape(b, N_KV, nc, CMP_STRIDE * HEAD_DIM)

    pos2 = lambda p: p.reshape(2, CMP_STRIDE * HEAD_DIM)
    kcc, vcc = _compress(chunks(kcvc[:, :LANES]), chunks(kcvc[:, LANES:]), pos2(cmp_pos_k[l]), pos2(cmp_pos_v[l]),
                         cmp_k_w1[l].astype(BF16), cmp_k_w2[l].astype(BF16),
                         cmp_v_w1[l].astype(BF16), cmp_v_w2[l].astype(BF16))

    attn = _attention(b, s, q, kse, kw0, vt, kcc, vcc, gates_t)

    h1, hn, pq = _post(x2, conv, attn.reshape(t, ATTN_W), conv_w[l], gn_conv[l][None, :], gn_attn[l][None, :],
                           w_out[l].astype(BF16), ln2[l][None, :], peer_wq[l].astype(BF16), s)

    eidx, gate = _topk(pq, peer_subkeys[l].astype(BF16))
    nk = PEER_HEADS * PEER_TOPK
    g8 = (jnp.arange(SUBLANES * nk)[:, None] // SUBLANES == jnp.arange(nk)[None, :]).astype(F32)
    table = lambda w: _pack_rows(w.astype(BF16).reshape(PEER_EXPERTS, SUBLANES, LANES))
    a = _peer_u(eidx, table(peer_u[l]), hn, gate, g8)
    return _peer_v(eidx, table(peer_v[l]), a, g8.T, h1, ln_f[None, :], last_layer).reshape(b, s, D_MODEL)


def _attention(b, s, q, kse, kw0, vt, kcc, vcc, gates_t):
    nc = s // CMP_STRIDE
    n_sel = s // SEL_BLOCK
    kc0 = _pad_lanes(kcc, True)
    vct = vcc.transpose(0, 1, 3, 2)
    kse, kw0 = kse.reshape(b, s, N_KV * LANES), kw0.reshape(b, s, N_KV * LANES)
    vt = vt.reshape(b, s // KC, 2 * N_KV, LANES, KC)
    gates_t = gates_t.reshape(b, s // TQ, 4 * SUBLANES, TQ)
    n_cmp = (s - CMP_BLOCK) // CMP_STRIDE + 1
    cs = np.arange(nc) * CMP_STRIDE
    ss = np.arange(HEAD_DIM) * SEL_BLOCK
    ov = ((cs[:, None] < ss[None, :] + SEL_BLOCK) & (cs[:, None] + CMP_BLOCK > ss[None, :])
          & (np.arange(nc)[:, None] < n_cmp) & (np.arange(HEAD_DIM)[None, :] < n_sel))
    ovt = jnp.asarray(ov.T.astype(np.float32))
    return _nsa(q.reshape(b, s, N_HEADS * LANES), kse, kw0, vt, kc0, vct, gates_t, ovt)


def kernel(x, ln1, w_in, conv_w, cmp_pos_k, cmp_pos_v, cmp_k_w1, cmp_k_w2, cmp_v_w1, cmp_v_w2, gn_conv, gn_attn,
           w_out, ln2, peer_wq, peer_subkeys, peer_u, peer_v, ln_f):
    b, s, _ = x.shape
    depth = w_in.shape[0]
    h = x
    for l in range(depth):
        h = _layer(h, l, ln1, w_in, conv_w, cmp_pos_k, cmp_pos_v, cmp_k_w1, cmp_k_w2, cmp_v_w1, cmp_v_w2,
                   gn_conv, gn_attn, w_out, ln2, peer_wq, peer_subkeys, peer_u, peer_v, ln_f, l + 1 == depth)
    return h
```

```python
import functools
import math

import jax
import jax.numpy as jnp
import numpy as np
from jax import lax
from jax.experimental import pallas as pl
from jax.experimental.pallas import tpu as pltpu

F32 = jnp.float32
BF16 = jnp.bfloat16

D_MODEL = 1024
CONV_CH = 512
N_HEADS = 8
HEAD_DIM = 64
N_KV = 2
HPG = N_HEADS // N_KV
ATTN_W = N_HEADS * HEAD_DIM
CMP_BLOCK = 32
CMP_STRIDE = 16
SEL_BLOCK = 64
SEL_TOP = 16
WINDOW = 512
N_GATES = 3 * N_HEADS
PEER_HEADS = 8
PEER_NKEYS = 128
PEER_TOPK = 16
EPS = 1e-6
NEG_INF = -1e30
FORCE = 1e4

LANES = 128
SUBLANES = 8
VMEM_LIMIT = 56 * 1024 * 1024

TQ = 256
KC = 256
SEL_UNROLL = 4
T_PROJ = 256
T_TOPK = 256
T_PEER = 256
assert T_PROJ == KC == TQ
ROWS_PER_EXPERT = D_MODEL // 2 // LANES


def _cparams(sem):
    return pltpu.CompilerParams(dimension_semantics=sem, vmem_limit_bytes=VMEM_LIMIT)


def _dot_nt(a, b, precision=None):
    return lax.dot_general(a, b, (((1,), (1,)), ((), ())), preferred_element_type=F32, precision=precision)


def _dot(a, b):
    return jnp.dot(a, b, preferred_element_type=F32)


def _rms(x, g):
    return x * lax.rsqrt(jnp.mean(x * x, axis=-1, keepdims=True) + EPS) * g


def _gelu(x):
    c = math.sqrt(2.0 / math.pi)
    return 0.5 * x * (1.0 + jnp.tanh(c * (x + 0.044715 * (x * x * x))))


def _inproj_kernel(x_ref, ln1_ref, w_ref, wt_ref, conv_ref, q_ref, kcvc_ref, kse_ref, kw0_ref, vt_ref, gt_ref, *,
                   tiles_per_seq):
    xn = _rms(x_ref[...], ln1_ref[...]).astype(BF16)
    n_conv, n_q, n_k = 3 * CONV_CH, N_HEADS * LANES, N_KV * LANES
    o = 0
    conv_ref[...] = _dot(xn, w_ref[:, o:o + n_conv])
    o += n_conv
    q_ref[...] = (_dot(xn, w_ref[:, o:o + n_q]) * (HEAD_DIM ** -0.5)).astype(BF16)
    o += n_q
    kcvc_ref[...] = _dot(xn, w_ref[:, o:o + n_k])
    o += n_k
    pos = (pl.program_id(0) % tiles_per_seq) * T_PROJ + lax.broadcasted_iota(jnp.int32, (T_PROJ, n_k), 0)
    lane = lax.broadcasted_iota(jnp.int32, (T_PROJ, n_k), 1) & (LANES - 1)
    onehot = (lane - HEAD_DIM) == jnp.right_shift(pos, int(math.log2(SEL_BLOCK)))
    kse_ref[...] = jnp.where(onehot, 1.0, _dot(xn, w_ref[:, o:o + n_k])).astype(BF16)
    o += n_k
    kw0_ref[...] = _dot(xn, w_ref[:, o:o + n_k]).astype(BF16)
    ones_rows = lax.broadcasted_iota(jnp.int32, (LANES, T_PROJ), 0) >= HEAD_DIM
    for j in range(2 * N_KV):
        vt = _dot_nt(wt_ref[j * LANES:(j + 1) * LANES, :], xn)
        vt_ref[0, j] = jnp.where(ones_rows, 1.0, vt).astype(BF16)
    r0 = 2 * N_KV * LANES
    gt_ref[0] = jax.nn.sigmoid(_dot_nt(wt_ref[r0:r0 + 4 * SUBLANES, :], xn))


def _inproj(x2, ln1, w_cat, w_t, seq_len):
    t = x2.shape[0]
    row = lambda i: (i, 0)
    fixed = lambda i: (0, 0)
    n_k = N_KV * LANES
    return pl.pallas_call(
        functools.partial(_inproj_kernel, tiles_per_seq=seq_len // T_PROJ),
        grid=(t // T_PROJ,),
        in_specs=[pl.BlockSpec((T_PROJ, D_MODEL), row),
                  pl.BlockSpec((1, D_MODEL), fixed),
                  pl.BlockSpec(w_cat.shape, fixed),
                  pl.BlockSpec(w_t.shape, fixed)],
        out_specs=[pl.BlockSpec((T_PROJ, 3 * CONV_CH), row),
                   pl.BlockSpec((T_PROJ, N_HEADS * LANES), row),
                   pl.BlockSpec((T_PROJ, n_k), row),
                   pl.BlockSpec((T_PROJ, n_k), row),
                   pl.BlockSpec((T_PROJ, n_k), row),
                   pl.BlockSpec((1, 2 * N_KV, LANES, T_PROJ), lambda i: (i, 0, 0, 0)),
                   pl.BlockSpec((1, 4 * SUBLANES, T_PROJ), lambda i: (i, 0, 0))],
        out_shape=[jax.ShapeDtypeStruct((t, 3 * CONV_CH), F32),
                   jax.ShapeDtypeStruct((t, N_HEADS * LANES), BF16),
                   jax.ShapeDtypeStruct((t, n_k), F32),
                   jax.ShapeDtypeStruct((t, n_k), BF16),
                   jax.ShapeDtypeStruct((t, n_k), BF16),
                   jax.ShapeDtypeStruct((t // T_PROJ, 2 * N_KV, LANES, T_PROJ), BF16),
                   jax.ShapeDtypeStruct((t // T_PROJ, 4 * SUBLANES, T_PROJ), F32)],
        compiler_params=_cparams(("parallel",)),
        name="inproj",
    )(x2, ln1, w_cat, w_t)


def _compress_kernel(ck_ref, cv_ref, posk_ref, posv_ref, w1k_ref, w2k_ref, w1v_ref, w2v_ref, ok_ref, ov_ref):
    half = CMP_STRIDE * HEAD_DIM
    for c_ref, pos_ref, w1_ref, w2_ref, o_ref in ((ck_ref, posk_ref, w1k_ref, w2k_ref, ok_ref),
                                                  (cv_ref, posv_ref, w1v_ref, w2v_ref, ov_ref)):
        for g in range(N_KV):
            c = c_ref[0, g]
            nc = c.shape[0]
            a = _dot((c + pos_ref[0:1, :]).astype(BF16), w1_ref[0:half, :])
            b = _dot((c + pos_ref[1:2, :]).astype(BF16), w1_ref[half:2 * half, :])
            hid = a + pltpu.roll(b, nc - 1, 0)
            out = _dot(_gelu(hid).astype(BF16), w2_ref[...])
            rows = lax.broadcasted_iota(jnp.int32, out.shape, 0)
            o_ref[0, g] = jnp.where(rows < nc - 1, out, 0.0).astype(BF16)


def _compress(ck, cv, posk, posv, w1k, w2k, w1v, w2v):
    b, g, nc, cw = ck.shape
    blk = lambda i: (i, 0, 0, 0)
    fixed = lambda i: (0, 0)
    return pl.pallas_call(
        _compress_kernel,
        grid=(b,),
        in_specs=[pl.BlockSpec((1, g, nc, cw), blk), pl.BlockSpec((1, g, nc, cw), blk),
                  pl.BlockSpec(posk.shape, fixed), pl.BlockSpec(posv.shape, fixed),
                  pl.BlockSpec(w1k.shape, fixed), pl.BlockSpec(w2k.shape, fixed),
                  pl.BlockSpec(w1v.shape, fixed), pl.BlockSpec(w2v.shape, fixed)],
        out_specs=[pl.BlockSpec((1, g, nc, HEAD_DIM), blk), pl.BlockSpec((1, g, nc, HEAD_DIM), blk)],
        out_shape=[jax.ShapeDtypeStruct((b, g, nc, HEAD_DIM), BF16)] * 2,
        compiler_params=_cparams(("parallel",)),
        name="compress",
    )(ck, cv, posk, posv, w1k, w2k, w1v, w2v)


def _nsa_kernel(q_ref, kse_ref, kw0_ref, vt_ref, kc0_ref, vct_ref, gt_ref, ovt_ref, o_ref, m_ref, acc_ref):
    qt = pl.program_id(1)
    t0 = qt * TQ
    nc = kc0_ref.shape[2]
    n_sel = ovt_ref.shape[0]
    w = HPG * TQ
    tq = t0 + (lax.broadcasted_iota(jnp.int32, (1, w), 1) & (TQ - 1))
    krow = lax.broadcasted_iota(jnp.int32, (KC, 1), 0)
    causal = (t0 + krow) <= tq
    band = (t0 - 2 * KC + krow) > (tq - WINDOW)
    eye = (lax.broadcasted_iota(jnp.int32, (TQ, TQ), 0)
           == lax.broadcasted_iota(jnp.int32, (TQ, TQ), 1)).astype(BF16)
    nrow = lax.broadcasted_iota(jnp.int32, (nc, 1), 0)
    valid_c = ((nrow * CMP_STRIDE + (CMP_BLOCK - 1)) <= tq) & (nrow < nc - 1)
    jrow = lax.broadcasted_iota(jnp.int32, (n_sel, TQ), 0)
    qblk = jnp.right_shift(t0 + lax.broadcasted_iota(jnp.int32, (n_sel, TQ), 1), int(math.log2(SEL_BLOCK)))
    forced = (jrow == 0) | (jrow == qblk) | (jrow == qblk - 1)
    lane128 = lax.broadcasted_iota(jnp.int32, (TQ, LANES), 1)
    gt = gt_ref[0, 0]

    def online(s, vt):
        m_old = m_ref[0:1, :]
        m_new = jnp.maximum(m_old, jnp.max(s, axis=0, keepdims=True))
        alpha = jnp.exp(m_old - m_new)
        p = jnp.exp(s - m_new)
        acc_ref[...] = alpha * acc_ref[...] + _dot(vt, p.astype(BF16))
        m_ref[...] = jnp.broadcast_to(m_new, (SUBLANES, w))

    def first(s, vt):
        m = jnp.max(s, axis=0, keepdims=True)
        acc_ref[...] = _dot(vt, jnp.exp(s - m).astype(BF16))
        m_ref[...] = jnp.broadcast_to(m, (SUBLANES, w))

    def finish():
        a = acc_ref[...]
        return a[0:HEAD_DIM, :] / a[HEAD_DIM:HEAD_DIM + 1, :]

    for g in range(N_KV):
        kc0 = kc0_ref[0, g]
        q_all = jnp.concatenate([q_ref[0, :, (g * HPG + h) * LANES:(g * HPG + h + 1) * LANES]
                                 for h in range(HPG)], axis=0)
        s = jnp.where(valid_c, _dot_nt(kc0, q_all), NEG_INF)
        m = jnp.max(s, axis=0, keepdims=True)
        p = jnp.where(valid_c, jnp.exp(s - m), 0.0)
        l = jnp.sum(p, axis=0, keepdims=True)
        pn = p * (1.0 / jnp.where(l > 0.0, l, 1.0))
        o_c = _dot(vct_ref[0, g], pn.astype(BF16))
        psum = pn[:, 0:TQ]
        for h in range(1, HPG):
            psum = psum + pn[:, h * TQ:(h + 1) * TQ]
        imp_t = jnp.dot(ovt_ref[...], psum, preferred_element_type=F32,
                        precision=lax.Precision.HIGHEST)
        val = jnp.where(jrow > qblk, -FORCE, imp_t + jnp.where(forced, FORCE, 0.0))
        rank = jnp.zeros((n_sel, TQ), jnp.int32)
        for k in range(n_sel):
            vk = val[k:k + 1, :]
            ahead = (vk > val) | ((vk == val) & (jrow > k))
            rank = rank + ahead.astype(jnp.int32)
        sel_t = (rank < SEL_TOP).astype(BF16)
        pad_t = jnp.concatenate([jnp.zeros((LANES - n_sel, TQ), BF16), sel_t], axis=0)
        sel_q = _dot_nt(eye, pad_t)
        bias = jnp.where((lane128 >= HEAD_DIM) & (sel_q < 0.5), NEG_INF, 0.0).astype(BF16)
        lhs = q_all + jnp.concatenate([bias] * HPG, axis=0)
        gl = slice(g * LANES, (g + 1) * LANES)
        kd = kse_ref[0, pl.ds(pl.multiple_of(t0, KC), KC), gl]
        first(jnp.where(causal, _dot_nt(kd, lhs), NEG_INF), vt_ref[0, qt, g])

        def scores(c):
            return _dot_nt(kse_ref[0, pl.ds(pl.multiple_of(c * KC, KC), KC), gl], lhs)

        def sel_body(j, carry):
            cs = [SEL_UNROLL * j + u for u in range(SEL_UNROLL)]
            ss = [scores(c) for c in cs]
            for c, s in zip(cs, ss):
                online(s, vt_ref[0, c, g])
            return carry

        def sel_tail(c, carry):
            online(scores(c), vt_ref[0, c, g])
            return carry

        n_main = qt // SEL_UNROLL
        lax.fori_loop(0, n_main, sel_body, 0)
        lax.fori_loop(n_main * SEL_UNROLL, qt, sel_tail, 0)
        o_s = finish()
        c1, c2 = jnp.maximum(qt - 1, 0), jnp.maximum(qt - 2, 0)
        wscores = lambda c: _dot_nt(kw0_ref[0, pl.ds(pl.multiple_of(c * KC, KC), KC), gl], q_all)
        s0 = jnp.where(causal, wscores(qt), NEG_INF)
        s1 = jnp.where(qt >= 1, wscores(c1), NEG_INF)
        s2 = jnp.where(band & (qt >= 2), wscores(c2), NEG_INF)
        first(s0, vt_ref[0, qt, N_KV + g])
        online(s1, vt_ref[0, c1, N_KV + g])
        online(s2, vt_ref[0, c2, N_KV + g])
        o_w = finish()
        outs = []
        for h in range(HPG):
            r = 3 * (g * HPG + h)
            c0, c1 = h * TQ, (h + 1) * TQ
            outs.append(gt[r:r + 1, :] * o_c[:, c0:c1] + gt[r + 1:r + 2, :] * o_s[:, c0:c1]
                        + gt[r + 2:r + 3, :] * o_w[:, c0:c1])
        o_ref[0, :, g * HPG * HEAD_DIM:(g + 1) * HPG * HEAD_DIM] = jnp.concatenate(outs, axis=0).T


def _nsa(q, kse, kw0, vt, kc0, vct, gates_t, ovt):
    b, s, _ = q.shape
    nc = kc0.shape[2]
    seq3 = lambda i, j: (i, 0, 0)
    seq = lambda i, j: (i, 0, 0, 0)
    seq5 = lambda i, j: (i, 0, 0, 0, 0)
    tile = lambda i, j: (i, j, 0)
    fixed = lambda i, j: (0, 0)
    return pl.pallas_call(
        _nsa_kernel,
        grid=(b, s // TQ),
        in_specs=[pl.BlockSpec((1, TQ, N_HEADS * LANES), tile),
                  pl.BlockSpec((1, s, N_KV * LANES), seq3), pl.BlockSpec((1, s, N_KV * LANES), seq3),
                  pl.BlockSpec((1, s // KC, 2 * N_KV, LANES, KC), seq5),
                  pl.BlockSpec((1, N_KV, nc, LANES), seq), pl.BlockSpec((1, N_KV, HEAD_DIM, nc), seq),
                  pl.BlockSpec((1, 1, 4 * SUBLANES, TQ), lambda i, j: (i, j, 0, 0)),
                  pl.BlockSpec(ovt.shape, fixed)],
        out_specs=pl.BlockSpec((1, TQ, ATTN_W), tile),
        out_shape=jax.ShapeDtypeStruct((b, s, ATTN_W), F32),
        scratch_shapes=[pltpu.VMEM((SUBLANES, HPG * TQ), F32), pltpu.VMEM((LANES, HPG * TQ), F32)],
        compiler_params=_cparams(("parallel", "arbitrary")),
        name="nsa",
    )(q, kse, kw0, vt, kc0, vct, gates_t, ovt)


def _post_kernel(x_ref, conv_ref, halo_ref, attn_ref, convw_ref, gnc_ref, gna_ref, wout_ref, ln2_ref, wq_ref,
                 h1_ref, hn_ref, pq_ref, *, tiles_per_seq):
    i = pl.program_id(0)
    c_h = conv_ref[:, 0:CONV_CH]
    c_b = conv_ref[:, CONV_CH:2 * CONV_CH]
    c_c = conv_ref[:, 2 * CONV_CH:3 * CONV_CH]
    z = c_c * c_h
    keep = jnp.where(i % tiles_per_seq == 0, 0.0, 1.0)
    zp = halo_ref[:, 2 * CONV_CH:3 * CONV_CH] * halo_ref[:, 0:CONV_CH] * keep
    rows = lax.broadcasted_iota(jnp.int32, z.shape, 0)
    n = z.shape[0]
    z1 = jnp.where(rows == 0, zp[SUBLANES - 1:SUBLANES, :], pltpu.roll(z, 1, 0))
    z2 = jnp.where(rows == 0, zp[SUBLANES - 2:SUBLANES - 1, :],
                   jnp.where(rows == 1, zp[SUBLANES - 1:SUBLANES, :], pltpu.roll(z, 2, 0)))
    conv = convw_ref[0:1, :] * z2 + convw_ref[1:2, :] * z1 + convw_ref[2:3, :] * z
    nc = _rms(c_b * conv, gnc_ref[...]).astype(BF16)
    na = _rms(attn_ref[...], gna_ref[...]).astype(BF16)
    h1 = x_ref[...] + _dot(nc, wout_ref[0:CONV_CH, :]) + _dot(na, wout_ref[CONV_CH:CONV_CH + ATTN_W, :])
    h1_ref[...] = h1
    hn = _rms(h1, ln2_ref[...])
    hn_ref[...] = hn
    pq_ref[...] = _dot(hn.astype(BF16), wq_ref[...]).astype(BF16)


def _post(x2, conv, attn2, convw, gnc, gna, wout, ln2, wq, seq_len):
    t = x2.shape[0]
    row = lambda i: (i, 0)
    fixed = lambda i: (0, 0)
    halo = lambda i: (jnp.maximum(i * (T_PROJ // SUBLANES) - 1, 0), 0)
    nq = wq.shape[1]
    return pl.pallas_call(
        functools.partial(_post_kernel, tiles_per_seq=seq_len // T_PROJ),
        grid=(t // T_PROJ,),
        in_specs=[pl.BlockSpec((T_PROJ, D_MODEL), row),
                  pl.BlockSpec((T_PROJ, 3 * CONV_CH), row),
                  pl.BlockSpec((SUBLANES, 3 * CONV_CH), halo),
                  pl.BlockSpec((T_PROJ, ATTN_W), row),
                  pl.BlockSpec(convw.shape, fixed), pl.BlockSpec(gnc.shape, fixed), pl.BlockSpec(gna.shape, fixed),
                  pl.BlockSpec(wout.shape, fixed), pl.BlockSpec(ln2.shape, fixed), pl.BlockSpec(wq.shape, fixed)],
        out_specs=[pl.BlockSpec((T_PROJ, D_MODEL), row), pl.BlockSpec((T_PROJ, D_MODEL), row),
                   pl.BlockSpec((T_PROJ, nq), row)],
        out_shape=[jax.ShapeDtypeStruct((t, D_MODEL), F32), jax.ShapeDtypeStruct((t, D_MODEL), F32),
                   jax.ShapeDtypeStruct((t, nq), BF16)],
        compiler_params=_cparams(("parallel",)),
        name="post",
    )(x2, conv, conv, attn2, convw, gnc, gna, wout, ln2, wq)


def _staircase():
    return [(a, b) for a in range(PEER_TOPK) for b in range(PEER_TOPK) if (a + 1) * (b + 1) <= PEER_TOPK]


N_CAND = 56


def _topk_kernel(pq_ref, sk_ref, e_ref, g_ref, sv_ref, si_ref, cand_ref, ce_ref, et_ref, gt_ref):
    tt = pq_ref.shape[0]
    rown = lax.broadcasted_iota(jnp.int32, (PEER_NKEYS, tt), 0).astype(F32)
    rowc = lax.broadcasted_iota(jnp.int32, (N_CAND, tt), 0).astype(F32)
    pairs = _staircase()
    for h in range(PEER_HEADS):
        for c in range(2):
            off = (h * 2 + c) * PEER_NKEYS
            x = _dot_nt(sk_ref[h, c], pq_ref[:, off:off + PEER_NKEYS])
            for it in range(PEER_TOPK):
                m = jnp.max(x, axis=0, keepdims=True)
                idx = jnp.min(jnp.where(x == m, rown, float(PEER_NKEYS)), axis=0, keepdims=True)
                sv_ref[c, it:it + 1, :] = m
                si_ref[c, it:it + 1, :] = idx
                x = jnp.where(rown == idx, -jnp.inf, x)
        cand_ref[...] = jnp.full((N_CAND, tt), -jnp.inf, F32)
        ce_ref[...] = jnp.zeros((N_CAND, tt), F32)
        for r, (a, b) in enumerate(pairs):
            cand_ref[r:r + 1, :] = sv_ref[0, a:a + 1, :] + sv_ref[1, b:b + 1, :]
            ce_ref[r:r + 1, :] = si_ref[0, a:a + 1, :] * float(PEER_NKEYS) + si_ref[1, b:b + 1, :]
        x = cand_ref[...]
        ce = ce_ref[...]
        best = []
        for it in range(PEER_TOPK):
            m = jnp.max(x, axis=0, keepdims=True)
            idx = jnp.min(jnp.where(x == m, rowc, float(N_CAND)), axis=0, keepdims=True)
            hit = rowc == idx
            et_ref[h * PEER_TOPK + it:h * PEER_TOPK + it + 1, :] = jnp.max(jnp.where(hit, ce, -1.0), axis=0, keepdims=True)
            best.append(m)
            x = jnp.where(hit, -jnp.inf, x)
        ex = [jnp.exp(v - best[0]) for v in best]
        tot = ex[0]
        for v in ex[1:]:
            tot = tot + v
        inv = 1.0 / tot
        for it in range(PEER_TOPK):
            gt_ref[h * PEER_TOPK + it:h * PEER_TOPK + it + 1, :] = ex[it] * inv
    e_ref[...] = (et_ref[...] * float(ROWS_PER_EXPERT)).astype(jnp.int32).T
    g_ref[...] = gt_ref[...].T


def _topk(pq, sk):
    t, nq = pq.shape
    nk = PEER_HEADS * PEER_TOPK
    row = lambda i: (i, 0)
    return pl.pallas_call(
        _topk_kernel,
        grid=(t // T_TOPK,),
        in_specs=[pl.BlockSpec((T_TOPK, nq), row), pl.BlockSpec(sk.shape, lambda i: (0, 0, 0, 0))],
        out_specs=[pl.BlockSpec((T_TOPK, nk), row), pl.BlockSpec((T_TOPK, nk), row)],
        out_shape=[jax.ShapeDtypeStruct((t, nk), jnp.int32), jax.ShapeDtypeStruct((t, nk), F32)],
        scratch_shapes=[pltpu.VMEM((2, PEER_TOPK, T_TOPK), F32), pltpu.VMEM((2, PEER_TOPK, T_TOPK), F32),
                        pltpu.VMEM((N_CAND, T_TOPK), F32), pltpu.VMEM((N_CAND, T_TOPK), F32),
                        pltpu.VMEM((nk, T_TOPK), F32), pltpu.VMEM((nk, T_TOPK), F32)],
        compiler_params=_cparams(("parallel",)),
        name="topk",
    )(pq, sk)


def _gather_rows(e_ref, tab_ref, t):
    rows = []
    for k in range(e_ref.shape[1]):
        e0 = pl.multiple_of(e_ref[t, k], ROWS_PER_EXPERT)
        rows.append(tab_ref[pl.ds(e0, ROWS_PER_EXPERT), :])
    return pltpu.bitcast(jnp.concatenate(rows, axis=0), BF16)


def _for_each_token(e_hbm, e_refs, sem, tab_ref, compute):
    i = pl.program_id(0)
    half = e_refs[0].shape[0]

    def copy(step, part):
        rows = pl.ds(pl.multiple_of((2 * step + part) * half, half), half)
        return pltpu.make_async_copy(e_hbm.at[rows], e_refs[part], sem.at[part])

    @pl.when(i == 0)
    def _():
        copy(0, 0).start()

    copy(i, 1).start()
    copy(i, 0).wait()
    w_next = _gather_rows(e_refs[0], tab_ref, 0)
    for t in range(2 * half):
        w_cur = w_next
        if t + 1 == half:
            @pl.when(i + 1 < pl.num_programs(0))
            def _():
                copy(i + 1, 0).start()

            copy(i, 1).wait()
        if t + 1 < 2 * half:
            w_next = _gather_rows(e_refs[(t + 1) // half], tab_ref, (t + 1) % half)
        compute(t, w_cur)


def _diag16(n):
    lane = lax.broadcasted_iota(jnp.int32, (2 * SUBLANES, n), 1)
    row = lax.broadcasted_iota(jnp.int32, (2 * SUBLANES, n), 0)
    return (lane & (SUBLANES - 1)) == (row & (SUBLANES - 1)), row < SUBLANES


def _peer_u_kernel(e_hbm, tab_ref, x_ref, gate_ref, g8_ref, a_ref, e0_ref, e1_ref, sem, hs_ref):
    diag, _ = _diag16(SUBLANES * e0_ref.shape[1])
    top = lax.broadcasted_iota(jnp.int32, (2 * SUBLANES, LANES), 0) < SUBLANES

    def token(t, w):
        parts = [x_ref[t:t + 1, i * LANES:(i + 1) * LANES] for i in range(SUBLANES)]
        xx = jnp.concatenate(parts + parts, axis=0)
        hi = xx.astype(BF16).astype(F32)
        x16 = jnp.where(top, hi, xx - hi).astype(BF16)
        r = _dot_nt(x16, w)
        hs_ref[t:t + 1, :] = jnp.sum(jnp.where(diag, r, 0.0), axis=0, keepdims=True)

    _for_each_token(e_hbm, (e0_ref, e1_ref), sem, tab_ref, token)
    h = jnp.dot(hs_ref[...], g8_ref[...], preferred_element_type=F32, precision=lax.Precision.HIGHEST)
    a_ref[...] = gate_ref[...] * _gelu(h)


def _peer_u(eidx, tab, hn, gate, g8):
    t, nk = eidx.shape
    row = lambda i: (i, 0)
    fixed = lambda i: (0, 0)
    return pl.pallas_call(
        _peer_u_kernel,
        grid=(t // T_PEER,),
        in_specs=[pl.BlockSpec(memory_space=pl.ANY),
                  pl.BlockSpec(tab.shape, fixed, pipeline_mode=pl.Buffered(1)),
                  pl.BlockSpec((T_PEER, D_MODEL), row),
                  pl.BlockSpec((T_PEER, nk), row),
                  pl.BlockSpec(g8.shape, fixed)],
        out_specs=pl.BlockSpec((T_PEER, nk), row),
        out_shape=jax.ShapeDtypeStruct((t, nk), F32),
        scratch_shapes=[pltpu.SMEM((T_PEER // 2, nk), jnp.int32), pltpu.SMEM((T_PEER // 2, nk), jnp.int32),
                        pltpu.SemaphoreType.DMA((2,)),
                        pltpu.VMEM((T_PEER, SUBLANES * nk), F32)],
        compiler_params=_cparams(("arbitrary",)),
        name="peer_u",
    )(eidx, tab, hn, gate, g8)


def _peer_v_kernel(e_hbm, tab_ref, a_ref, rep_ref, h1_ref, lnf_ref, o_ref, e0_ref, e1_ref, sem, arep_ref, po_ref, *,
                   last_layer):
    n = SUBLANES * e0_ref.shape[1]
    diag, top = _diag16(n)
    arep_ref[...] = jnp.dot(a_ref[...], rep_ref[...], preferred_element_type=F32, precision=lax.Precision.HIGHEST)

    def token(t, w):
        l32 = jnp.where(diag, jnp.broadcast_to(arep_ref[t:t + 1, :], (2 * SUBLANES, n)), 0.0)
        hi = l32.astype(BF16).astype(F32)
        lhs = jnp.where(top, hi, l32 - hi).astype(BF16)
        out = _dot(lhs, w)
        out = out[0:SUBLANES] + out[SUBLANES:2 * SUBLANES]
        for i in range(SUBLANES):
            po_ref[t:t + 1, i * LANES:(i + 1) * LANES] = out[i:i + 1, :]

    _for_each_token(e_hbm, (e0_ref, e1_ref), sem, tab_ref, token)
    h = h1_ref[...] + po_ref[...]
    o_ref[...] = _rms(h, lnf_ref[...]) if last_layer else h


def _peer_v(eidx, tab, a, rep, h1, lnf, last_layer):
    t, nk = eidx.shape
    row = lambda i: (i, 0)
    fixed = lambda i: (0, 0)
    return pl.pallas_call(
        functools.partial(_peer_v_kernel, last_layer=last_layer),
        grid=(t // T_PEER,),
        in_specs=[pl.BlockSpec(memory_space=pl.ANY),
                  pl.BlockSpec(tab.shape, fixed, pipeline_mode=pl.Buffered(1)),
                  pl.BlockSpec((T_PEER, nk), row),
                  pl.BlockSpec(rep.shape, fixed),
                  pl.BlockSpec((T_PEER, D_MODEL), row),
                  pl.BlockSpec((1, D_MODEL), fixed)],
        out_specs=pl.BlockSpec((T_PEER, D_MODEL), row),
        out_shape=jax.ShapeDtypeStruct((t, D_MODEL), F32),
        scratch_shapes=[pltpu.SMEM((T_PEER // 2, nk), jnp.int32), pltpu.SMEM((T_PEER // 2, nk), jnp.int32),
                        pltpu.SemaphoreType.DMA((2,)),
                        pltpu.VMEM((T_PEER, SUBLANES * nk), F32),
                        pltpu.VMEM((T_PEER, D_MODEL), F32)],
        compiler_params=_cparams(("arbitrary",)),
        name="peer_v",
    )(eidx, tab, a, rep, h1, lnf)


def _pack_table(w):
    bits = lax.bitcast_convert_type(w.astype(BF16), jnp.uint16).astype(jnp.uint32)
    sub = lambda i: bits[:, i * LANES:(i + 1) * LANES]
    words = jnp.stack([sub(2 * r) | (sub(2 * r + 1) << 16) for r in range(ROWS_PER_EXPERT)], axis=1)
    return words.reshape(w.shape[0] * ROWS_PER_EXPERT, LANES)


def _pad_lanes(a, left):
    z = jnp.zeros_like(a)
    return jnp.concatenate([a, z] if left else [z, a], axis=-1)


def _layer(h, l, ln1, w_in, conv_w, cmp_pos_k, cmp_pos_v, cmp_k_w1, cmp_k_w2, cmp_v_w1, cmp_v_w2,
           gn_conv, gn_attn, w_out, ln2, peer_wq, peer_subkeys, peer_u, peer_v, ln_f, last_layer):
    b, s, _ = h.shape
    t = b * s
    x2 = h.reshape(t, D_MODEL)
    w = w_in[l]
    o_q = 3 * CONV_CH
    o_kv = o_q + ATTN_W
    o_g = o_kv + 6 * N_KV * HEAD_DIM
    kvw = N_KV * HEAD_DIM
    part = lambda i: w[:, o_kv + i * kvw:o_kv + (i + 1) * kvw]
    pad_heads = lambda a, n: _pad_lanes(a.reshape(D_MODEL, n, HEAD_DIM), True).reshape(D_MODEL, n * LANES)
    w_cat = jnp.concatenate([w[:, :o_q], pad_heads(w[:, o_q:o_kv], N_HEADS), part(0), part(1),
                             pad_heads(part(2), N_KV), pad_heads(part(4), N_KV)], axis=1).astype(BF16)
    w_t = jnp.concatenate([pad_heads(part(3), N_KV).T, pad_heads(part(5), N_KV).T,
                           jnp.pad(w[:, o_g:], ((0, 0), (0, 4 * SUBLANES - N_GATES))).T], axis=0).astype(BF16)
    conv, q, kcvc, kse, kw0, vt, gates_t = _inproj(x2, ln1[l][None, :], w_cat, w_t, s)

    nc = s // CMP_STRIDE

    def chunks(a):
        a = a.reshape(b, nc, CMP_STRIDE, N_KV, HEAD_DIM).transpose(0, 3, 1, 2, 4)
        return a.reshape(b, N_KV, nc, CMP_STRIDE * HEAD_DIM)

    pos2 = lambda p: p.reshape(2, CMP_STRIDE * HEAD_DIM)
    kcc, vcc = _compress(chunks(kcvc[:, :LANES]), chunks(kcvc[:, LANES:]), pos2(cmp_pos_k[l]), pos2(cmp_pos_v[l]),
                         cmp_k_w1[l].astype(BF16), cmp_k_w2[l].astype(BF16),
                         cmp_v_w1[l].astype(BF16), cmp_v_w2[l].astype(BF16))

    attn = _attention(b, s, q, kse, kw0, vt, kcc, vcc, gates_t)

    h1, hn, pq = _post(x2, conv, attn.reshape(t, ATTN_W), conv_w[l], gn_conv[l][None, :], gn_attn[l][None, :],
                           w_out[l].astype(BF16), ln2[l][None, :], peer_wq[l].astype(BF16), s)

    eidx, gate = _topk(pq, peer_subkeys[l].astype(BF16))
    nk = PEER_HEADS * PEER_TOPK
    g8 = (jnp.arange(SUBLANES * nk)[:, None] // SUBLANES == jnp.arange(nk)[None, :]).astype(F32)
    a = _peer_u(eidx, _pack_table(peer_u[l]), hn, gate, g8)
    out = _peer_v(eidx, _pack_table(peer_v[l]), a, g8.T, h1, ln_f[None, :], last_layer)
    return out.reshape(b, s, D_MODEL)


def _attention(b, s, q, kse, kw0, vt, kcc, vcc, gates_t):
    nc = s // CMP_STRIDE
    n_sel = s // SEL_BLOCK
    kc0 = _pad_lanes(kcc, True)
    vct = vcc.transpose(0, 1, 3, 2)
    kse, kw0 = kse.reshape(b, s, N_KV * LANES), kw0.reshape(b, s, N_KV * LANES)
    vt = vt.reshape(b, s // KC, 2 * N_KV, LANES, KC)
    gates_t = gates_t.reshape(b, s // TQ, 4 * SUBLANES, TQ)
    n_cmp = (s - CMP_BLOCK) // CMP_STRIDE + 1
    cs = np.arange(nc) * CMP_STRIDE
    ss = np.arange(HEAD_DIM) * SEL_BLOCK
    ov = ((cs[:, None] < ss[None, :] + SEL_BLOCK) & (cs[:, None] + CMP_BLOCK > ss[None, :])
          & (np.arange(nc)[:, None] < n_cmp) & (np.arange(HEAD_DIM)[None, :] < n_sel))
    ovt = jnp.asarray(ov.T.astype(np.float32))
    return _nsa(q.reshape(b, s, N_HEADS * LANES), kse, kw0, vt, kc0, vct, gates_t, ovt)


def kernel(x, ln1, w_in, conv_w, cmp_pos_k, cmp_pos_v, cmp_k_w1, cmp_k_w2, cmp_v_w1, cmp_v_w2, gn_conv, gn_attn,
           w_out, ln2, peer_wq, peer_subkeys, peer_u, peer_v, ln_f):
    b, s, _ = x.shape
    depth = w_in.shape[0]
    h = x
    for l in range(depth):
        h = _layer(h, l, ln1, w_in, conv_w, cmp_pos_k, cmp_pos_v, cmp_k_w1, cmp_k_w2, cmp_v_w1, cmp_v_w2,
                   gn_conv, gn_attn, w_out, ln2, peer_wq, peer_subkeys, peer_u, peer_v, ln_f, l + 1 == depth)
    return h
```

```python
import functools
import math

import jax
import jax.numpy as jnp
import numpy as np
from jax import lax
from jax.experimental import pallas as pl
from jax.experimental.pallas import tpu as pltpu

F32 = jnp.float32
BF16 = jnp.bfloat16

D_MODEL = 1024
CONV_CH = 512
N_HEADS = 8
HEAD_DIM = 64
N_KV = 2
HPG = N_HEADS // N_KV
ATTN_W = N_HEADS * HEAD_DIM
CMP_BLOCK = 32
CMP_STRIDE = 16
SEL_BLOCK = 64
SEL_TOP = 16
WINDOW = 512
N_GATES = 3 * N_HEADS
PEER_HEADS = 8
PEER_NKEYS = 128
PEER_TOPK = 16
EPS = 1e-6
NEG_INF = -1e30
FORCE = 1e4

LANES = 128
SUBLANES = 8
VMEM_LIMIT = 56 * 1024 * 1024

TQ = 256
KC = 256
SEL_UNROLL = 4
T_PROJ = 256
T_TOPK = 512
T_PEER = 256
assert T_PROJ == KC == TQ
ROWS_PER_EXPERT = D_MODEL // 2 // LANES


def _cparams(sem):
    return pltpu.CompilerParams(dimension_semantics=sem, vmem_limit_bytes=VMEM_LIMIT)


def _dot_nt(a, b, precision=None):
    return lax.dot_general(a, b, (((1,), (1,)), ((), ())), preferred_element_type=F32, precision=precision)


def _dot(a, b):
    return jnp.dot(a, b, preferred_element_type=F32)


def _rms(x, g):
    return x * lax.rsqrt(jnp.mean(x * x, axis=-1, keepdims=True) + EPS) * g


def _gelu(x):
    c = math.sqrt(2.0 / math.pi)
    return 0.5 * x * (1.0 + jnp.tanh(c * (x + 0.044715 * (x * x * x))))


def _inproj_kernel(x_ref, ln1_ref, w_ref, wt_ref, conv_ref, q_ref, kcvc_ref, kse_ref, kw0_ref, vt_ref, gt_ref, *,
                   tiles_per_seq):
    xn = _rms(x_ref[...], ln1_ref[...]).astype(BF16)
    n_conv, n_q, n_k = 3 * CONV_CH, N_HEADS * LANES, N_KV * LANES
    o = 0
    conv_ref[...] = _dot(xn, w_ref[:, o:o + n_conv])
    o += n_conv
    q_ref[...] = (_dot(xn, w_ref[:, o:o + n_q]) * (HEAD_DIM ** -0.5)).astype(BF16)
    o += n_q
    kcvc_ref[...] = _dot(xn, w_ref[:, o:o + n_k])
    o += n_k
    pos = (pl.program_id(0) % tiles_per_seq) * T_PROJ + lax.broadcasted_iota(jnp.int32, (T_PROJ, n_k), 0)
    lane = lax.broadcasted_iota(jnp.int32, (T_PROJ, n_k), 1) & (LANES - 1)
    onehot = (lane - HEAD_DIM) == jnp.right_shift(pos, int(math.log2(SEL_BLOCK)))
    kse_ref[...] = jnp.where(onehot, 1.0, _dot(xn, w_ref[:, o:o + n_k])).astype(BF16)
    o += n_k
    kw0_ref[...] = _dot(xn, w_ref[:, o:o + n_k]).astype(BF16)
    ones_rows = lax.broadcasted_iota(jnp.int32, (LANES, T_PROJ), 0) >= HEAD_DIM
    for j in range(2 * N_KV):
        vt = _dot_nt(wt_ref[j * LANES:(j + 1) * LANES, :], xn)
        vt_ref[0, j] = jnp.where(ones_rows, 1.0, vt).astype(BF16)
    r0 = 2 * N_KV * LANES
    gt_ref[0] = jax.nn.sigmoid(_dot_nt(wt_ref[r0:r0 + 4 * SUBLANES, :], xn))


def _inproj(x2, ln1, w_cat, w_t, seq_len):
    t = x2.shape[0]
    row = lambda i: (i, 0)
    fixed = lambda i: (0, 0)
    n_k = N_KV * LANES
    return pl.pallas_call(
        functools.partial(_inproj_kernel, tiles_per_seq=seq_len // T_PROJ),
        grid=(t // T_PROJ,),
        in_specs=[pl.BlockSpec((T_PROJ, D_MODEL), row),
                  pl.BlockSpec((1, D_MODEL), fixed),
                  pl.BlockSpec(w_cat.shape, fixed),
                  pl.BlockSpec(w_t.shape, fixed)],
        out_specs=[pl.BlockSpec((T_PROJ, 3 * CONV_CH), row),
                   pl.BlockSpec((T_PROJ, N_HEADS * LANES), row),
                   pl.BlockSpec((T_PROJ, n_k), row),
                   pl.BlockSpec((T_PROJ, n_k), row),
                   pl.BlockSpec((T_PROJ, n_k), row),
                   pl.BlockSpec((1, 2 * N_KV, LANES, T_PROJ), lambda i: (i, 0, 0, 0)),
                   pl.BlockSpec((1, 4 * SUBLANES, T_PROJ), lambda i: (i, 0, 0))],
        out_shape=[jax.ShapeDtypeStruct((t, 3 * CONV_CH), F32),
                   jax.ShapeDtypeStruct((t, N_HEADS * LANES), BF16),
                   jax.ShapeDtypeStruct((t, n_k), F32),
                   jax.ShapeDtypeStruct((t, n_k), BF16),
                   jax.ShapeDtypeStruct((t, n_k), BF16),
                   jax.ShapeDtypeStruct((t // T_PROJ, 2 * N_KV, LANES, T_PROJ), BF16),
                   jax.ShapeDtypeStruct((t // T_PROJ, 4 * SUBLANES, T_PROJ), F32)],
        compiler_params=_cparams(("parallel",)),
        name="inproj",
    )(x2, ln1, w_cat, w_t)


def _compress_kernel(ck_ref, cv_ref, posk_ref, posv_ref, w1k_ref, w2k_ref, w1v_ref, w2v_ref, ok_ref, ov_ref):
    half = CMP_STRIDE * HEAD_DIM
    for c_ref, pos_ref, w1_ref, w2_ref, o_ref in ((ck_ref, posk_ref, w1k_ref, w2k_ref, ok_ref),
                                                  (cv_ref, posv_ref, w1v_ref, w2v_ref, ov_ref)):
        for g in range(N_KV):
            c = c_ref[0, g]
            nc = c.shape[0]
            a = _dot((c + pos_ref[0:1, :]).astype(BF16), w1_ref[0:half, :])
            b = _dot((c + pos_ref[1:2, :]).astype(BF16), w1_ref[half:2 * half, :])
            hid = a + pltpu.roll(b, nc - 1, 0)
            out = _dot(_gelu(hid).astype(BF16), w2_ref[...])
            rows = lax.broadcasted_iota(jnp.int32, out.shape, 0)
            o_ref[0, g] = jnp.where(rows < nc - 1, out, 0.0).astype(BF16)


def _compress(ck, cv, posk, posv, w1k, w2k, w1v, w2v):
    b, g, nc, cw = ck.shape
    blk = lambda i: (i, 0, 0, 0)
    fixed = lambda i: (0, 0)
    return pl.pallas_call(
        _compress_kernel,
        grid=(b,),
        in_specs=[pl.BlockSpec((1, g, nc, cw), blk), pl.BlockSpec((1, g, nc, cw), blk),
                  pl.BlockSpec(posk.shape, fixed), pl.BlockSpec(posv.shape, fixed),
                  pl.BlockSpec(w1k.shape, fixed), pl.BlockSpec(w2k.shape, fixed),
                  pl.BlockSpec(w1v.shape, fixed), pl.BlockSpec(w2v.shape, fixed)],
        out_specs=[pl.BlockSpec((1, g, nc, HEAD_DIM), blk), pl.BlockSpec((1, g, nc, HEAD_DIM), blk)],
        out_shape=[jax.ShapeDtypeStruct((b, g, nc, HEAD_DIM), BF16)] * 2,
        compiler_params=_cparams(("parallel",)),
        name="compress",
    )(ck, cv, posk, posv, w1k, w2k, w1v, w2v)


def _nsa_kernel(q_ref, kse_ref, kw0_ref, vt_ref, kc0_ref, vct_ref, gt_ref, ovt_ref, o_ref, m_ref, acc_ref):
    qt = pl.program_id(1)
    t0 = qt * TQ
    nc = kc0_ref.shape[2]
    n_sel = ovt_ref.shape[0]
    w = HPG * TQ
    tq = t0 + (lax.broadcasted_iota(jnp.int32, (1, w), 1) & (TQ - 1))
    krow = lax.broadcasted_iota(jnp.int32, (KC, 1), 0)
    causal = (t0 + krow) <= tq
    band = (t0 - 2 * KC + krow) > (tq - WINDOW)
    eye = (lax.broadcasted_iota(jnp.int32, (TQ, TQ), 0)
           == lax.broadcasted_iota(jnp.int32, (TQ, TQ), 1)).astype(BF16)
    nrow = lax.broadcasted_iota(jnp.int32, (nc, 1), 0)
    valid_c = ((nrow * CMP_STRIDE + (CMP_BLOCK - 1)) <= tq) & (nrow < nc - 1)
    jrow = lax.broadcasted_iota(jnp.int32, (n_sel, TQ), 0)
    qblk = jnp.right_shift(t0 + lax.broadcasted_iota(jnp.int32, (n_sel, TQ), 1), int(math.log2(SEL_BLOCK)))
    forced = (jrow == 0) | (jrow == qblk) | (jrow == qblk - 1)
    lane128 = lax.broadcasted_iota(jnp.int32, (TQ, LANES), 1)
    gt = gt_ref[0, 0]

    def online(s, vt):
        m_old = m_ref[0:1, :]
        m_new = jnp.maximum(m_old, jnp.max(s, axis=0, keepdims=True))
        alpha = jnp.exp(m_old - m_new)
        p = jnp.exp(s - m_new)
        acc_ref[...] = alpha * acc_ref[...] + _dot(vt, p.astype(BF16))
        m_ref[...] = jnp.broadcast_to(m_new, (SUBLANES, w))

    def first(s, vt):
        m = jnp.max(s, axis=0, keepdims=True)
        acc_ref[...] = _dot(vt, jnp.exp(s - m).astype(BF16))
        m_ref[...] = jnp.broadcast_to(m, (SUBLANES, w))

    def finish():
        a = acc_ref[...]
        return a[0:HEAD_DIM, :] / a[HEAD_DIM:HEAD_DIM + 1, :]

    for g in range(N_KV):
        kc0 = kc0_ref[0, g]
        q_all = jnp.concatenate([q_ref[0, :, (g * HPG + h) * LANES:(g * HPG + h + 1) * LANES]
                                 for h in range(HPG)], axis=0)
        s = jnp.where(valid_c, _dot_nt(kc0, q_all), NEG_INF)
        m = jnp.max(s, axis=0, keepdims=True)
        p = jnp.where(valid_c, jnp.exp(s - m), 0.0)
        l = jnp.sum(p, axis=0, keepdims=True)
        pn = p * (1.0 / jnp.where(l > 0.0, l, 1.0))
        o_c = _dot(vct_ref[0, g], pn.astype(BF16))
        psum = pn[:, 0:TQ]
        for h in range(1, HPG):
            psum = psum + pn[:, h * TQ:(h + 1) * TQ]
        imp_t = jnp.dot(ovt_ref[...], psum, preferred_element_type=F32,
                        precision=lax.Precision.HIGHEST)
        val = jnp.where(jrow > qblk, -FORCE, imp_t + jnp.where(forced, FORCE, 0.0))
        rank = jnp.zeros((n_sel, TQ), jnp.int32)
        for k in range(n_sel):
            vk = val[k:k + 1, :]
            ahead = (vk > val) | ((vk == val) & (jrow > k))
            rank = rank + ahead.astype(jnp.int32)
        sel_t = (rank < SEL_TOP).astype(BF16)
        pad_t = jnp.concatenate([jnp.zeros((LANES - n_sel, TQ), BF16), sel_t], axis=0)
        sel_q = _dot_nt(eye, pad_t)
        bias = jnp.where((lane128 >= HEAD_DIM) & (sel_q < 0.5), NEG_INF, 0.0).astype(BF16)
        lhs = q_all + jnp.concatenate([bias] * HPG, axis=0)
        gl = slice(g * LANES, (g + 1) * LANES)
        kd = kse_ref[0, pl.ds(pl.multiple_of(t0, KC), KC), gl]
        first(jnp.where(causal, _dot_nt(kd, lhs), NEG_INF), vt_ref[0, qt, g])

        def scores(c):
            return _dot_nt(kse_ref[0, pl.ds(pl.multiple_of(c * KC, KC), KC), gl], lhs)

        def sel_body(j, carry):
            cs = [SEL_UNROLL * j + u for u in range(SEL_UNROLL)]
            ss = [scores(c) for c in cs]
            for c, s in zip(cs, ss):
                online(s, vt_ref[0, c, g])
            return carry

        def sel_tail(c, carry):
            online(scores(c), vt_ref[0, c, g])
            return carry

        n_main = qt // SEL_UNROLL
        lax.fori_loop(0, n_main, sel_body, 0)
        lax.fori_loop(n_main * SEL_UNROLL, qt, sel_tail, 0)
        o_s = finish()
        c1, c2 = jnp.maximum(qt - 1, 0), jnp.maximum(qt - 2, 0)
        wscores = lambda c: _dot_nt(kw0_ref[0, pl.ds(pl.multiple_of(c * KC, KC), KC), gl], q_all)
        s0 = jnp.where(causal, wscores(qt), NEG_INF)
        s1 = jnp.where(qt >= 1, wscores(c1), NEG_INF)
        s2 = jnp.where(band & (qt >= 2), wscores(c2), NEG_INF)
        first(s0, vt_ref[0, qt, N_KV + g])
        online(s1, vt_ref[0, c1, N_KV + g])
        online(s2, vt_ref[0, c2, N_KV + g])
        o_w = finish()
        outs = []
        for h in range(HPG):
            r = 3 * (g * HPG + h)
            c0, c1 = h * TQ, (h + 1) * TQ
            outs.append(gt[r:r + 1, :] * o_c[:, c0:c1] + gt[r + 1:r + 2, :] * o_s[:, c0:c1]
                        + gt[r + 2:r + 3, :] * o_w[:, c0:c1])
        o_ref[0, :, g * HPG * HEAD_DIM:(g + 1) * HPG * HEAD_DIM] = jnp.concatenate(outs, axis=0).T


def _nsa(q, kse, kw0, vt, kc0, vct, gates_t, ovt):
    b, s, _ = q.shape
    nc = kc0.shape[2]
    seq3 = lambda i, j: (i, 0, 0)
    seq = lambda i, j: (i, 0, 0, 0)
    seq5 = lambda i, j: (i, 0, 0, 0, 0)
    tile = lambda i, j: (i, j, 0)
    fixed = lambda i, j: (0, 0)
    return pl.pallas_call(
        _nsa_kernel,
        grid=(b, s // TQ),
        in_specs=[pl.BlockSpec((1, TQ, N_HEADS * LANES), tile),
                  pl.BlockSpec((1, s, N_KV * LANES), seq3), pl.BlockSpec((1, s, N_KV * LANES), seq3),
                  pl.BlockSpec((1, s // KC, 2 * N_KV, LANES, KC), seq5),
                  pl.BlockSpec((1, N_KV, nc, LANES), seq), pl.BlockSpec((1, N_KV, HEAD_DIM, nc), seq),
                  pl.BlockSpec((1, 1, 4 * SUBLANES, TQ), lambda i, j: (i, j, 0, 0)),
                  pl.BlockSpec(ovt.shape, fixed)],
        out_specs=pl.BlockSpec((1, TQ, ATTN_W), tile),
        out_shape=jax.ShapeDtypeStruct((b, s, ATTN_W), F32),
        scratch_shapes=[pltpu.VMEM((SUBLANES, HPG * TQ), F32), pltpu.VMEM((LANES, HPG * TQ), F32)],
        compiler_params=_cparams(("parallel", "arbitrary")),
        name="nsa",
    )(q, kse, kw0, vt, kc0, vct, gates_t, ovt)


def _post_kernel(x_ref, conv_ref, halo_ref, attn_ref, convw_ref, gnc_ref, gna_ref, wout_ref, ln2_ref, wq_ref,
                 h1_ref, hn_ref, pq_ref, *, tiles_per_seq):
    i = pl.program_id(0)
    c_h = conv_ref[:, 0:CONV_CH]
    c_b = conv_ref[:, CONV_CH:2 * CONV_CH]
    c_c = conv_ref[:, 2 * CONV_CH:3 * CONV_CH]
    z = c_c * c_h
    keep = jnp.where(i % tiles_per_seq == 0, 0.0, 1.0)
    zp = halo_ref[:, 2 * CONV_CH:3 * CONV_CH] * halo_ref[:, 0:CONV_CH] * keep
    rows = lax.broadcasted_iota(jnp.int32, z.shape, 0)
    n = z.shape[0]
    z1 = jnp.where(rows == 0, zp[SUBLANES - 1:SUBLANES, :], pltpu.roll(z, 1, 0))
    z2 = jnp.where(rows == 0, zp[SUBLANES - 2:SUBLANES - 1, :],
                   jnp.where(rows == 1, zp[SUBLANES - 1:SUBLANES, :], pltpu.roll(z, 2, 0)))
    conv = convw_ref[0:1, :] * z2 + convw_ref[1:2, :] * z1 + convw_ref[2:3, :] * z
    nc = _rms(c_b * conv, gnc_ref[...]).astype(BF16)
    na = _rms(attn_ref[...], gna_ref[...]).astype(BF16)
    h1 = x_ref[...] + _dot(nc, wout_ref[0:CONV_CH, :]) + _dot(na, wout_ref[CONV_CH:CONV_CH + ATTN_W, :])
    h1_ref[...] = h1
    hn = _rms(h1, ln2_ref[...])
    hn_ref[...] = hn
    pq_ref[...] = _dot(hn.astype(BF16), wq_ref[...]).astype(BF16)


def _post(x2, conv, attn2, convw, gnc, gna, wout, ln2, wq, seq_len):
    t = x2.shape[0]
    row = lambda i: (i, 0)
    fixed = lambda i: (0, 0)
    halo = lambda i: (jnp.maximum(i * (T_PROJ // SUBLANES) - 1, 0), 0)
    nq = wq.shape[1]
    return pl.pallas_call(
        functools.partial(_post_kernel, tiles_per_seq=seq_len // T_PROJ),
        grid=(t // T_PROJ,),
        in_specs=[pl.BlockSpec((T_PROJ, D_MODEL), row),
                  pl.BlockSpec((T_PROJ, 3 * CONV_CH), row),
                  pl.BlockSpec((SUBLANES, 3 * CONV_CH), halo),
                  pl.BlockSpec((T_PROJ, ATTN_W), row),
                  pl.BlockSpec(convw.shape, fixed), pl.BlockSpec(gnc.shape, fixed), pl.BlockSpec(gna.shape, fixed),
                  pl.BlockSpec(wout.shape, fixed), pl.BlockSpec(ln2.shape, fixed), pl.BlockSpec(wq.shape, fixed)],
        out_specs=[pl.BlockSpec((T_PROJ, D_MODEL), row), pl.BlockSpec((T_PROJ, D_MODEL), row),
                   pl.BlockSpec((T_PROJ, nq), row)],
        out_shape=[jax.ShapeDtypeStruct((t, D_MODEL), F32), jax.ShapeDtypeStruct((t, D_MODEL), F32),
                   jax.ShapeDtypeStruct((t, nq), BF16)],
        compiler_params=_cparams(("parallel",)),
        name="post",
    )(x2, conv, conv, attn2, convw, gnc, gna, wout, ln2, wq)


def _staircase():
    return [(a, b) for a in range(PEER_TOPK) for b in range(PEER_TOPK) if (a + 1) * (b + 1) <= PEER_TOPK]


N_CAND = 56


def _topk_kernel(pq_ref, sk_ref, e_ref, g_ref, sv_ref, si_ref, cand_ref, ce_ref, et_ref, gt_ref):
    tt = pq_ref.shape[0]
    rown = lax.broadcasted_iota(jnp.int32, (PEER_NKEYS, tt), 0).astype(F32)
    rowc = lax.broadcasted_iota(jnp.int32, (N_CAND, tt), 0).astype(F32)
    pairs = _staircase()
    for h in range(PEER_HEADS):
        for c in range(2):
            off = (h * 2 + c) * PEER_NKEYS
            x = _dot_nt(sk_ref[h, c], pq_ref[:, off:off + PEER_NKEYS])
            for it in range(PEER_TOPK):
                m = jnp.max(x, axis=0, keepdims=True)
                idx = jnp.min(jnp.where(x == m, rown, float(PEER_NKEYS)), axis=0, keepdims=True)
                sv_ref[c, it:it + 1, :] = m
                si_ref[c, it:it + 1, :] = idx
                x = jnp.where(rown == idx, -jnp.inf, x)
        cand_ref[...] = jnp.full((N_CAND, tt), -jnp.inf, F32)
        ce_ref[...] = jnp.zeros((N_CAND, tt), F32)
        for r, (a, b) in enumerate(pairs):
            cand_ref[r:r + 1, :] = sv_ref[0, a:a + 1, :] + sv_ref[1, b:b + 1, :]
            ce_ref[r:r + 1, :] = si_ref[0, a:a + 1, :] * float(PEER_NKEYS) + si_ref[1, b:b + 1, :]
        x = cand_ref[...]
        ce = ce_ref[...]
        best = []
        for it in range(PEER_TOPK):
            m = jnp.max(x, axis=0, keepdims=True)
            idx = jnp.min(jnp.where(x == m, rowc, float(N_CAND)), axis=0, keepdims=True)
            hit = rowc == idx
            et_ref[h * PEER_TOPK + it:h * PEER_TOPK + it + 1, :] = jnp.max(jnp.where(hit, ce, -1.0), axis=0, keepdims=True)
            best.append(m)
            x = jnp.where(hit, -jnp.inf, x)
        ex = [jnp.exp(v - best[0]) for v in best]
        tot = ex[0]
        for v in ex[1:]:
            tot = tot + v
        inv = 1.0 / tot
        for it in range(PEER_TOPK):
            gt_ref[h * PEER_TOPK + it:h * PEER_TOPK + it + 1, :] = ex[it] * inv
    e_ref[...] = (et_ref[...] * float(ROWS_PER_EXPERT)).astype(jnp.int32).T
    g_ref[...] = gt_ref[...].T


def _topk(pq, sk):
    t, nq = pq.shape
    nk = PEER_HEADS * PEER_TOPK
    row = lambda i: (i, 0)
    return pl.pallas_call(
        _topk_kernel,
        grid=(t // T_TOPK,),
        in_specs=[pl.BlockSpec((T_TOPK, nq), row), pl.BlockSpec(sk.shape, lambda i: (0, 0, 0, 0))],
        out_specs=[pl.BlockSpec((T_TOPK, nk), row), pl.BlockSpec((T_TOPK, nk), row)],
        out_shape=[jax.ShapeDtypeStruct((t, nk), jnp.int32), jax.ShapeDtypeStruct((t, nk), F32)],
        scratch_shapes=[pltpu.VMEM((2, PEER_TOPK, T_TOPK), F32), pltpu.VMEM((2, PEER_TOPK, T_TOPK), F32),
                        pltpu.VMEM((N_CAND, T_TOPK), F32), pltpu.VMEM((N_CAND, T_TOPK), F32),
                        pltpu.VMEM((nk, T_TOPK), F32), pltpu.VMEM((nk, T_TOPK), F32)],
        compiler_params=_cparams(("parallel",)),
        name="topk",
    )(pq, sk)


def _gather_rows(e_ref, tab_ref, t):
    rows = []
    for k in range(e_ref.shape[1]):
        e0 = pl.multiple_of(e_ref[t, k], ROWS_PER_EXPERT)
        rows.append(tab_ref[pl.ds(e0, ROWS_PER_EXPERT), :])
    return pltpu.bitcast(jnp.concatenate(rows, axis=0), BF16)


def _for_each_token(e_hbm, e_refs, sem, tab_ref, compute):
    i = pl.program_id(0)
    half = e_refs[0].shape[0]

    def copy(step, part):
        rows = pl.ds(pl.multiple_of((2 * step + part) * half, half), half)
        return pltpu.make_async_copy(e_hbm.at[rows], e_refs[part], sem.at[part])

    @pl.when(i == 0)
    def _():
        copy(0, 0).start()

    copy(i, 1).start()
    copy(i, 0).wait()
    w_next = _gather_rows(e_refs[0], tab_ref, 0)
    for t in range(2 * half):
        w_cur = w_next
        if t + 1 == half:
            @pl.when(i + 1 < pl.num_programs(0))
            def _():
                copy(i + 1, 0).start()

            copy(i, 1).wait()
        if t + 1 < 2 * half:
            w_next = _gather_rows(e_refs[(t + 1) // half], tab_ref, (t + 1) % half)
        compute(t, w_cur)


def _diag16(n):
    lane = lax.broadcasted_iota(jnp.int32, (2 * SUBLANES, n), 1)
    row = lax.broadcasted_iota(jnp.int32, (2 * SUBLANES, n), 0)
    return (lane & (SUBLANES - 1)) == (row & (SUBLANES - 1)), row < SUBLANES


def _peer_u_kernel(e_hbm, tab_ref, x_ref, gate_ref, g8_ref, a_ref, e0_ref, e1_ref, sem, hs_ref):
    diag, _ = _diag16(SUBLANES * e0_ref.shape[1])
    top = lax.broadcasted_iota(jnp.int32, (2 * SUBLANES, LANES), 0) < SUBLANES

    def token(t, w):
        parts = [x_ref[t:t + 1, i * LANES:(i + 1) * LANES] for i in range(SUBLANES)]
        xx = jnp.concatenate(parts + parts, axis=0)
        hi = xx.astype(BF16).astype(F32)
        x16 = jnp.where(top, hi, xx - hi).astype(BF16)
        r = _dot_nt(x16, w)
        hs_ref[t:t + 1, :] = jnp.sum(jnp.where(diag, r, 0.0), axis=0, keepdims=True)

    _for_each_token(e_hbm, (e0_ref, e1_ref), sem, tab_ref, token)
    h = jnp.dot(hs_ref[...], g8_ref[...], preferred_element_type=F32, precision=lax.Precision.HIGHEST)
    a_ref[...] = gate_ref[...] * _gelu(h)


def _peer_u(eidx, tab, hn, gate, g8):
    t, nk = eidx.shape
    row = lambda i: (i, 0)
    fixed = lambda i: (0, 0)
    return pl.pallas_call(
        _peer_u_kernel,
        grid=(t // T_PEER,),
        in_specs=[pl.BlockSpec(memory_space=pl.ANY),
                  pl.BlockSpec(tab.shape, fixed, pipeline_mode=pl.Buffered(1)),
                  pl.BlockSpec((T_PEER, D_MODEL), row),
                  pl.BlockSpec((T_PEER, nk), row),
                  pl.BlockSpec(g8.shape, fixed)],
        out_specs=pl.BlockSpec((T_PEER, nk), row),
        out_shape=jax.ShapeDtypeStruct((t, nk), F32),
        scratch_shapes=[pltpu.SMEM((T_PEER // 2, nk), jnp.int32), pltpu.SMEM((T_PEER // 2, nk), jnp.int32),
                        pltpu.SemaphoreType.DMA((2,)),
                        pltpu.VMEM((T_PEER, SUBLANES * nk), F32)],
        compiler_params=_cparams(("arbitrary",)),
        name="peer_u",
    )(eidx, tab, hn, gate, g8)


def _peer_v_kernel(e_hbm, tab_ref, a_ref, rep_ref, h1_ref, lnf_ref, o_ref, e0_ref, e1_ref, sem, arep_ref, po_ref, *,
                   last_layer):
    n = SUBLANES * e0_ref.shape[1]
    diag, top = _diag16(n)
    arep_ref[...] = jnp.dot(a_ref[...], rep_ref[...], preferred_element_type=F32, precision=lax.Precision.HIGHEST)

    def token(t, w):
        l32 = jnp.where(diag, jnp.broadcast_to(arep_ref[t:t + 1, :], (2 * SUBLANES, n)), 0.0)
        hi = l32.astype(BF16).astype(F32)
        lhs = jnp.where(top, hi, l32 - hi).astype(BF16)
        out = _dot(lhs, w)
        out = out[0:SUBLANES] + out[SUBLANES:2 * SUBLANES]
        for i in range(SUBLANES):
            po_ref[t:t + 1, i * LANES:(i + 1) * LANES] = out[i:i + 1, :]

    _for_each_token(e_hbm, (e0_ref, e1_ref), sem, tab_ref, token)
    h = h1_ref[...] + po_ref[...]
    o_ref[...] = _rms(h, lnf_ref[...]) if last_layer else h


def _peer_v(eidx, tab, a, rep, h1, lnf, last_layer):
    t, nk = eidx.shape
    row = lambda i: (i, 0)
    fixed = lambda i: (0, 0)
    return pl.pallas_call(
        functools.partial(_peer_v_kernel, last_layer=last_layer),
        grid=(t // T_PEER,),
        in_specs=[pl.BlockSpec(memory_space=pl.ANY),
                  pl.BlockSpec(tab.shape, fixed, pipeline_mode=pl.Buffered(1)),
                  pl.BlockSpec((T_PEER, nk), row),
                  pl.BlockSpec(rep.shape, fixed),
                  pl.BlockSpec((T_PEER, D_MODEL), row),
                  pl.BlockSpec((1, D_MODEL), fixed)],
        out_specs=pl.BlockSpec((T_PEER, D_MODEL), row),
        out_shape=jax.ShapeDtypeStruct((t, D_MODEL), F32),
        scratch_shapes=[pltpu.SMEM((T_PEER // 2, nk), jnp.int32), pltpu.SMEM((T_PEER // 2, nk), jnp.int32),
                        pltpu.SemaphoreType.DMA((2,)),
                        pltpu.VMEM((T_PEER, SUBLANES * nk), F32),
                        pltpu.VMEM((T_PEER, D_MODEL), F32)],
        compiler_params=_cparams(("arbitrary",)),
        name="peer_v",
    )(eidx, tab, a, rep, h1, lnf)


def _pack_table(w):
    bits = lax.bitcast_convert_type(w.astype(BF16), jnp.uint16).astype(jnp.uint32)
    sub = lambda i: bits[:, i * LANES:(i + 1) * LANES]
    words = jnp.stack([sub(2 * r) | (sub(2 * r + 1) << 16) for r in range(ROWS_PER_EXPERT)], axis=1)
    return words.reshape(w.shape[0] * ROWS_PER_EXPERT, LANES)


def _pad_lanes(a, left):
    z = jnp.zeros_like(a)
    return jnp.concatenate([a, z] if left else [z, a], axis=-1)


def _layer(h, l, ln1, w_in, conv_w, cmp_pos_k, cmp_pos_v, cmp_k_w1, cmp_k_w2, cmp_v_w1, cmp_v_w2,
           gn_conv, gn_attn, w_out, ln2, peer_wq, peer_subkeys, peer_u, peer_v, ln_f, last_layer):
    b, s, _ = h.shape
    t = b * s
    x2 = h.reshape(t, D_MODEL)
    w = w_in[l]
    o_q = 3 * CONV_CH
    o_kv = o_q + ATTN_W
    o_g = o_kv + 6 * N_KV * HEAD_DIM
    kvw = N_KV * HEAD_DIM
    part = lambda i: w[:, o_kv + i * kvw:o_kv + (i + 1) * kvw]
    pad_heads = lambda a, n: _pad_lanes(a.reshape(D_MODEL, n, HEAD_DIM), True).reshape(D_MODEL, n * LANES)
    w_cat = jnp.concatenate([w[:, :o_q], pad_heads(w[:, o_q:o_kv], N_HEADS), part(0), part(1),
                             pad_heads(part(2), N_KV), pad_heads(part(4), N_KV)], axis=1).astype(BF16)
    w_t = jnp.concatenate([pad_heads(part(3), N_KV).T, pad_heads(part(5), N_KV).T,
                           jnp.pad(w[:, o_g:], ((0, 0), (0, 4 * SUBLANES - N_GATES))).T], axis=0).astype(BF16)
    conv, q, kcvc, kse, kw0, vt, gates_t = _inproj(x2, ln1[l][None, :], w_cat, w_t, s)

    nc = s // CMP_STRIDE

    def chunks(a):
        a = a.reshape(b, nc, CMP_STRIDE, N_KV, HEAD_DIM).transpose(0, 3, 1, 2, 4)
        return a.reshape(b, N_KV, nc, CMP_STRIDE * HEAD_DIM)

    pos2 = lambda p: p.reshape(2, CMP_STRIDE * HEAD_DIM)
    kcc, vcc = _compress(chunks(kcvc[:, :LANES]), chunks(kcvc[:, LANES:]), pos2(cmp_pos_k[l]), pos2(cmp_pos_v[l]),
                         cmp_k_w1[l].astype(BF16), cmp_k_w2[l].astype(BF16),
                         cmp_v_w1[l].astype(BF16), cmp_v_w2[l].astype(BF16))

    attn = _attention(b, s, q, kse, kw0, vt, kcc, vcc, gates_t)

    h1, hn, pq = _post(x2, conv, attn.reshape(t, ATTN_W), conv_w[l], gn_conv[l][None, :], gn_attn[l][None, :],
                           w_out[l].astype(BF16), ln2[l][None, :], peer_wq[l].astype(BF16), s)

    eidx, gate = _topk(pq, peer_subkeys[l].astype(BF16))
    nk = PEER_HEADS * PEER_TOPK
    g8 = (jnp.arange(SUBLANES * nk)[:, None] // SUBLANES == jnp.arange(nk)[None, :]).astype(F32)
    a = _peer_u(eidx, _pack_table(peer_u[l]), hn, gate, g8)
    out = _peer_v(eidx, _pack_table(peer_v[l]), a, g8.T, h1, ln_f[None, :], last_layer)
    return out.reshape(b, s, D_MODEL)


def _attention(b, s, q, kse, kw0, vt, kcc, vcc, gates_t):
    nc = s // CMP_STRIDE
    n_sel = s // SEL_BLOCK
    kc0 = _pad_lanes(kcc, True)
    vct = vcc.transpose(0, 1, 3, 2)
    kse, kw0 = kse.reshape(b, s, N_KV * LANES), kw0.reshape(b, s, N_KV * LANES)
    vt = vt.reshape(b, s // KC, 2 * N_KV, LANES, KC)
    gates_t = gates_t.reshape(b, s // TQ, 4 * SUBLANES, TQ)
    n_cmp = (s - CMP_BLOCK) // CMP_STRIDE + 1
    cs = np.arange(nc) * CMP_STRIDE
    ss = np.arange(HEAD_DIM) * SEL_BLOCK
    ov = ((cs[:, None] < ss[None, :] + SEL_BLOCK) & (cs[:, None] + CMP_BLOCK > ss[None, :])
          & (np.arange(nc)[:, None] < n_cmp) & (np.arange(HEAD_DIM)[None, :] < n_sel))
    ovt = jnp.asarray(ov.T.astype(np.float32))
    return _nsa(q.reshape(b, s, N_HEADS * LANES), kse, kw0, vt, kc0, vct, gates_t, ovt)


def kernel(x, ln1, w_in, conv_w, cmp_pos_k, cmp_pos_v, cmp_k_w1, cmp_k_w2, cmp_v_w1, cmp_v_w2, gn_conv, gn_attn,
           w_out, ln2, peer_wq, peer_subkeys, peer_u, peer_v, ln_f):
    b, s, _ = x.shape
    depth = w_in.shape[0]
    h = x
    for l in range(depth):
        h = _layer(h, l, ln1, w_in, conv_w, cmp_pos_k, cmp_pos_v, cmp_k_w1, cmp_k_w2, cmp_v_w1, cmp_v_w2,
                   gn_conv, gn_attn, w_out, ln2, peer_wq, peer_subkeys, peer_u, peer_v, ln_f, l + 1 == depth)
    return h
```

```python
import functools
import math

import jax
import jax.numpy as jnp
import numpy as np
from jax import lax
from jax.experimental import pallas as pl
from jax.experimental.pallas import tpu as pltpu

F32 = jnp.float32
BF16 = jnp.bfloat16

D_MODEL = 1024
CONV_CH = 512
N_HEADS = 8
HEAD_DIM = 64
N_KV = 2
HPG = N_HEADS // N_KV
ATTN_W = N_HEADS * HEAD_DIM
CMP_BLOCK = 32
CMP_STRIDE = 16
SEL_BLOCK = 64
SEL_TOP = 16
WINDOW = 512
N_GATES = 3 * N_HEADS
PEER_HEADS = 8
PEER_NKEYS = 128
PEER_TOPK = 16
EPS = 1e-6
NEG_INF = -1e30
FORCE = 1e4

LANES = 128
SUBLANES = 8
VMEM_LIMIT = 56 * 1024 * 1024

TQ = 256
KC = 256
SEL_UNROLL = 4
T_PROJ = 256
T_TOPK = 512
T_PEER = 256
assert T_PROJ == KC == TQ
ROWS_PER_EXPERT = D_MODEL // 2 // LANES


def _cparams(sem):
    return pltpu.CompilerParams(dimension_semantics=sem, vmem_limit_bytes=VMEM_LIMIT)


def _dot_nt(a, b, precision=None):
    return lax.dot_general(a, b, (((1,), (1,)), ((), ())), preferred_element_type=F32, precision=precision)


def _dot(a, b):
    return jnp.dot(a, b, preferred_element_type=F32)


def _rms(x, g):
    return x * lax.rsqrt(jnp.mean(x * x, axis=-1, keepdims=True) + EPS) * g


def _dot_01(x, m01):
    m = m01.astype(BF16)
    x1 = x.astype(BF16)
    r1 = x - x1.astype(F32)
    x2 = r1.astype(BF16)
    x3 = (r1 - x2.astype(F32)).astype(BF16)
    return _dot(x1, m) + _dot(x2, m) + _dot(x3, m)


def _gelu(x):
    c = math.sqrt(2.0 / math.pi)
    return 0.5 * x * (1.0 + jnp.tanh(c * (x + 0.044715 * (x * x * x))))


def _inproj_kernel(x_ref, ln1_ref, w_ref, wt_ref, conv_ref, q_ref, kcvc_ref, kse_ref, kw0_ref, vt_ref, gt_ref, *,
                   tiles_per_seq):
    xn = _rms(x_ref[...], ln1_ref[...]).astype(BF16)
    n_conv, n_q, n_k = 3 * CONV_CH, N_HEADS * LANES, N_KV * LANES
    o = 0
    conv_ref[...] = _dot(xn, w_ref[:, o:o + n_conv])
    o += n_conv
    q_ref[...] = (_dot(xn, w_ref[:, o:o + n_q]) * (HEAD_DIM ** -0.5)).astype(BF16)
    o += n_q
    kcvc_ref[...] = _dot(xn, w_ref[:, o:o + n_k])
    o += n_k
    pos = (pl.program_id(0) % tiles_per_seq) * T_PROJ + lax.broadcasted_iota(jnp.int32, (T_PROJ, n_k), 0)
    lane = lax.broadcasted_iota(jnp.int32, (T_PROJ, n_k), 1) & (LANES - 1)
    onehot = (lane - HEAD_DIM) == jnp.right_shift(pos, int(math.log2(SEL_BLOCK)))
    kse_ref[...] = jnp.where(onehot, 1.0, _dot(xn, w_ref[:, o:o + n_k])).astype(BF16)
    o += n_k
    kw0_ref[...] = _dot(xn, w_ref[:, o:o + n_k]).astype(BF16)
    ones_rows = lax.broadcasted_iota(jnp.int32, (LANES, T_PROJ), 0) >= HEAD_DIM
    for j in range(2 * N_KV):
        vt = _dot_nt(wt_ref[j * LANES:(j + 1) * LANES, :], xn)
        vt_ref[0, j] = jnp.where(ones_rows, 1.0, vt).astype(BF16)
    r0 = 2 * N_KV * LANES
    gt_ref[0] = jax.nn.sigmoid(_dot_nt(wt_ref[r0:r0 + 4 * SUBLANES, :], xn))


def _inproj(x2, ln1, w_cat, w_t, seq_len):
    t = x2.shape[0]
    row = lambda i: (i, 0)
    fixed = lambda i: (0, 0)
    n_k = N_KV * LANES
    return pl.pallas_call(
        functools.partial(_inproj_kernel, tiles_per_seq=seq_len // T_PROJ),
        grid=(t // T_PROJ,),
        in_specs=[pl.BlockSpec((T_PROJ, D_MODEL), row),
                  pl.BlockSpec((1, D_MODEL), fixed),
                  pl.BlockSpec(w_cat.shape, fixed),
                  pl.BlockSpec(w_t.shape, fixed)],
        out_specs=[pl.BlockSpec((T_PROJ, 3 * CONV_CH), row),
                   pl.BlockSpec((T_PROJ, N_HEADS * LANES), row),
                   pl.BlockSpec((T_PROJ, n_k), row),
                   pl.BlockSpec((T_PROJ, n_k), row),
                   pl.BlockSpec((T_PROJ, n_k), row),
                   pl.BlockSpec((1, 2 * N_KV, LANES, T_PROJ), lambda i: (i, 0, 0, 0)),
                   pl.BlockSpec((1, 4 * SUBLANES, T_PROJ), lambda i: (i, 0, 0))],
        out_shape=[jax.ShapeDtypeStruct((t, 3 * CONV_CH), F32),
                   jax.ShapeDtypeStruct((t, N_HEADS * LANES), BF16),
                   jax.ShapeDtypeStruct((t, n_k), F32),
                   jax.ShapeDtypeStruct((t, n_k), BF16),
                   jax.ShapeDtypeStruct((t, n_k), BF16),
                   jax.ShapeDtypeStruct((t // T_PROJ, 2 * N_KV, LANES, T_PROJ), BF16),
                   jax.ShapeDtypeStruct((t // T_PROJ, 4 * SUBLANES, T_PROJ), F32)],
        compiler_params=_cparams(("parallel",)),
        name="inproj",
    )(x2, ln1, w_cat, w_t)


def _compress_kernel(ck_ref, cv_ref, posk_ref, posv_ref, w1k_ref, w2k_ref, w1v_ref, w2v_ref, ok_ref, ov_ref):
    half = CMP_STRIDE * HEAD_DIM
    for c_ref, pos_ref, w1_ref, w2_ref, o_ref in ((ck_ref, posk_ref, w1k_ref, w2k_ref, ok_ref),
                                                  (cv_ref, posv_ref, w1v_ref, w2v_ref, ov_ref)):
        for g in range(N_KV):
            c = c_ref[0, g]
            nc = c.shape[0]
            a = _dot((c + pos_ref[0:1, :]).astype(BF16), w1_ref[0:half, :])
            b = _dot((c + pos_ref[1:2, :]).astype(BF16), w1_ref[half:2 * half, :])
            hid = a + pltpu.roll(b, nc - 1, 0)
            out = _dot(_gelu(hid).astype(BF16), w2_ref[...])
            rows = lax.broadcasted_iota(jnp.int32, out.shape, 0)
            o_ref[0, g] = jnp.where(rows < nc - 1, out, 0.0).astype(BF16)


def _compress(ck, cv, posk, posv, w1k, w2k, w1v, w2v):
    b, g, nc, cw = ck.shape
    blk = lambda i: (i, 0, 0, 0)
    fixed = lambda i: (0, 0)
    return pl.pallas_call(
        _compress_kernel,
        grid=(b,),
        in_specs=[pl.BlockSpec((1, g, nc, cw), blk), pl.BlockSpec((1, g, nc, cw), blk),
                  pl.BlockSpec(posk.shape, fixed), pl.BlockSpec(posv.shape, fixed),
                  pl.BlockSpec(w1k.shape, fixed), pl.BlockSpec(w2k.shape, fixed),
                  pl.BlockSpec(w1v.shape, fixed), pl.BlockSpec(w2v.shape, fixed)],
        out_specs=[pl.BlockSpec((1, g, nc, HEAD_DIM), blk), pl.BlockSpec((1, g, nc, HEAD_DIM), blk)],
        out_shape=[jax.ShapeDtypeStruct((b, g, nc, HEAD_DIM), BF16)] * 2,
        compiler_params=_cparams(("parallel",)),
        name="compress",
    )(ck, cv, posk, posv, w1k, w2k, w1v, w2v)


def _nsa_kernel(q_ref, kse_ref, kw0_ref, vt_ref, kc0_ref, vct_ref, gt_ref, ovt_ref, o_ref, m_ref, acc_ref):
    qt = pl.program_id(1)
    t0 = qt * TQ
    nc = kc0_ref.shape[2]
    n_sel = ovt_ref.shape[0]
    w = HPG * TQ
    tq = t0 + (lax.broadcasted_iota(jnp.int32, (1, w), 1) & (TQ - 1))
    krow = lax.broadcasted_iota(jnp.int32, (KC, 1), 0)
    causal = (t0 + krow) <= tq
    band = (t0 - 2 * KC + krow) > (tq - WINDOW)
    eye = (lax.broadcasted_iota(jnp.int32, (TQ, TQ), 0)
           == lax.broadcasted_iota(jnp.int32, (TQ, TQ), 1)).astype(BF16)
    nrow = lax.broadcasted_iota(jnp.int32, (nc, 1), 0)
    valid_c = ((nrow * CMP_STRIDE + (CMP_BLOCK - 1)) <= tq) & (nrow < nc - 1)
    jrow = lax.broadcasted_iota(jnp.int32, (n_sel, TQ), 0)
    qblk = jnp.right_shift(t0 + lax.broadcasted_iota(jnp.int32, (n_sel, TQ), 1), int(math.log2(SEL_BLOCK)))
    forced = (jrow == 0) | (jrow == qblk) | (jrow == qblk - 1)
    lane128 = lax.broadcasted_iota(jnp.int32, (TQ, LANES), 1)
    gt = gt_ref[0, 0]

    def online(s, vt):
        m_old = m_ref[0:1, :]
        m_new = jnp.maximum(m_old, jnp.max(s, axis=0, keepdims=True))
        alpha = jnp.exp(m_old - m_new)
        p = jnp.exp(s - m_new)
        acc_ref[...] = alpha * acc_ref[...] + _dot(vt, p.astype(BF16))
        m_ref[...] = jnp.broadcast_to(m_new, (SUBLANES, w))

    def first(s, vt):
        m = jnp.max(s, axis=0, keepdims=True)
        acc_ref[...] = _dot(vt, jnp.exp(s - m).astype(BF16))
        m_ref[...] = jnp.broadcast_to(m, (SUBLANES, w))

    def finish():
        a = acc_ref[...]
        return a[0:HEAD_DIM, :] / a[HEAD_DIM:HEAD_DIM + 1, :]

    for g in range(N_KV):
        kc0 = kc0_ref[0, g]
        q_all = jnp.concatenate([q_ref[0, :, (g * HPG + h) * LANES:(g * HPG + h + 1) * LANES]
                                 for h in range(HPG)], axis=0)
        s = jnp.where(valid_c, _dot_nt(kc0, q_all), NEG_INF)
        m = jnp.max(s, axis=0, keepdims=True)
        p = jnp.where(valid_c, jnp.exp(s - m), 0.0)
        l = jnp.sum(p, axis=0, keepdims=True)
        pn = p * (1.0 / jnp.where(l > 0.0, l, 1.0))
        o_c = _dot(vct_ref[0, g], pn.astype(BF16))
        psum = pn[:, 0:TQ]
        for h in range(1, HPG):
            psum = psum + pn[:, h * TQ:(h + 1) * TQ]
        imp_t = jnp.dot(ovt_ref[...], psum, preferred_element_type=F32,
                        precision=lax.Precision.HIGHEST)
        val = jnp.where(jrow > qblk, -FORCE, imp_t + jnp.where(forced, FORCE, 0.0))
        rank = jnp.zeros((n_sel, TQ), jnp.int32)
        for k in range(n_sel):
            vk = val[k:k + 1, :]
            ahead = (vk > val) | ((vk == val) & (jrow > k))
            rank = rank + ahead.astype(jnp.int32)
        sel_t = (rank < SEL_TOP).astype(BF16)
        pad_t = jnp.concatenate([jnp.zeros((LANES - n_sel, TQ), BF16), sel_t], axis=0)
        sel_q = _dot_nt(eye, pad_t)
        bias = jnp.where((lane128 >= HEAD_DIM) & (sel_q < 0.5), NEG_INF, 0.0).astype(BF16)
        lhs = q_all + jnp.concatenate([bias] * HPG, axis=0)
        gl = slice(g * LANES, (g + 1) * LANES)
        kd = kse_ref[0, pl.ds(pl.multiple_of(t0, KC), KC), gl]
        first(jnp.where(causal, _dot_nt(kd, lhs), NEG_INF), vt_ref[0, qt, g])

        def scores(c):
            return _dot_nt(kse_ref[0, pl.ds(pl.multiple_of(c * KC, KC), KC), gl], lhs)

        def sel_body(j, carry):
            cs = [SEL_UNROLL * j + u for u in range(SEL_UNROLL)]
            ss = [scores(c) for c in cs]
            for c, s in zip(cs, ss):
                online(s, vt_ref[0, c, g])
            return carry

        def sel_tail(c, carry):
            online(scores(c), vt_ref[0, c, g])
            return carry

        n_main = qt // SEL_UNROLL
        lax.fori_loop(0, n_main, sel_body, 0)
        lax.fori_loop(n_main * SEL_UNROLL, qt, sel_tail, 0)
        o_s = finish()
        c1, c2 = jnp.maximum(qt - 1, 0), jnp.maximum(qt - 2, 0)
        wscores = lambda c: _dot_nt(kw0_ref[0, pl.ds(pl.multiple_of(c * KC, KC), KC), gl], q_all)
        s0 = jnp.where(causal, wscores(qt), NEG_INF)
        s1 = jnp.where(qt >= 1, wscores(c1), NEG_INF)
        s2 = jnp.where(band & (qt >= 2), wscores(c2), NEG_INF)
        first(s0, vt_ref[0, qt, N_KV + g])
        online(s1, vt_ref[0, c1, N_KV + g])
        online(s2, vt_ref[0, c2, N_KV + g])
        o_w = finish()
        outs = []
        for h in range(HPG):
            r = 3 * (g * HPG + h)
            c0, c1 = h * TQ, (h + 1) * TQ
            outs.append(gt[r:r + 1, :] * o_c[:, c0:c1] + gt[r + 1:r + 2, :] * o_s[:, c0:c1]
                        + gt[r + 2:r + 3, :] * o_w[:, c0:c1])
        o_ref[0, :, g * HPG * HEAD_DIM:(g + 1) * HPG * HEAD_DIM] = jnp.concatenate(outs, axis=0).T


def _nsa(q, kse, kw0, vt, kc0, vct, gates_t, ovt):
    b, s, _ = q.shape
    nc = kc0.shape[2]
    seq3 = lambda i, j: (i, 0, 0)
    seq = lambda i, j: (i, 0, 0, 0)
    seq5 = lambda i, j: (i, 0, 0, 0, 0)
    tile = lambda i, j: (i, j, 0)
    fixed = lambda i, j: (0, 0)
    return pl.pallas_call(
        _nsa_kernel,
        grid=(b, s // TQ),
        in_specs=[pl.BlockSpec((1, TQ, N_HEADS * LANES), tile),
                  pl.BlockSpec((1, s, N_KV * LANES), seq3), pl.BlockSpec((1, s, N_KV * LANES), seq3),
                  pl.BlockSpec((1, s // KC, 2 * N_KV, LANES, KC), seq5),
                  pl.BlockSpec((1, N_KV, nc, LANES), seq), pl.BlockSpec((1, N_KV, HEAD_DIM, nc), seq),
                  pl.BlockSpec((1, 1, 4 * SUBLANES, TQ), lambda i, j: (i, j, 0, 0)),
                  pl.BlockSpec(ovt.shape, fixed)],
        out_specs=pl.BlockSpec((1, TQ, ATTN_W), tile),
        out_shape=jax.ShapeDtypeStruct((b, s, ATTN_W), F32),
        scratch_shapes=[pltpu.VMEM((SUBLANES, HPG * TQ), F32), pltpu.VMEM((LANES, HPG * TQ), F32)],
        compiler_params=_cparams(("parallel", "arbitrary")),
        name="nsa",
    )(q, kse, kw0, vt, kc0, vct, gates_t, ovt)


def _post_kernel(x_ref, conv_ref, halo_ref, attn_ref, convw_ref, gnc_ref, gna_ref, wout_ref, ln2_ref, wq_ref,
                 h1_ref, hn_ref, pq_ref, *, tiles_per_seq):
    i = pl.program_id(0)
    c_h = conv_ref[:, 0:CONV_CH]
    c_b = conv_ref[:, CONV_CH:2 * CONV_CH]
    c_c = conv_ref[:, 2 * CONV_CH:3 * CONV_CH]
    z = c_c * c_h
    keep = jnp.where(i % tiles_per_seq == 0, 0.0, 1.0)
    zp = halo_ref[:, 2 * CONV_CH:3 * CONV_CH] * halo_ref[:, 0:CONV_CH] * keep
    rows = lax.broadcasted_iota(jnp.int32, z.shape, 0)
    n = z.shape[0]
    z1 = jnp.where(rows == 0, zp[SUBLANES - 1:SUBLANES, :], pltpu.roll(z, 1, 0))
    z2 = jnp.where(rows == 0, zp[SUBLANES - 2:SUBLANES - 1, :],
                   jnp.where(rows == 1, zp[SUBLANES - 1:SUBLANES, :], pltpu.roll(z, 2, 0)))
    conv = convw_ref[0:1, :] * z2 + convw_ref[1:2, :] * z1 + convw_ref[2:3, :] * z
    nc = _rms(c_b * conv, gnc_ref[...]).astype(BF16)
    na = _rms(attn_ref[...], gna_ref[...]).astype(BF16)
    h1 = x_ref[...] + _dot(nc, wout_ref[0:CONV_CH, :]) + _dot(na, wout_ref[CONV_CH:CONV_CH + ATTN_W, :])
    h1_ref[...] = h1
    hn = _rms(h1, ln2_ref[...])
    hn_ref[...] = hn
    pq_ref[...] = _dot(hn.astype(BF16), wq_ref[...]).astype(BF16)


def _post(x2, conv, attn2, convw, gnc, gna, wout, ln2, wq, seq_len):
    t = x2.shape[0]
    row = lambda i: (i, 0)
    fixed = lambda i: (0, 0)
    halo = lambda i: (jnp.maximum(i * (T_PROJ // SUBLANES) - 1, 0), 0)
    nq = wq.shape[1]
    return pl.pallas_call(
        functools.partial(_post_kernel, tiles_per_seq=seq_len // T_PROJ),
        grid=(t // T_PROJ,),
        in_specs=[pl.BlockSpec((T_PROJ, D_MODEL), row),
                  pl.BlockSpec((T_PROJ, 3 * CONV_CH), row),
                  pl.BlockSpec((SUBLANES, 3 * CONV_CH), halo),
                  pl.BlockSpec((T_PROJ, ATTN_W), row),
                  pl.BlockSpec(convw.shape, fixed), pl.BlockSpec(gnc.shape, fixed), pl.BlockSpec(gna.shape, fixed),
                  pl.BlockSpec(wout.shape, fixed), pl.BlockSpec(ln2.shape, fixed), pl.BlockSpec(wq.shape, fixed)],
        out_specs=[pl.BlockSpec((T_PROJ, D_MODEL), row), pl.BlockSpec((T_PROJ, D_MODEL), row),
                   pl.BlockSpec((T_PROJ, nq), row)],
        out_shape=[jax.ShapeDtypeStruct((t, D_MODEL), F32), jax.ShapeDtypeStruct((t, D_MODEL), F32),
                   jax.ShapeDtypeStruct((t, nq), BF16)],
        compiler_params=_cparams(("parallel",)),
        name="post",
    )(x2, conv, conv, attn2, convw, gnc, gna, wout, ln2, wq)


def _staircase():
    return [(a, b) for a in range(PEER_TOPK) for b in range(PEER_TOPK) if (a + 1) * (b + 1) <= PEER_TOPK]


N_CAND = 56


def _topk_kernel(pq_ref, sk_ref, e_ref, g_ref, sv_ref, si_ref, cand_ref, ce_ref, et_ref, gt_ref):
    tt = pq_ref.shape[0]
    rown = lax.broadcasted_iota(jnp.int32, (PEER_NKEYS, tt), 0).astype(F32)
    rowc = lax.broadcasted_iota(jnp.int32, (N_CAND, tt), 0).astype(F32)
    pairs = _staircase()
    for h in range(PEER_HEADS):
        for c in range(2):
            off = (h * 2 + c) * PEER_NKEYS
            x = _dot_nt(sk_ref[h, c], pq_ref[:, off:off + PEER_NKEYS])
            for it in range(PEER_TOPK):
                m = jnp.max(x, axis=0, keepdims=True)
                idx = jnp.min(jnp.where(x == m, rown, float(PEER_NKEYS)), axis=0, keepdims=True)
                sv_ref[c, it:it + 1, :] = m
                si_ref[c, it:it + 1, :] = idx
                x = jnp.where(rown == idx, -jnp.inf, x)
        cand_ref[...] = jnp.full((N_CAND, tt), -jnp.inf, F32)
        ce_ref[...] = jnp.zeros((N_CAND, tt), F32)
        for r, (a, b) in enumerate(pairs):
            cand_ref[r:r + 1, :] = sv_ref[0, a:a + 1, :] + sv_ref[1, b:b + 1, :]
            ce_ref[r:r + 1, :] = si_ref[0, a:a + 1, :] * float(PEER_NKEYS) + si_ref[1, b:b + 1, :]
        x = cand_ref[...]
        ce = ce_ref[...]
        best = []
        for it in range(PEER_TOPK):
            m = jnp.max(x, axis=0, keepdims=True)
            idx = jnp.min(jnp.where(x == m, rowc, float(N_CAND)), axis=0, keepdims=True)
            hit = rowc == idx
            et_ref[h * PEER_TOPK + it:h * PEER_TOPK + it + 1, :] = jnp.max(jnp.where(hit, ce, -1.0), axis=0, keepdims=True)
            best.append(m)
            x = jnp.where(hit, -jnp.inf, x)
        ex = [jnp.exp(v - best[0]) for v in best]
        tot = ex[0]
        for v in ex[1:]:
            tot = tot + v
        inv = 1.0 / tot
        for it in range(PEER_TOPK):
            gt_ref[h * PEER_TOPK + it:h * PEER_TOPK + it + 1, :] = ex[it] * inv
    e_ref[...] = (et_ref[...] * float(ROWS_PER_EXPERT)).astype(jnp.int32).T
    g_ref[...] = gt_ref[...].T


def _topk(pq, sk):
    t, nq = pq.shape
    nk = PEER_HEADS * PEER_TOPK
    row = lambda i: (i, 0)
    return pl.pallas_call(
        _topk_kernel,
        grid=(t // T_TOPK,),
        in_specs=[pl.BlockSpec((T_TOPK, nq), row), pl.BlockSpec(sk.shape, lambda i: (0, 0, 0, 0))],
        out_specs=[pl.BlockSpec((T_TOPK, nk), row), pl.BlockSpec((T_TOPK, nk), row)],
        out_shape=[jax.ShapeDtypeStruct((t, nk), jnp.int32), jax.ShapeDtypeStruct((t, nk), F32)],
        scratch_shapes=[pltpu.VMEM((2, PEER_TOPK, T_TOPK), F32), pltpu.VMEM((2, PEER_TOPK, T_TOPK), F32),
                        pltpu.VMEM((N_CAND, T_TOPK), F32), pltpu.VMEM((N_CAND, T_TOPK), F32),
                        pltpu.VMEM((nk, T_TOPK), F32), pltpu.VMEM((nk, T_TOPK), F32)],
        compiler_params=_cparams(("parallel",)),
        name="topk",
    )(pq, sk)


def _gather_rows(e_ref, tab_ref, t):
    rows = []
    for k in range(e_ref.shape[1]):
        e0 = pl.multiple_of(e_ref[t, k], ROWS_PER_EXPERT)
        rows.append(tab_ref[pl.ds(e0, ROWS_PER_EXPERT), :])
    return pltpu.bitcast(jnp.concatenate(rows, axis=0), BF16)


def _for_each_token(e_hbm, e_refs, sem, tab_ref, compute):
    i = pl.program_id(0)
    half = e_refs[0].shape[0]

    def copy(step, part):
        rows = pl.ds(pl.multiple_of((2 * step + part) * half, half), half)
        return pltpu.make_async_copy(e_hbm.at[rows], e_refs[part], sem.at[part])

    @pl.when(i == 0)
    def _():
        copy(0, 0).start()

    copy(i, 1).start()
    copy(i, 0).wait()
    w_next = _gather_rows(e_refs[0], tab_ref, 0)
    for t in range(2 * half):
        w_cur = w_next
        if t + 1 == half:
            @pl.when(i + 1 < pl.num_programs(0))
            def _():
                copy(i + 1, 0).start()

            copy(i, 1).wait()
        if t + 1 < 2 * half:
            w_next = _gather_rows(e_refs[(t + 1) // half], tab_ref, (t + 1) % half)
        compute(t, w_cur)


def _diag16(n):
    lane = lax.broadcasted_iota(jnp.int32, (2 * SUBLANES, n), 1)
    row = lax.broadcasted_iota(jnp.int32, (2 * SUBLANES, n), 0)
    return (lane & (SUBLANES - 1)) == (row & (SUBLANES - 1)), row < SUBLANES


def _peer_u_kernel(e_hbm, tab_ref, x_ref, gate_ref, g8_ref, a_ref, e0_ref, e1_ref, sem, hs_ref):
    diag, _ = _diag16(SUBLANES * e0_ref.shape[1])
    top = lax.broadcasted_iota(jnp.int32, (2 * SUBLANES, LANES), 0) < SUBLANES

    def token(t, w):
        parts = [x_ref[t:t + 1, i * LANES:(i + 1) * LANES] for i in range(SUBLANES)]
        xx = jnp.concatenate(parts + parts, axis=0)
        hi = xx.astype(BF16).astype(F32)
        x16 = jnp.where(top, hi, xx - hi).astype(BF16)
        r = _dot_nt(x16, w)
        hs_ref[t:t + 1, :] = jnp.sum(jnp.where(diag, r, 0.0), axis=0, keepdims=True)

    _for_each_token(e_hbm, (e0_ref, e1_ref), sem, tab_ref, token)
    h = _dot_01(hs_ref[...], g8_ref[...])
    a_ref[...] = gate_ref[...] * _gelu(h)


def _peer_u(eidx, tab, hn, gate, g8):
    t, nk = eidx.shape
    row = lambda i: (i, 0)
    fixed = lambda i: (0, 0)
    return pl.pallas_call(
        _peer_u_kernel,
        grid=(t // T_PEER,),
        in_specs=[pl.BlockSpec(memory_space=pl.ANY),
                  pl.BlockSpec(tab.shape, fixed, pipeline_mode=pl.Buffered(1)),
                  pl.BlockSpec((T_PEER, D_MODEL), row),
                  pl.BlockSpec((T_PEER, nk), row),
                  pl.BlockSpec(g8.shape, fixed)],
        out_specs=pl.BlockSpec((T_PEER, nk), row),
        out_shape=jax.ShapeDtypeStruct((t, nk), F32),
        scratch_shapes=[pltpu.SMEM((T_PEER // 2, nk), jnp.int32), pltpu.SMEM((T_PEER // 2, nk), jnp.int32),
                        pltpu.SemaphoreType.DMA((2,)),
                        pltpu.VMEM((T_PEER, SUBLANES * nk), F32)],
        compiler_params=_cparams(("arbitrary",)),
        name="peer_u",
    )(eidx, tab, hn, gate, g8)


def _peer_v_kernel(e_hbm, tab_ref, a_ref, rep_ref, h1_ref, lnf_ref, o_ref, e0_ref, e1_ref, sem, arep_ref, po_ref, *,
                   last_layer):
    n = SUBLANES * e0_ref.shape[1]
    diag, top = _diag16(n)
    arep_ref[...] = _dot_01(a_ref[...], rep_ref[...])

    def token(t, w):
        l32 = jnp.where(diag, jnp.broadcast_to(arep_ref[t:t + 1, :], (2 * SUBLANES, n)), 0.0)
        hi = l32.astype(BF16).astype(F32)
        lhs = jnp.where(top, hi, l32 - hi).astype(BF16)
        out = _dot(lhs, w)
        out = out[0:SUBLANES] + out[SUBLANES:2 * SUBLANES]
        for i in range(SUBLANES):
            po_ref[t:t + 1, i * LANES:(i + 1) * LANES] = out[i:i + 1, :]

    _for_each_token(e_hbm, (e0_ref, e1_ref), sem, tab_ref, token)
    h = h1_ref[...] + po_ref[...]
    o_ref[...] = _rms(h, lnf_ref[...]) if last_layer else h


def _peer_v(eidx, tab, a, rep, h1, lnf, last_layer):
    t, nk = eidx.shape
    row = lambda i: (i, 0)
    fixed = lambda i: (0, 0)
    return pl.pallas_call(
        functools.partial(_peer_v_kernel, last_layer=last_layer),
        grid=(t // T_PEER,),
        in_specs=[pl.BlockSpec(memory_space=pl.ANY),
                  pl.BlockSpec(tab.shape, fixed, pipeline_mode=pl.Buffered(1)),
                  pl.BlockSpec((T_PEER, nk), row),
                  pl.BlockSpec(rep.shape, fixed),
                  pl.BlockSpec((T_PEER, D_MODEL), row),
                  pl.BlockSpec((1, D_MODEL), fixed)],
        out_specs=pl.BlockSpec((T_PEER, D_MODEL), row),
        out_shape=jax.ShapeDtypeStruct((t, D_MODEL), F32),
        scratch_shapes=[pltpu.SMEM((T_PEER // 2, nk), jnp.int32), pltpu.SMEM((T_PEER // 2, nk), jnp.int32),
                        pltpu.SemaphoreType.DMA((2,)),
                        pltpu.VMEM((T_PEER, SUBLANES * nk), F32),
                        pltpu.VMEM((T_PEER, D_MODEL), F32)],
        compiler_params=_cparams(("arbitrary",)),
        name="peer_v",
    )(eidx, tab, a, rep, h1, lnf)


def _pack_table(w):
    bits = lax.bitcast_convert_type(w.astype(BF16), jnp.uint16).astype(jnp.uint32)
    sub = lambda i: bits[:, i * LANES:(i + 1) * LANES]
    words = jnp.stack([sub(2 * r) | (sub(2 * r + 1) << 16) for r in range(ROWS_PER_EXPERT)], axis=1)
    return words.reshape(w.shape[0] * ROWS_PER_EXPERT, LANES)


def _pad_lanes(a, left):
    z = jnp.zeros_like(a)
    return jnp.concatenate([a, z] if left else [z, a], axis=-1)


def _layer(h, l, ln1, w_in, conv_w, cmp_pos_k, cmp_pos_v, cmp_k_w1, cmp_k_w2, cmp_v_w1, cmp_v_w2,
           gn_conv, gn_attn, w_out, ln2, peer_wq, peer_subkeys, peer_u, peer_v, ln_f, last_layer):
    b, s, _ = h.shape
    t = b * s
    x2 = h.reshape(t, D_MODEL)
    w = w_in[l]
    o_q = 3 * CONV_CH
    o_kv = o_q + ATTN_W
    o_g = o_kv + 6 * N_KV * HEAD_DIM
    kvw = N_KV * HEAD_DIM
    part = lambda i: w[:, o_kv + i * kvw:o_kv + (i + 1) * kvw]
    pad_heads = lambda a, n: _pad_lanes(a.reshape(D_MODEL, n, HEAD_DIM), True).reshape(D_MODEL, n * LANES)
    w_cat = jnp.concatenate([w[:, :o_q], pad_heads(w[:, o_q:o_kv], N_HEADS), part(0), part(1),
                             pad_heads(part(2), N_KV), pad_heads(part(4), N_KV)], axis=1).astype(BF16)
    w_t = jnp.concatenate([pad_heads(part(3), N_KV).T, pad_heads(part(5), N_KV).T,
                           jnp.pad(w[:, o_g:], ((0, 0), (0, 4 * SUBLANES - N_GATES))).T], axis=0).astype(BF16)
    conv, q, kcvc, kse, kw0, vt, gates_t = _inproj(x2, ln1[l][None, :], w_cat, w_t, s)

    nc = s // CMP_STRIDE

    def chunks(a):
        a = a.reshape(b, nc, CMP_STRIDE, N_KV, HEAD_DIM).transpose(0, 3, 1, 2, 4)
        return a.reshape(b, N_KV, nc, CMP_STRIDE * HEAD_DIM)

    pos2 = lambda p: p.reshape(2, CMP_STRIDE * HEAD_DIM)
    kcc, vcc = _compress(chunks(kcvc[:, :LANES]), chunks(kcvc[:, LANES:]), pos2(cmp_pos_k[l]), pos2(cmp_pos_v[l]),
                         cmp_k_w1[l].astype(BF16), cmp_k_w2[l].astype(BF16),
                         cmp_v_w1[l].astype(BF16), cmp_v_w2[l].astype(BF16))

    attn = _attention(b, s, q, kse, kw0, vt, kcc, vcc, gates_t)

    h1, hn, pq = _post(x2, conv, attn.reshape(t, ATTN_W), conv_w[l], gn_conv[l][None, :], gn_attn[l][None, :],
                           w_out[l].astype(BF16), ln2[l][None, :], peer_wq[l].astype(BF16), s)

    eidx, gate = _topk(pq, peer_subkeys[l].astype(BF16))
    nk = PEER_HEADS * PEER_TOPK
    g8 = (jnp.arange(SUBLANES * nk)[:, None] // SUBLANES == jnp.arange(nk)[None, :]).astype(F32)
    a = _peer_u(eidx, _pack_table(peer_u[l]), hn, gate, g8)
    out = _peer_v(eidx, _pack_table(peer_v[l]), a, g8.T, h1, ln_f[None, :], last_layer)
    return out.reshape(b, s, D_MODEL)


def _attention(b, s, q, kse, kw0, vt, kcc, vcc, gates_t):
    nc = s // CMP_STRIDE
    n_sel = s // SEL_BLOCK
    kc0 = _pad_lanes(kcc, True)
    vct = vcc.transpose(0, 1, 3, 2)
    kse, kw0 = kse.reshape(b, s, N_KV * LANES), kw0.reshape(b, s, N_KV * LANES)
    vt = vt.reshape(b, s // KC, 2 * N_KV, LANES, KC)
    gates_t = gates_t.reshape(b, s // TQ, 4 * SUBLANES, TQ)
    n_cmp = (s - CMP_BLOCK) // CMP_STRIDE + 1
    cs = np.arange(nc) * CMP_STRIDE
    ss = np.arange(HEAD_DIM) * SEL_BLOCK
    ov = ((cs[:, None] < ss[None, :] + SEL_BLOCK) & (cs[:, None] + CMP_BLOCK > ss[None, :])
          & (np.arange(nc)[:, None] < n_cmp) & (np.arange(HEAD_DIM)[None, :] < n_sel))
    ovt = jnp.asarray(ov.T.astype(np.float32))
    return _nsa(q.reshape(b, s, N_HEADS * LANES), kse, kw0, vt, kc0, vct, gates_t, ovt)


def kernel(x, ln1, w_in, conv_w, cmp_pos_k, cmp_pos_v, cmp_k_w1, cmp_k_w2, cmp_v_w1, cmp_v_w2, gn_conv, gn_attn,
           w_out, ln2, peer_wq, peer_subkeys, peer_u, peer_v, ln_f):
    b, s, _ = x.shape
    depth = w_in.shape[0]
    h = x
    for l in range(depth):
        h = _layer(h, l, ln1, w_in, conv_w, cmp_pos_k, cmp_pos_v, cmp_k_w1, cmp_k_w2, cmp_v_w1, cmp_v_w2,
                   gn_conv, gn_attn, w_out, ln2, peer_wq, peer_subkeys, peer_u, peer_v, ln_f, l + 1 == depth)
    return h
```
